```python
import jax, jax.numpy as jnp
from jax import lax
import numpy as np

D_MODEL = 1024
BATCH = 4
SEQ = 8192
DEPTH = 1
DEC_BATCH = 32
DEC_SEQ = 64
PAST_LEN = 2048

CHUNK = 64
Q_BLOCK = 128
N_FOX_HEADS = 16
FOX_HEAD_DIM = 64
FOX_WIDTH = N_FOX_HEADS * FOX_HEAD_DIM
N_RET_HEADS = 4
RET_KEY_DIM = 256
RET_VAL_DIM = 512
RET_QK_WIDTH = N_RET_HEADS * RET_KEY_DIM
RET_V_WIDTH = N_RET_HEADS * RET_VAL_DIM
MIX_WIDTH = FOX_WIDTH + RET_V_WIDTH
D_FF = 4 * D_MODEL
ROPE_BASE = 10000.0
EPS = 1e-6
B_FORGET_INIT = 3.0
IN_OFFSETS = (
    FOX_WIDTH,
    2 * FOX_WIDTH,
    3 * FOX_WIDTH,
    3 * FOX_WIDTH + N_FOX_HEADS,
    3 * FOX_WIDTH + N_FOX_HEADS + RET_QK_WIDTH,
    3 * FOX_WIDTH + N_FOX_HEADS + 2 * RET_QK_WIDTH,
    3 * FOX_WIDTH + N_FOX_HEADS + 2 * RET_QK_WIDTH + RET_V_WIDTH,
    3 * FOX_WIDTH + N_FOX_HEADS + 2 * RET_QK_WIDTH + 2 * RET_V_WIDTH,
    3 * FOX_WIDTH + N_FOX_HEADS + 2 * RET_QK_WIDTH + 2 * RET_V_WIDTH + D_MODEL,
)
IN_WIDTH = 3 * FOX_WIDTH + N_FOX_HEADS + 2 * RET_QK_WIDTH + 2 * RET_V_WIDTH + 2 * D_MODEL

kernel_name = "hybrid_fox_retention_stream_step"


def _rmsnorm(x, g):
    xf = x.astype(jnp.float32)
    xf = xf * lax.rsqrt(jnp.mean(xf * xf, axis=-1, keepdims=True) + EPS)
    return (xf * g.astype(jnp.float32)).astype(x.dtype)


def _rotary(x, pos):
    d = x.shape[-1]
    inv_freq = ROPE_BASE ** (-jnp.arange(0, d, 2, dtype=jnp.float32) / d)
    ang = pos.astype(jnp.float32)[:, None] * inv_freq[None, :]
    cos = jnp.cos(ang)[None, :, None, :]
    sin = jnp.sin(ang)[None, :, None, :]
    xf = x.astype(jnp.float32)
    x1, x2 = xf[..., : d // 2], xf[..., d // 2:]
    return jnp.concatenate([x1 * cos - x2 * sin, x1 * sin + x2 * cos], axis=-1).astype(x.dtype)


def _ret_log_gamma():
    return jnp.log(1.0 - 2.0 ** (-5.0 - jnp.arange(N_RET_HEADS, dtype=jnp.float32)))


def _project(h, pos, w_in, b_forget):
    b, t = h.shape[0], h.shape[1]
    z = h @ w_in
    zq_f, zk_f, zv_f, zf, zq_r, zk_r, zv_r, zg_r, zga, zgb = jnp.split(z, IN_OFFSETS, axis=-1)
    q_f = zq_f.reshape(b, t, N_FOX_HEADS, FOX_HEAD_DIM)
    k_f = zk_f.reshape(b, t, N_FOX_HEADS, FOX_HEAD_DIM)
    v_f = zv_f.reshape(b, t, N_FOX_HEADS, FOX_HEAD_DIM)
    logf = jax.nn.log_sigmoid((zf + b_forget).astype(jnp.float32))
    q_r = _rotary(zq_r.reshape(b, t, N_RET_HEADS, RET_KEY_DIM), pos)
    k_r = _rotary(zk_r.reshape(b, t, N_RET_HEADS, RET_KEY_DIM), pos) * (RET_KEY_DIM ** -0.5)
    v_r = zv_r.reshape(b, t, N_RET_HEADS, RET_VAL_DIM)
    return q_f, k_f, v_f, logf, q_r, k_r, v_r, zg_r, zga, zgb


def _fox_prompt(q, k, v, logf):
    b, t, h, d = q.shape
    nb = t // Q_BLOCK
    scale = d ** -0.5
    c_t = jnp.cumsum(logf, axis=1).transpose(0, 2, 1)
    kpos = jnp.arange(t)
    qb = q.reshape(b, nb, Q_BLOCK, h, d).transpose(1, 0, 2, 3, 4)
    cb = c_t.reshape(b, h, nb, Q_BLOCK).transpose(2, 0, 1, 3)

    def block(args):
        qi, ci, bi = args
        s = jnp.einsum('bqhd,bkhd->bhqk', qi, k, preferred_element_type=jnp.float32) * scale
        s = s + (ci[..., :, None] - c_t[:, :, None, :])
        qpos = bi * Q_BLOCK + jnp.arange(Q_BLOCK)
        s = jnp.where(kpos[None, :] <= qpos[:, None], s, -jnp.inf)
        p = jax.nn.softmax(s, axis=-1)
        return jnp.einsum('bhqk,bkhd->bqhd', p.astype(v.dtype), v)

    o = lax.map(block, (qb, cb, jnp.arange(nb)))
    return o.transpose(1, 0, 2, 3, 4).reshape(b, t, h, d)


def _fox_sample(q, k, v, logf, cache_k, cache_v, cache_logf):
    b, l, h, d = q.shape
    p_len = cache_k.shape[1]
    scale = d ** -0.5
    k_all = jnp.concatenate([cache_k.astype(k.dtype), k], axis=1)
    v_all = jnp.concatenate([cache_v.astype(v.dtype), v], axis=1)
    logf_all = jnp.concatenate([cache_logf.astype(jnp.float32), logf], axis=1)
    c_t = jnp.cumsum(logf_all, axis=1).transpose(0, 2, 1)
    s = jnp.einsum('bqhd,bkhd->bhqk', q, k_all, preferred_element_type=jnp.float32) * scale
    s = s + (c_t[:, :, p_len:, None] - c_t[:, :, None, :])
    kpos = jnp.arange(p_len + l)
    qpos = p_len + jnp.arange(l)
    s = jnp.where(kpos[None, :] <= qpos[:, None], s, -jnp.inf)
    p = jax.nn.softmax(s, axis=-1)
    return jnp.einsum('bhqk,bkhd->bqhd', p.astype(v.dtype), v_all)


def _ret_chunk(state, qc, kc, vc, lg):
    state = state.astype(jnp.float32)
    l = qc.shape[1]
    idx = jnp.arange(l, dtype=jnp.float32)
    dist = jnp.abs(idx[:, None] - idx[None, :])
    decay = jnp.exp(lg[:, None, None] * dist[None])
    q_dec = jnp.exp(lg[:, None] * (idx + 1.0)[None])
    k_dec = jnp.exp(lg[:, None] * (l - 1.0 - idx)[None])
    qf, kf, vf = qc.astype(jnp.float32), kc.astype(jnp.float32), vc.astype(jnp.float32)
    s = jnp.einsum('bnhk,bmhk->bhnm', qf, kf) * decay[None]
    o = jnp.einsum('bhnm,bmhv->bnhv', s, vf) + jnp.einsum('bnhk,hn,bhkv->bnhv', qf, q_dec, state)
    new_state = jnp.exp(lg * l)[None, :, None, None] * state + jnp.einsum('bmhk,hm,bmhv->bhkv', kf, k_dec, vf)
    return new_state, o


def _ret_prompt(q, k, v, lg):
    b, t = q.shape[0], q.shape[1]
    n = t // CHUNK

    def to_chunks(a):
        return a.reshape(b, n, CHUNK, a.shape[2], a.shape[3]).transpose(1, 0, 2, 3, 4)

    s0 = jnp.zeros((b, N_RET_HEADS, RET_KEY_DIM, RET_VAL_DIM), jnp.float32)
    s_fin, o = lax.scan(lambda s, xs: _ret_chunk(s, xs[0], xs[1], xs[2], lg), s0,
                        (to_chunks(q), to_chunks(k), to_chunks(v)))
    o = o.transpose(1, 0, 2, 3, 4).reshape(b, t, N_RET_HEADS, RET_VAL_DIM)
    return o, s_fin


def _mixer_output(o_fox, y_ret, g_r, gate_a, gate_b, g_ret_norm, w_branch, w_out):
    b, t = o_fox.shape[0], o_fox.shape[1]
    yf = y_ret.astype(jnp.float32)
    yn = yf * lax.rsqrt(jnp.mean(yf * yf, axis=-1, keepdims=True) + EPS)
    yn = yn * g_ret_norm.astype(jnp.float32).reshape(N_RET_HEADS, RET_VAL_DIM)
    o_b = (jax.nn.silu(g_r.astype(jnp.float32)) * yn.reshape(b, t, RET_V_WIDTH)).astype(g_r.dtype)
    o_a = o_fox.reshape(b, t, FOX_WIDTH)
    merged = jax.nn.sigmoid(gate_a) * (o_a @ w_branch[:FOX_WIDTH]) + jax.nn.sigmoid(gate_b) * (o_b @ w_branch[FOX_WIDTH:])
    return merged @ w_out


def _mlp(h, w_up, w_down):
    return jnp.square(jax.nn.relu(h @ w_up)) @ w_down


def setup_inputs(seed: int = 0) -> dict:
    key = jax.random.key(seed)
    ks = jax.random.split(key, 20)
    nrm = jax.random.normal
    f32 = jnp.float32
    return {
        "x_prompt": nrm(ks[0], (BATCH, SEQ, D_MODEL), f32),
        "x_sample": nrm(ks[1], (DEC_BATCH, DEC_SEQ, D_MODEL), f32),
        "cache_fox_k": nrm(ks[2], (DEPTH, DEC_BATCH, PAST_LEN, N_FOX_HEADS, FOX_HEAD_DIM), f32),
        "cache_fox_v": nrm(ks[3], (DEPTH, DEC_BATCH, PAST_LEN, N_FOX_HEADS, FOX_HEAD_DIM), f32),
        "cache_fox_logf": jax.nn.log_sigmoid(B_FORGET_INIT + nrm(ks[4], (DEPTH, DEC_BATCH, PAST_LEN, N_FOX_HEADS), f32)),
        "state_ret": 0.25 * nrm(ks[5], (DEPTH, DEC_BATCH, N_RET_HEADS, RET_KEY_DIM, RET_VAL_DIM), f32),
        "g_attn": 1.0 + 0.01 * nrm(ks[6], (DEPTH, D_MODEL), f32),
        "w_in": nrm(ks[7], (DEPTH, D_MODEL, IN_WIDTH), f32) * (D_MODEL ** -0.5),
        "b_forget": B_FORGET_INIT + 0.1 * nrm(ks[8], (DEPTH, N_FOX_HEADS), f32),
        "g_ret_norm": 1.0 + 0.01 * nrm(ks[9], (DEPTH, RET_V_WIDTH), f32),
        "w_branch": jnp.concatenate([
            nrm(ks[10], (DEPTH, FOX_WIDTH, D_MODEL), f32) * (FOX_WIDTH ** -0.5),
            nrm(ks[11], (DEPTH, RET_V_WIDTH, D_MODEL), f32) * (RET_V_WIDTH ** -0.5)], axis=1),
        "w_out": nrm(ks[12], (DEPTH, D_MODEL, D_MODEL), f32) * (D_MODEL ** -0.5),
        "g_mlp": 1.0 + 0.01 * nrm(ks[13], (DEPTH, D_MODEL), f32),
        "w_up": nrm(ks[14], (DEPTH, D_MODEL, D_FF), f32) * (D_MODEL ** -0.5),
        "w_down": nrm(ks[15], (DEPTH, D_FF, D_MODEL), f32) * (D_FF ** -0.5),
        "g_final": 1.0 + 0.01 * nrm(ks[16], (D_MODEL,), f32),
    }


def reference(x_prompt, x_sample, cache_fox_k, cache_fox_v, cache_fox_logf, state_ret,
              g_attn, w_in, b_forget, g_ret_norm, w_branch, w_out, g_mlp, w_up, w_down, g_final):
    p_len = cache_fox_k.shape[2]
    pos_p = jnp.arange(x_prompt.shape[1])
    pos_s = p_len + jnp.arange(x_sample.shape[1])
    lg = _ret_log_gamma()
    xp, xs = x_prompt, x_sample
    kp_l, vp_l, fp_l, sp_l, ks_l, vs_l, fs_l, ss_l = [], [], [], [], [], [], [], []
    for layer in range(DEPTH):
        hp = _rmsnorm(xp, g_attn[layer])
        q_f, k_f, v_f, lf, q_r, k_r, v_r, g_r, ga, gb = _project(hp, pos_p, w_in[layer], b_forget[layer])
        o_f = _fox_prompt(q_f, k_f, v_f, lf)
        y_r, s_r = _ret_prompt(q_r, k_r, v_r, lg)
        xp = xp + _mixer_output(o_f, y_r, g_r, ga, gb, g_ret_norm[layer], w_branch[layer], w_out[layer])
        xp = xp + _mlp(_rmsnorm(xp, g_mlp[layer]), w_up[layer], w_down[layer])
        kp_l.append(k_f); vp_l.append(v_f); fp_l.append(lf); sp_l.append(s_r)
        hs = _rmsnorm(xs, g_attn[layer])
        q_f, k_f, v_f, lf, q_r, k_r, v_r, g_r, ga, gb = _project(hs, pos_s, w_in[layer], b_forget[layer])
        o_f = _fox_sample(q_f, k_f, v_f, lf, cache_fox_k[layer], cache_fox_v[layer], cache_fox_logf[layer])
        s_r, y_r = _ret_chunk(state_ret[layer], q_r, k_r, v_r, lg)
        xs = xs + _mixer_output(o_f, y_r, g_r, ga, gb, g_ret_norm[layer], w_branch[layer], w_out[layer])
        xs = xs + _mlp(_rmsnorm(xs, g_mlp[layer]), w_up[layer], w_down[layer])
        ks_l.append(k_f); vs_l.append(v_f); fs_l.append(lf); ss_l.append(s_r)
    y_prompt = _rmsnorm(xp, g_final)
    y_sample = _rmsnorm(xs, g_final)
    new_fox_k_prompt = jnp.stack(kp_l)
    new_fox_v_prompt = jnp.stack(vp_l)
    new_fox_logf_prompt = jnp.stack(fp_l)
    new_state_ret_prompt = jnp.stack(sp_l)
    new_fox_k_sample = jnp.stack(ks_l)
    new_fox_v_sample = jnp.stack(vs_l)
    new_fox_logf_sample = jnp.stack(fs_l)
    new_state_ret_sample = jnp.stack(ss_l)
    return (y_prompt, y_sample, new_fox_k_prompt, new_fox_v_prompt, new_fox_logf_prompt, new_state_ret_prompt,
            new_fox_k_sample, new_fox_v_sample, new_fox_logf_sample, new_state_ret_sample)
```

```python
import functools

import jax
import jax.numpy as jnp
from jax import lax
from jax.experimental import pallas as pl
from jax.experimental.pallas import tpu as pltpu

D_MODEL = 1024
N_FOX_HEADS = 16
FOX_HEAD_DIM = 64
FOX_WIDTH = N_FOX_HEADS * FOX_HEAD_DIM
N_RET_HEADS = 4
RET_KEY_DIM = 256
RET_VAL_DIM = 512
RET_QK_WIDTH = N_RET_HEADS * RET_KEY_DIM
RET_V_WIDTH = N_RET_HEADS * RET_VAL_DIM
D_FF = 4 * D_MODEL
CHUNK = 64
ROPE_BASE = 10000.0
EPS = 1e-6

LANES = 128
HEADS_PER_LANE_BLOCK = LANES // FOX_HEAD_DIM
N_HEAD_PAIRS = N_FOX_HEADS // HEADS_PER_LANE_BLOCK
N_C_PIECES = 3
ONES_ROWS = 16
MASK_VALUE = -1e30
VMEM_LIMIT = 56 * 1024 * 1024

TOKEN_TILE = 512
MIX_TOKEN_TILE = 256
FOX_BLOCK = 256
RET_BLOCK = 256
CUMSUM_TILE_PROMPT = 512
CUMSUM_TILE_SAMPLE = 704

BF16 = jnp.bfloat16
F32 = jnp.float32


def _cparams(n_axes):
    return pltpu.CompilerParams(dimension_semantics=("arbitrary",) * n_axes,
                                vmem_limit_bytes=VMEM_LIMIT)


def _dot(a, b):
    return jnp.dot(a, b, preferred_element_type=F32)


def _dot_nt(a, b):
    return lax.dot_general(a, b, (((1,), (1,)), ((), ())), preferred_element_type=F32)


def _rmsnorm(x, g):
    return x * lax.rsqrt(jnp.mean(x * x, axis=-1, keepdims=True) + EPS) * g


def _split3(x):
    a = x.astype(BF16)
    r = x - a.astype(F32)
    b = r.astype(BF16)
    c = (r - b.astype(F32)).astype(BF16)
    return a, b, c


def _proj_fox_kernel(x_ref, g_ref, w_ref, b_ref, q_ref, k_ref, v_ref, lf_ref, lfp_ref):
    h = _rmsnorm(x_ref[...], g_ref[...]).astype(BF16)
    q_ref[...] = (_dot(h, w_ref[:, 0:FOX_WIDTH]) * (FOX_HEAD_DIM ** -0.5)).astype(BF16)
    k_ref[...] = _dot(h, w_ref[:, FOX_WIDTH:2 * FOX_WIDTH])
    v_ref[...] = _dot(h, w_ref[:, 2 * FOX_WIDTH:3 * FOX_WIDTH])
    zf = _dot(h, w_ref[:, 3 * FOX_WIDTH:3 * FOX_WIDTH + LANES]) + b_ref[...]
    lf = -(jnp.maximum(-zf, 0.0) + jnp.log(1.0 + jnp.exp(-jnp.abs(zf))))
    lfp_ref[...] = lf
    lf_ref[...] = lf[:, 0:N_FOX_HEADS]


def _proj_fox(x, g, w, b):
    n = x.shape[0]
    tm = TOKEN_TILE
    wcols = w.shape[1]
    tok = lambda i: (i, 0)
    fixed = lambda i: (0, 0)
    return pl.pallas_call(
        _proj_fox_kernel,
        grid=(n // tm,),
        in_specs=[pl.BlockSpec((tm, D_MODEL), tok), pl.BlockSpec((1, D_MODEL), fixed),
                  pl.BlockSpec((D_MODEL, wcols), fixed), pl.BlockSpec((1, LANES), fixed)],
        out_specs=[pl.BlockSpec((tm, FOX_WIDTH), tok), pl.BlockSpec((tm, FOX_WIDTH), tok),
                   pl.BlockSpec((tm, FOX_WIDTH), tok), pl.BlockSpec((tm, N_FOX_HEADS), tok),
                   pl.BlockSpec((tm, LANES), tok)],
        out_shape=[jax.ShapeDtypeStruct((n, FOX_WIDTH), BF16), jax.ShapeDtypeStruct((n, FOX_WIDTH), F32),
                   jax.ShapeDtypeStruct((n, FOX_WIDTH), F32), jax.ShapeDtypeStruct((n, N_FOX_HEADS), F32),
                   jax.ShapeDtypeStruct((n, LANES), F32)],
        compiler_params=_cparams(1), name="proj_fox",
    )(x, g, w, b)


def _proj_ret_qk_kernel(x_ref, g_ref, w_ref, cos_ref, sin_ref, q_ref, k_ref):
    h = _rmsnorm(x_ref[...], g_ref[...]).astype(BF16)
    cos = cos_ref[...]
    sin = sin_ref[...]
    half = RET_KEY_DIM // 2
    for out_ref, base, scale in ((q_ref, 0, 1.0), (k_ref, RET_QK_WIDTH, RET_KEY_DIM ** -0.5)):
        for hd in range(N_RET_HEADS):
            lo = hd * RET_KEY_DIM
            z = _dot(h, w_ref[:, base + lo:base + lo + RET_KEY_DIM])
            x1 = z[:, :half]
            x2 = z[:, half:]
            out_ref[:, lo:lo + half] = ((x1 * cos - x2 * sin) * scale).astype(BF16)
            out_ref[:, lo + half:lo + RET_KEY_DIM] = ((x1 * sin + x2 * cos) * scale).astype(BF16)


def _proj_ret_qk(x, g, w, cos, sin):
    n = x.shape[0]
    tm = TOKEN_TILE
    period = cos.shape[0] // tm
    tok = lambda i: (i, 0)
    fixed = lambda i: (0, 0)
    pos = lambda i: (i % period, 0)
    half = RET_KEY_DIM // 2
    return pl.pallas_call(
        _proj_ret_qk_kernel,
        grid=(n // tm,),
        in_specs=[pl.BlockSpec((tm, D_MODEL), tok), pl.BlockSpec((1, D_MODEL), fixed),
                  pl.BlockSpec((D_MODEL, 2 * RET_QK_WIDTH), fixed),
                  pl.BlockSpec((tm, half), pos), pl.BlockSpec((tm, half), pos)],
        out_specs=[pl.BlockSpec((tm, RET_QK_WIDTH), tok), pl.BlockSpec((tm, RET_QK_WIDTH), tok)],
        out_shape=[jax.ShapeDtypeStruct((n, RET_QK_WIDTH), BF16)] * 2,
        compiler_params=_cparams(1), name="proj_ret_qk",
    )(x, g, w, cos, sin)


def _proj_ret_vg_kernel(x_ref, g_ref, w_ref, v_ref, gr_ref):
    h = _rmsnorm(x_ref[...], g_ref[...]).astype(BF16)
    v_ref[...] = _dot(h, w_ref[:, 0:RET_V_WIDTH]).astype(BF16)
    gr_ref[...] = _dot(h, w_ref[:, RET_V_WIDTH:2 * RET_V_WIDTH])


def _proj_ret_vg(x, g, w):
    n = x.shape[0]
    tm = TOKEN_TILE
    tok = lambda i: (i, 0)
    fixed = lambda i: (0, 0)
    return pl.pallas_call(
        _proj_ret_vg_kernel,
        grid=(n // tm,),
        in_specs=[pl.BlockSpec((tm, D_MODEL), tok), pl.BlockSpec((1, D_MODEL), fixed),
                  pl.BlockSpec((D_MODEL, 2 * RET_V_WIDTH), fixed)],
        out_specs=[pl.BlockSpec((tm, RET_V_WIDTH), tok), pl.BlockSpec((tm, RET_V_WIDTH), tok)],
        out_shape=[jax.ShapeDtypeStruct((n, RET_V_WIDTH), BF16), jax.ShapeDtypeStruct((n, RET_V_WIDTH), F32)],
        compiler_params=_cparams(1), name="proj_ret_vg",
    )(x, g, w)


def _cumsum_kernel(lf_ref, c_ref, carry_ref, *, tb):
    @pl.when(pl.program_id(1) == 0)
    def _():
        carry_ref[...] = jnp.zeros_like(carry_ref)

    row = lax.broadcasted_iota(jnp.int32, (tb, tb), 0)
    col = lax.broadcasted_iota(jnp.int32, (tb, tb), 1)
    tri = jnp.where(col <= row, 1.0, 0.0).astype(BF16)
    c = carry_ref[0:1, :]
    for piece in _split3(lf_ref[0]):
        c = c + _dot(tri, piece)
    carry_ref[...] = jnp.broadcast_to(c[tb - 1:tb, :], carry_ref.shape)

    prow = lax.broadcasted_iota(jnp.int32, (LANES, LANES), 0)
    pcol = lax.broadcasted_iota(jnp.int32, (LANES, LANES), 1)
    out = jnp.zeros((tb, LANES), F32)
    for p, piece in enumerate(_split3(-c)):
        place = jnp.where(pcol == prow + p * N_FOX_HEADS, jnp.where(prow < N_FOX_HEADS, 1.0, 0.0), 0.0)
        out = out + _dot(piece, place.astype(BF16))
    c_ref[0] = out.astype(BF16)


def _cumsum_pieces(lf, tb):
    b, t, _ = lf.shape
    blk = lambda i, j: (i, j, 0)
    return pl.pallas_call(
        functools.partial(_cumsum_kernel, tb=tb),
        grid=(b, t // tb),
        in_specs=[pl.BlockSpec((1, tb, LANES), blk)],
        out_specs=pl.BlockSpec((1, tb, LANES), blk),
        out_shape=jax.ShapeDtypeStruct((b, t, LANES), BF16),
        scratch_shapes=[pltpu.VMEM((8, LANES), F32)],
        compiler_params=_cparams(2), name="cumsum",
    )(lf)


def _piece_selector(shape, axis, head):
    idx = lax.broadcasted_iota(jnp.int32, shape, axis)
    hit = jnp.where(idx < N_C_PIECES * N_FOX_HEADS, jnp.where((idx & (N_FOX_HEADS - 1)) == head, 1.0, 0.0), 0.0)
    return hit


def _fox_prompt_kernel(q_ref, k_ref, v_ref, c_ref, o_ref, kaug_ref, vt_ref, acc_ref, *, t, blk):
    pair = pl.program_id(1)
    qi = pl.program_id(2)
    nblk = t // blk
    d = FOX_HEAD_DIM

    @pl.when(qi == 0)
    def _build():
        ones = jnp.ones((ONES_ROWS, blk), F32)

        def body(j, carry):
            off = pl.multiple_of(j * blk, blk)
            kaug_ref[pl.ds(off, blk), 0:LANES] = k_ref[0, pl.ds(off, blk), :].astype(BF16)
            kaug_ref[pl.ds(off, blk), LANES:2 * LANES] = c_ref[0, pl.ds(off, blk), :]
            vt = v_ref[0, pl.ds(off, blk), :].T
            for h in range(HEADS_PER_LANE_BLOCK):
                vt_ref[j, h] = jnp.concatenate([vt[h * d:(h + 1) * d], ones], axis=0).astype(BF16)
            return carry

        lax.fori_loop(0, nblk, body, 0)

    qt = q_ref[0].astype(F32).T
    row = lax.broadcasted_iota(jnp.int32, (LANES, blk), 0)
    qa = []
    for h in range(HEADS_PER_LANE_BLOCK):
        in_head = jnp.where(row >= h * d, jnp.where(row < (h + 1) * d, 1.0, 0.0), 0.0)
        sel = _piece_selector((LANES, blk), 0, pair * HEADS_PER_LANE_BLOCK + h)
        qa.append(jnp.concatenate([qt * in_head, sel], axis=0).astype(BF16))

    acc_ref[...] = jnp.zeros_like(acc_ref)
    krow = lax.broadcasted_iota(jnp.int32, (blk, blk), 0)
    qcol = lax.broadcasted_iota(jnp.int32, (blk, blk), 1)

    def kv_step(j, m_prev, masked):
        off = pl.multiple_of(j * blk, blk)
        kb = kaug_ref[pl.ds(off, blk), :]
        m_next = []
        for h in range(HEADS_PER_LANE_BLOCK):
            s = _dot(kb, qa[h])
            if masked:
                s = jnp.where(krow <= qcol, s, MASK_VALUE)
            m_new = jnp.maximum(m_prev[h], jnp.max(s, axis=0, keepdims=True))
            alpha = jnp.exp(m_prev[h] - m_new)
            p = jnp.exp(s - m_new).astype(BF16)
            acc_ref[h] = acc_ref[h] * alpha + _dot(vt_ref[j, h], p)
            m_next.append(m_new)
        return tuple(m_next)

    m0 = jnp.full((1, blk), MASK_VALUE, F32)
    m = lax.fori_loop(0, qi, functools.partial(kv_step, masked=False), (m0,) * HEADS_PER_LANE_BLOCK)
    kv_step(qi, m, True)

    outs = []
    for h in range(HEADS_PER_LANE_BLOCK):
        a = acc_ref[h]
        outs.append(a[0:d] / a[d:d + 1])
    o_ref[0] = jnp.concatenate(outs, axis=0).T.astype(BF16)


def _fox_prompt(q, k, v, c):
    b, t, _ = q.shape
    blk = FOX_BLOCK
    nblk = t // blk
    return pl.pallas_call(
        functools.partial(_fox_prompt_kernel, t=t, blk=blk),
        grid=(b, N_HEAD_PAIRS, nblk),
        in_specs=[pl.BlockSpec((1, blk, LANES), lambda i, p, j: (i, j, p)),
                  pl.BlockSpec((1, t, LANES), lambda i, p, j: (i, 0, p)),
                  pl.BlockSpec((1, t, LANES), lambda i, p, j: (i, 0, p)),
                  pl.BlockSpec((1, t, LANES), lambda i, p, j: (i, 0, 0))],
        out_specs=pl.BlockSpec((1, blk, LANES), lambda i, p, j: (i, j, p)),
        out_shape=jax.ShapeDtypeStruct((b, t, FOX_WIDTH), BF16),
        scratch_shapes=[pltpu.VMEM((t, 2 * LANES), BF16),
                        pltpu.VMEM((nblk, HEADS_PER_LANE_BLOCK, FOX_HEAD_DIM + ONES_ROWS, blk), BF16),
                        pltpu.VMEM((HEADS_PER_LANE_BLOCK, FOX_HEAD_DIM + ONES_ROWS, blk), F32)],
        compiler_params=_cparams(3), name="fox_prompt",
    )(q, k, v, c)


def _fox_sample_kernel(q_ref, kn_ref, vn_ref, kc_ref, vc_ref, c_ref, o_ref, *, past, new):
    pair = pl.program_id(1)
    d = FOX_HEAD_DIM
    nk = past + new
    kall = jnp.concatenate([kc_ref[0].astype(BF16), kn_ref[0].astype(BF16)], axis=0)
    vall = jnp.concatenate([vc_ref[0].astype(BF16), vn_ref[0].astype(BF16)], axis=0)
    kaug = jnp.concatenate([kall, c_ref[0]], axis=1)
    q = q_ref[0]
    lane = lax.broadcasted_iota(jnp.int32, (new, LANES), 1)
    qrow = lax.broadcasted_iota(jnp.int32, (new, nk), 0)
    kcol = lax.broadcasted_iota(jnp.int32, (new, nk), 1)
    out = jnp.zeros((new, LANES), F32)
    for h in range(HEADS_PER_LANE_BLOCK):
        in_head = jnp.where(lane >= h * d, jnp.where(lane < (h + 1) * d, 1.0, 0.0), 0.0)
        sel = _piece_selector((new, LANES), 1, pair * HEADS_PER_LANE_BLOCK + h)
        qa = jnp.concatenate([q * in_head.astype(BF16), sel.astype(BF16)], axis=1)
        s = _dot_nt(qa, kaug)
        s = jnp.where(kcol <= qrow + past, s, MASK_VALUE)
        m = jnp.max(s, axis=-1, keepdims=True)
        e = jnp.exp(s - m)
        l = jnp.sum(e, axis=-1, keepdims=True)
        o = _dot(e.astype(BF16), vall) / l
        out = out + o * in_head
    o_ref[0] = out.astype(BF16)


def _fox_sample(q, kn, vn, kc, vc, c):
    b, new, _ = q.shape
    past = kc.shape[1]
    nk = past + new
    per_pair = lambda i, p: (i, 0, p)
    return pl.pallas_call(
        functools.partial(_fox_sample_kernel, past=past, new=new),
        grid=(b, N_HEAD_PAIRS),
        in_specs=[pl.BlockSpec((1, new, LANES), per_pair), pl.BlockSpec((1, new, LANES), per_pair),
                  pl.BlockSpec((1, new, LANES), per_pair), pl.BlockSpec((1, past, LANES), per_pair),
                  pl.BlockSpec((1, past, LANES), per_pair),
                  pl.BlockSpec((1, nk, LANES), lambda i, p: (i, 0, 0))],
        out_specs=pl.BlockSpec((1, new, LANES), per_pair),
        out_shape=jax.ShapeDtypeStruct((b, new, FOX_WIDTH), BF16),
        compiler_params=_cparams(2), name="fox_sample",
    )(q, kn, vn, kc, vc, c)


def _retention_kernel(lg_ref, q_ref, k_ref, v_ref, g_ref, gn_ref, s0_ref, o_ref, st_ref, *, blk):
    @pl.when(pl.program_id(2) == 0)
    def _():
        st_ref[0, 0] = s0_ref[0, 0]

    lg = lg_ref[0, 0:1, 0:1]
    n = lax.broadcasted_iota(jnp.int32, (blk, blk), 0)
    m = lax.broadcasted_iota(jnp.int32, (blk, blk), 1)
    shift = CHUNK.bit_length() - 1
    dist = jnp.abs(n - m).astype(F32)
    decay = jnp.where((m >> shift) <= (n >> shift), jnp.exp(lg * dist), 0.0)
    q = q_ref[0]
    k = k_ref[0]
    v = v_ref[0]
    state = st_ref[0, 0]
    pos = lax.broadcasted_iota(jnp.int32, (blk, RET_KEY_DIM), 0).astype(F32)
    s = _dot_nt(q, k) * decay
    qd = (q.astype(F32) * jnp.exp(lg * (pos + 1.0))).astype(BF16)
    y = _dot(s.astype(BF16), v) + _dot(qd, state.astype(BF16))
    kd = k.astype(F32) * jnp.exp(lg * (blk - 1.0 - pos))
    st_ref[0, 0] = jnp.exp(lg * blk) * state + _dot(kd.T.astype(BF16), v)

    yn = y * lax.rsqrt(jnp.mean(y * y, axis=-1, keepdims=True) + EPS) * gn_ref[0]
    g = g_ref[0]
    o_ref[0] = (g * jax.nn.sigmoid(g) * yn).astype(BF16)


def _retention(lg, q, k, v, g, gn, s0, blk):
    b, t, _ = q.shape
    qk = lambda i, h, j: (i, j, h)
    per_head_state = lambda i, h, j: (i, h, 0, 0)
    return pl.pallas_call(
        functools.partial(_retention_kernel, blk=blk),
        grid=(b, N_RET_HEADS, t // blk),
        in_specs=[pl.BlockSpec((1, 8, LANES), lambda i, h, j: (h, 0, 0)),
                  pl.BlockSpec((1, blk, RET_KEY_DIM), qk), pl.BlockSpec((1, blk, RET_KEY_DIM), qk),
                  pl.BlockSpec((1, blk, RET_VAL_DIM), qk), pl.BlockSpec((1, blk, RET_VAL_DIM), qk),
                  pl.BlockSpec((1, 1, RET_VAL_DIM), lambda i, h, j: (h, 0, 0)),
                  pl.BlockSpec((1, 1, RET_KEY_DIM, RET_VAL_DIM), per_head_state)],
        out_specs=[pl.BlockSpec((1, blk, RET_VAL_DIM), qk),
                   pl.BlockSpec((1, 1, RET_KEY_DIM, RET_VAL_DIM), per_head_state)],
        out_shape=[jax.ShapeDtypeStruct((b, t, RET_V_WIDTH), BF16),
                   jax.ShapeDtypeStruct((b, N_RET_HEADS, RET_KEY_DIM, RET_VAL_DIM), F32)],
        compiler_params=_cparams(3), name="retention",
    )(lg, q, k, v, g, gn, s0)


def _mixer_kernel(x_ref, oa_ref, ob_ref, g_ref, wg_ref, wa_ref, wb_ref, wo_ref, x1_ref):
    x = x_ref[...]
    h = _rmsnorm(x, g_ref[...]).astype(BF16)
    gate_a = jax.nn.sigmoid(_dot(h, wg_ref[:, 0:D_MODEL]))
    gate_b = jax.nn.sigmoid(_dot(h, wg_ref[:, D_MODEL:2 * D_MODEL]))
    merged = gate_a * _dot(oa_ref[...], wa_ref[...]) + gate_b * _dot(ob_ref[...], wb_ref[...])
    x1_ref[...] = x + _dot(merged.astype(BF16), wo_ref[...])


def _mixer(x, oa, ob, g, wg, wa, wb, wo):
    n = x.shape[0]
    tm = MIX_TOKEN_TILE
    tok = lambda i: (i, 0)
    fixed = lambda i: (0, 0)
    return pl.pallas_call(
        _mixer_kernel,
        grid=(n // tm,),
        in_specs=[pl.BlockSpec((tm, D_MODEL), tok), pl.BlockSpec((tm, FOX_WIDTH), tok),
                  pl.BlockSpec((tm, RET_V_WIDTH), tok), pl.BlockSpec((1, D_MODEL), fixed),
                  pl.BlockSpec((D_MODEL, 2 * D_MODEL), fixed), pl.BlockSpec((FOX_WIDTH, D_MODEL), fixed),
                  pl.BlockSpec((RET_V_WIDTH, D_MODEL), fixed), pl.BlockSpec((D_MODEL, D_MODEL), fixed)],
        out_specs=pl.BlockSpec((tm, D_MODEL), tok),
        out_shape=jax.ShapeDtypeStruct((n, D_MODEL), F32),
        compiler_params=_cparams(1), name="mixer",
    )(x, oa, ob, g, wg, wa, wb, wo)


def _mlp_kernel(x_ref, g_ref, wu_ref, wd_ref, gf_ref, y_ref):
    x = x_ref[...]
    h = _rmsnorm(x, g_ref[...]).astype(BF16)
    acc = x
    for c in range(D_FF // D_MODEL):
        cols = slice(c * D_MODEL, (c + 1) * D_MODEL)
        u = jnp.square(jnp.maximum(_dot(h, wu_ref[:, cols]), 0.0)).astype(BF16)
        acc = acc + _dot(u, wd_ref[cols, :])
    y_ref[...] = _rmsnorm(acc, gf_ref[...])


def _mlp(x, g, wu, wd, gf):
    n = x.shape[0]
    tm = MIX_TOKEN_TILE
    tok = lambda i: (i, 0)
    fixed = lambda i: (0, 0)
    return pl.pallas_call(
        _mlp_kernel,
        grid=(n // tm,),
        in_specs=[pl.BlockSpec((tm, D_MODEL), tok), pl.BlockSpec((1, D_MODEL), fixed),
                  pl.BlockSpec((D_MODEL, D_FF), fixed), pl.BlockSpec((D_FF, D_MODEL), fixed),
                  pl.BlockSpec((1, D_MODEL), fixed)],
        out_specs=pl.BlockSpec((tm, D_MODEL), tok),
        out_shape=jax.ShapeDtypeStruct((n, D_MODEL), F32),
        compiler_params=_cparams(1), name="mlp",
    )(x, g, wu, wd, gf)


def _rope_tables(pos):
    inv_freq = ROPE_BASE ** (-jnp.arange(0, RET_KEY_DIM, 2, dtype=F32) / RET_KEY_DIM)
    ang = pos.astype(F32)[:, None] * inv_freq[None, :]
    return jnp.cos(ang), jnp.sin(ang)


def _group(x, pos, params, ret_blk, attend, state0):
    b, t, _ = x.shape
    n = b * t
    xf = x.reshape(n, D_MODEL)
    cos, sin = _rope_tables(pos)
    reps = max(1, TOKEN_TILE // t)
    cos = jnp.tile(cos, (reps, 1))
    sin = jnp.tile(sin, (reps, 1))

    q, k, v, lf, lf_pad = _proj_fox(xf, params["g_attn"], params["w_fox"], params["b_forget"])
    q_r, k_r = _proj_ret_qk(xf, params["g_attn"], params["w_ret_qk"], cos, sin)
    v_r, g_r = _proj_ret_vg(xf, params["g_attn"], params["w_ret_vg"])

    o_a = attend(q.reshape(b, t, FOX_WIDTH), k.reshape(b, t, FOX_WIDTH), v.reshape(b, t, FOX_WIDTH),
                 lf_pad.reshape(b, t, LANES))
    o_b, state = _retention(params["lg"], q_r.reshape(b, t, RET_QK_WIDTH), k_r.reshape(b, t, RET_QK_WIDTH),
                            v_r.reshape(b, t, RET_V_WIDTH), g_r.reshape(b, t, RET_V_WIDTH),
                            params["g_ret_norm"], state0, ret_blk)
    x1 = _mixer(xf, o_a.reshape(n, FOX_WIDTH), o_b.reshape(n, RET_V_WIDTH), params["g_attn"],
                params["w_gates"], params["w_branch_a"], params["w_branch_b"], params["w_out"])
    y = _mlp(x1, params["g_mlp"], params["w_up"], params["w_down"], params["g_final"])
    shape5 = (1, b, t, N_FOX_HEADS, FOX_HEAD_DIM)
    return (y.reshape(b, t, D_MODEL), k.reshape(shape5), v.reshape(shape5),
            lf.reshape(1, b, t, N_FOX_HEADS), state[None])


def kernel(x_prompt, x_sample, cache_fox_k, cache_fox_v, cache_fox_logf, state_ret, g_attn, w_in, b_forget,
           g_ret_norm, w_branch, w_out, g_mlp, w_up, w_down, g_final):
    assert w_in.shape[0] == 1, "single-layer trunk"
    wi = w_in[0]
    fox_cols = 3 * FOX_WIDTH + N_FOX_HEADS
    qk_end = fox_cols + 2 * RET_QK_WIDTH
    vg_end = qk_end + 2 * RET_V_WIDTH
    lane_pad = LANES - N_FOX_HEADS
    lg = jnp.log(1.0 - 2.0 ** (-5.0 - jnp.arange(N_RET_HEADS, dtype=F32)))
    params = {
        "g_attn": g_attn[0].reshape(1, D_MODEL),
        "w_fox": jnp.pad(wi[:, :fox_cols], ((0, 0), (0, lane_pad))).astype(BF16),
        "b_forget": jnp.pad(b_forget[0], (0, lane_pad)).reshape(1, LANES),
        "w_ret_qk": wi[:, fox_cols:qk_end].astype(BF16),
        "w_ret_vg": wi[:, qk_end:vg_end].astype(BF16),
        "w_gates": wi[:, vg_end:].astype(BF16),
        "lg": jnp.broadcast_to(lg[:, None, None], (N_RET_HEADS, 8, LANES)),
        "g_ret_norm": g_ret_norm[0].reshape(N_RET_HEADS, 1, RET_VAL_DIM),
        "w_branch_a": w_branch[0, :FOX_WIDTH].astype(BF16),
        "w_branch_b": w_branch[0, FOX_WIDTH:].astype(BF16),
        "w_out": w_out[0].astype(BF16),
        "g_mlp": g_mlp[0].reshape(1, D_MODEL),
        "w_up": w_up[0].astype(BF16),
        "w_down": w_down[0].astype(BF16),
        "g_final": g_final.reshape(1, D_MODEL),
    }
    bp, tp, _ = x_prompt.shape
    bs, ts, _ = x_sample.shape
    past = cache_fox_k.shape[2]

    def attend_prompt(q, k, v, lf_pad):
        return _fox_prompt(q, k, v, _cumsum_pieces(lf_pad, CUMSUM_TILE_PROMPT))

    def attend_sample(q, k, v, lf_pad):
        lf_all = jnp.concatenate([jnp.pad(cache_fox_logf[0], ((0, 0), (0, 0), (0, lane_pad))), lf_pad], axis=1)
        c = _cumsum_pieces(lf_all, CUMSUM_TILE_SAMPLE)
        return _fox_sample(q, k, v, cache_fox_k[0].reshape(bs, past, FOX_WIDTH),
                           cache_fox_v[0].reshape(bs, past, FOX_WIDTH), c)

    zero_state = jnp.zeros((bp, N_RET_HEADS, RET_KEY_DIM, RET_VAL_DIM), F32)
    yp, kp, vp, fp, sp = _group(x_prompt, jnp.arange(tp), params, RET_BLOCK, attend_prompt, zero_state)
    ys, ks, vs, fs, ss = _group(x_sample, past + jnp.arange(ts), params, CHUNK, attend_sample, state_ret[0])
    return (yp, ys, kp, vp, fp, sp, ks, vs, fs, ss)
```

```python
import functools

import jax
import jax.numpy as jnp
from jax import lax
from jax.experimental import pallas as pl
from jax.experimental.pallas import tpu as pltpu

D_MODEL = 1024
N_FOX_HEADS = 16
FOX_HEAD_DIM = 64
FOX_WIDTH = N_FOX_HEADS * FOX_HEAD_DIM
N_RET_HEADS = 4
RET_KEY_DIM = 256
RET_VAL_DIM = 512
RET_QK_WIDTH = N_RET_HEADS * RET_KEY_DIM
RET_V_WIDTH = N_RET_HEADS * RET_VAL_DIM
D_FF = 4 * D_MODEL
CHUNK = 64
ROPE_BASE = 10000.0
EPS = 1e-6

LANES = 128
HEADS_PER_LANE_BLOCK = LANES // FOX_HEAD_DIM
N_HEAD_PAIRS = N_FOX_HEADS // HEADS_PER_LANE_BLOCK
N_C_PIECES = 3
ONES_ROWS = 16
MASK_VALUE = -1e30
VMEM_LIMIT = 56 * 1024 * 1024

TOKEN_TILE = 512
MIX_TOKEN_TILE = 256
FOX_BLOCK = 256
FOX_Q_BLOCK = 512
RET_BLOCK = 256
CUMSUM_TILE_PROMPT = 512
CUMSUM_TILE_SAMPLE = 704

BF16 = jnp.bfloat16
F32 = jnp.float32


def _cparams(n_axes):
    return pltpu.CompilerParams(dimension_semantics=("arbitrary",) * n_axes,
                                vmem_limit_bytes=VMEM_LIMIT)


def _dot(a, b):
    return jnp.dot(a, b, preferred_element_type=F32)


def _dot_nt(a, b):
    return lax.dot_general(a, b, (((1,), (1,)), ((), ())), preferred_element_type=F32)


def _rmsnorm(x, g):
    return x * lax.rsqrt(jnp.mean(x * x, axis=-1, keepdims=True) + EPS) * g


def _split3(x):
    a = x.astype(BF16)
    r = x - a.astype(F32)
    b = r.astype(BF16)
    c = (r - b.astype(F32)).astype(BF16)
    return a, b, c


def _proj_fox_kernel(x_ref, g_ref, w_ref, b_ref, q_ref, k_ref, v_ref, lf_ref, lfp_ref):
    h = _rmsnorm(x_ref[...], g_ref[...]).astype(BF16)
    q_ref[...] = (_dot(h, w_ref[:, 0:FOX_WIDTH]) * (FOX_HEAD_DIM ** -0.5)).astype(BF16)
    k_ref[...] = _dot(h, w_ref[:, FOX_WIDTH:2 * FOX_WIDTH])
    v_ref[...] = _dot(h, w_ref[:, 2 * FOX_WIDTH:3 * FOX_WIDTH])
    zf = _dot(h, w_ref[:, 3 * FOX_WIDTH:3 * FOX_WIDTH + LANES]) + b_ref[...]
    lf = -(jnp.maximum(-zf, 0.0) + jnp.log(1.0 + jnp.exp(-jnp.abs(zf))))
    lfp_ref[...] = lf
    lf_ref[...] = lf[:, 0:N_FOX_HEADS]


def _proj_fox(x, g, w, b):
    n = x.shape[0]
    tm = TOKEN_TILE
    wcols = w.shape[1]
    tok = lambda i: (i, 0)
    fixed = lambda i: (0, 0)
    return pl.pallas_call(
        _proj_fox_kernel,
        grid=(n // tm,),
        in_specs=[pl.BlockSpec((tm, D_MODEL), tok), pl.BlockSpec((1, D_MODEL), fixed),
                  pl.BlockSpec((D_MODEL, wcols), fixed), pl.BlockSpec((1, LANES), fixed)],
        out_specs=[pl.BlockSpec((tm, FOX_WIDTH), tok), pl.BlockSpec((tm, FOX_WIDTH), tok),
                   pl.BlockSpec((tm, FOX_WIDTH), tok), pl.BlockSpec((tm, N_FOX_HEADS), tok),
                   pl.BlockSpec((tm, LANES), tok)],
        out_shape=[jax.ShapeDtypeStruct((n, FOX_WIDTH), BF16), jax.ShapeDtypeStruct((n, FOX_WIDTH), F32),
                   jax.ShapeDtypeStruct((n, FOX_WIDTH), F32), jax.ShapeDtypeStruct((n, N_FOX_HEADS), F32),
                   jax.ShapeDtypeStruct((n, LANES), F32)],
        compiler_params=_cparams(1), name="proj_fox",
    )(x, g, w, b)


def _proj_ret_qk_kernel(x_ref, g_ref, w_ref, cos_ref, sin_ref, q_ref, k_ref):
    h = _rmsnorm(x_ref[...], g_ref[...]).astype(BF16)
    cos = cos_ref[...]
    sin = sin_ref[...]
    half = RET_KEY_DIM // 2
    for out_ref, base, scale in ((q_ref, 0, 1.0), (k_ref, RET_QK_WIDTH, RET_KEY_DIM ** -0.5)):
        for hd in range(N_RET_HEADS):
            lo = hd * RET_KEY_DIM
            z = _dot(h, w_ref[:, base + lo:base + lo + RET_KEY_DIM])
            x1 = z[:, :half]
            x2 = z[:, half:]
            out_ref[:, lo:lo + half] = ((x1 * cos - x2 * sin) * scale).astype(BF16)
            out_ref[:, lo + half:lo + RET_KEY_DIM] = ((x1 * sin + x2 * cos) * scale).astype(BF16)


def _proj_ret_qk(x, g, w, cos, sin):
    n = x.shape[0]
    tm = TOKEN_TILE
    period = cos.shape[0] // tm
    tok = lambda i: (i, 0)
    fixed = lambda i: (0, 0)
    pos = lambda i: (i % period, 0)
    half = RET_KEY_DIM // 2
    return pl.pallas_call(
        _proj_ret_qk_kernel,
        grid=(n // tm,),
        in_specs=[pl.BlockSpec((tm, D_MODEL), tok), pl.BlockSpec((1, D_MODEL), fixed),
                  pl.BlockSpec((D_MODEL, 2 * RET_QK_WIDTH), fixed),
                  pl.BlockSpec((tm, half), pos), pl.BlockSpec((tm, half), pos)],
        out_specs=[pl.BlockSpec((tm, RET_QK_WIDTH), tok), pl.BlockSpec((tm, RET_QK_WIDTH), tok)],
        out_shape=[jax.ShapeDtypeStruct((n, RET_QK_WIDTH), BF16)] * 2,
        compiler_params=_cparams(1), name="proj_ret_qk",
    )(x, g, w, cos, sin)


def _proj_ret_vg_kernel(x_ref, g_ref, w_ref, v_ref, gr_ref):
    h = _rmsnorm(x_ref[...], g_ref[...]).astype(BF16)
    v_ref[...] = _dot(h, w_ref[:, 0:RET_V_WIDTH]).astype(BF16)
    gr_ref[...] = _dot(h, w_ref[:, RET_V_WIDTH:2 * RET_V_WIDTH])


def _proj_ret_vg(x, g, w):
    n = x.shape[0]
    tm = TOKEN_TILE
    tok = lambda i: (i, 0)
    fixed = lambda i: (0, 0)
    return pl.pallas_call(
        _proj_ret_vg_kernel,
        grid=(n // tm,),
        in_specs=[pl.BlockSpec((tm, D_MODEL), tok), pl.BlockSpec((1, D_MODEL), fixed),
                  pl.BlockSpec((D_MODEL, 2 * RET_V_WIDTH), fixed)],
        out_specs=[pl.BlockSpec((tm, RET_V_WIDTH), tok), pl.BlockSpec((tm, RET_V_WIDTH), tok)],
        out_shape=[jax.ShapeDtypeStruct((n, RET_V_WIDTH), BF16), jax.ShapeDtypeStruct((n, RET_V_WIDTH), F32)],
        compiler_params=_cparams(1), name="proj_ret_vg",
    )(x, g, w)


def _cumsum_kernel(lf_ref, c_ref, carry_ref, *, tb):
    @pl.when(pl.program_id(1) == 0)
    def _():
        carry_ref[...] = jnp.zeros_like(carry_ref)

    row = lax.broadcasted_iota(jnp.int32, (tb, tb), 0)
    col = lax.broadcasted_iota(jnp.int32, (tb, tb), 1)
    tri = jnp.where(col <= row, 1.0, 0.0).astype(BF16)
    c = carry_ref[0:1, :]
    for piece in _split3(lf_ref[0]):
        c = c + _dot(tri, piece)
    carry_ref[...] = jnp.broadcast_to(c[tb - 1:tb, :], carry_ref.shape)

    prow = lax.broadcasted_iota(jnp.int32, (LANES, LANES), 0)
    pcol = lax.broadcasted_iota(jnp.int32, (LANES, LANES), 1)
    out = jnp.zeros((tb, LANES), F32)
    for p, piece in enumerate(_split3(-c)):
        place = jnp.where(pcol == prow + p * N_FOX_HEADS, jnp.where(prow < N_FOX_HEADS, 1.0, 0.0), 0.0)
        out = out + _dot(piece, place.astype(BF16))
    c_ref[0] = out.astype(BF16)


def _cumsum_pieces(lf, tb):
    b, t, _ = lf.shape
    blk = lambda i, j: (i, j, 0)
    return pl.pallas_call(
        functools.partial(_cumsum_kernel, tb=tb),
        grid=(b, t // tb),
        in_specs=[pl.BlockSpec((1, tb, LANES), blk)],
        out_specs=pl.BlockSpec((1, tb, LANES), blk),
        out_shape=jax.ShapeDtypeStruct((b, t, LANES), BF16),
        scratch_shapes=[pltpu.VMEM((8, LANES), F32)],
        compiler_params=_cparams(2), name="cumsum",
    )(lf)


def _piece_selector(shape, axis, head):
    idx = lax.broadcasted_iota(jnp.int32, shape, axis)
    hit = jnp.where(idx < N_C_PIECES * N_FOX_HEADS, jnp.where((idx & (N_FOX_HEADS - 1)) == head, 1.0, 0.0), 0.0)
    return hit


_ROW_M, _ROW_MX, _ROW_ALPHA = 0, HEADS_PER_LANE_BLOCK, 2 * HEADS_PER_LANE_BLOCK


def _fox_prompt_kernel(q_ref, k_ref, v_ref, c_ref, o_ref, kaug_ref, vt_ref, qa_ref, s_ref, p_ref, acc_ref,
                       stat_ref, *, t, tq, blk):
    pair = pl.program_id(1)
    qi = pl.program_id(2)
    nblk = t // blk
    diag_blocks = tq // blk
    d = FOX_HEAD_DIM
    heads = range(HEADS_PER_LANE_BLOCK)

    @pl.when(qi == 0)
    def _build():
        ones = jnp.ones((ONES_ROWS, blk), F32)

        def body(j, carry):
            off = pl.multiple_of(j * blk, blk)
            kaug_ref[pl.ds(off, blk), 0:LANES] = k_ref[0, pl.ds(off, blk), :].astype(BF16)
            kaug_ref[pl.ds(off, blk), LANES:2 * LANES] = c_ref[0, pl.ds(off, blk), :]
            vt = v_ref[0, pl.ds(off, blk), :].T
            for h in range(HEADS_PER_LANE_BLOCK):
                vt_ref[j, h] = jnp.concatenate([vt[h * d:(h + 1) * d], ones], axis=0).astype(BF16)
            return carry

        lax.fori_loop(0, nblk, body, 0)

    qt = q_ref[0].astype(F32).T
    row = lax.broadcasted_iota(jnp.int32, (LANES, tq), 0)
    for h in heads:
        in_head = jnp.where(row >= h * d, jnp.where(row < (h + 1) * d, 1.0, 0.0), 0.0)
        sel = _piece_selector((LANES, tq), 0, pair * HEADS_PER_LANE_BLOCK + h)
        qa_ref[h] = jnp.concatenate([qt * in_head, sel], axis=0).astype(BF16)

    acc_ref[...] = jnp.zeros_like(acc_ref)
    stat_ref[...] = jnp.full(stat_ref.shape, MASK_VALUE, F32)
    krow = lax.broadcasted_iota(jnp.int32, (blk, tq), 0)
    qcol = lax.broadcasted_iota(jnp.int32, (blk, tq), 1)

    def stage_a(j, diag):
        off = pl.multiple_of(j * blk, blk)
        kb = kaug_ref[pl.ds(off, blk), :]
        for h in heads:
            s = _dot(kb, qa_ref[h])
            if diag is not None:
                s = jnp.where(krow + diag * blk <= qcol, s, MASK_VALUE)
            s_ref[h] = s
            stat_ref[_ROW_MX + h:_ROW_MX + h + 1, :] = jnp.max(s, axis=0, keepdims=True)

    def stage_b():
        for h in heads:
            m_old = stat_ref[_ROW_M + h:_ROW_M + h + 1, :]
            m_new = jnp.maximum(m_old, stat_ref[_ROW_MX + h:_ROW_MX + h + 1, :])
            stat_ref[_ROW_ALPHA + h:_ROW_ALPHA + h + 1, :] = jnp.exp(m_old - m_new)
            stat_ref[_ROW_M + h:_ROW_M + h + 1, :] = m_new
            p_ref[h] = jnp.exp(s_ref[h] - m_new).astype(BF16)

    def stage_c(j):
        for h in heads:
            alpha = stat_ref[_ROW_ALPHA + h:_ROW_ALPHA + h + 1, :]
            acc_ref[h] = acc_ref[h] * alpha + _dot(vt_ref[j, h], p_ref[h])

    nfull = qi * diag_blocks

    @pl.when(qi > 0)
    def _pipelined():
        stage_a(0, None)
        stage_b()
        stage_a(1, None)

        def body(j, carry):
            stage_c(j - 2)
            stage_b()
            stage_a(j, None)
            return carry

        lax.fori_loop(2, nfull, body, 0)
        for dg in range(diag_blocks):
            stage_c(nfull + dg - 2)
            stage_b()
            stage_a(nfull + dg, dg)
        stage_c(nfull + diag_blocks - 2)
        stage_b()
        stage_c(nfull + diag_blocks - 1)

    @pl.when(qi == 0)
    def _first_block():
        for dg in range(diag_blocks):
            stage_a(dg, dg)
            stage_b()
            stage_c(dg)

    outs = []
    for h in heads:
        a = acc_ref[h]
        outs.append(a[0:d] / a[d:d + 1])
    o_ref[0] = jnp.concatenate(outs, axis=0).T.astype(BF16)


def _fox_prompt(q, k, v, c):
    b, t, _ = q.shape
    blk = FOX_BLOCK
    tq = FOX_Q_BLOCK
    assert tq == 2 * blk, "the pipeline prologue assumes two below-diagonal blocks exist once qi > 0"
    nblk = t // blk
    hp = HEADS_PER_LANE_BLOCK
    acc_rows = FOX_HEAD_DIM + ONES_ROWS
    return pl.pallas_call(
        functools.partial(_fox_prompt_kernel, t=t, tq=tq, blk=blk),
        grid=(b, N_HEAD_PAIRS, t // tq),
        in_specs=[pl.BlockSpec((1, tq, LANES), lambda i, p, j: (i, j, p)),
                  pl.BlockSpec((1, t, LANES), lambda i, p, j: (i, 0, p)),
                  pl.BlockSpec((1, t, LANES), lambda i, p, j: (i, 0, p)),
                  pl.BlockSpec((1, t, LANES), lambda i, p, j: (i, 0, 0))],
        out_specs=pl.BlockSpec((1, tq, LANES), lambda i, p, j: (i, j, p)),
        out_shape=jax.ShapeDtypeStruct((b, t, FOX_WIDTH), BF16),
        scratch_shapes=[pltpu.VMEM((t, 2 * LANES), BF16),
                        pltpu.VMEM((nblk, hp, acc_rows, blk), BF16),
                        pltpu.VMEM((hp, 2 * LANES, tq), BF16),
                        pltpu.VMEM((hp, blk, tq), F32),
                        pltpu.VMEM((hp, blk, tq), BF16),
                        pltpu.VMEM((hp, acc_rows, tq), F32),
                        pltpu.VMEM((8, tq), F32)],
        compiler_params=_cparams(3), name="fox_prompt",
    )(q, k, v, c)


def _fox_sample_kernel(q_ref, kn_ref, vn_ref, kc_ref, vc_ref, c_ref, o_ref, *, past, new):
    pair = pl.program_id(1)
    d = FOX_HEAD_DIM
    nk = past + new
    kall = jnp.concatenate([kc_ref[0].astype(BF16), kn_ref[0].astype(BF16)], axis=0)
    vall = jnp.concatenate([vc_ref[0].astype(BF16), vn_ref[0].astype(BF16)], axis=0)
    kaug = jnp.concatenate([kall, c_ref[0]], axis=1)
    q = q_ref[0]
    lane = lax.broadcasted_iota(jnp.int32, (new, LANES), 1)
    qrow = lax.broadcasted_iota(jnp.int32, (new, nk), 0)
    kcol = lax.broadcasted_iota(jnp.int32, (new, nk), 1)
    out = jnp.zeros((new, LANES), F32)
    for h in range(HEADS_PER_LANE_BLOCK):
        in_head = jnp.where(lane >= h * d, jnp.where(lane < (h + 1) * d, 1.0, 0.0), 0.0)
        sel = _piece_selector((new, LANES), 1, pair * HEADS_PER_LANE_BLOCK + h)
        qa = jnp.concatenate([q * in_head.astype(BF16), sel.astype(BF16)], axis=1)
        s = _dot_nt(qa, kaug)
        s = jnp.where(kcol <= qrow + past, s, MASK_VALUE)
        m = jnp.max(s, axis=-1, keepdims=True)
        e = jnp.exp(s - m)
        l = jnp.sum(e, axis=-1, keepdims=True)
        o = _dot(e.astype(BF16), vall) / l
        out = out + o * in_head
    o_ref[0] = out.astype(BF16)


def _fox_sample(q, kn, vn, kc, vc, c):
    b, new, _ = q.shape
    past = kc.shape[1]
    nk = past + new
    per_pair = lambda i, p: (i, 0, p)
    return pl.pallas_call(
        functools.partial(_fox_sample_kernel, past=past, new=new),
        grid=(b, N_HEAD_PAIRS),
        in_specs=[pl.BlockSpec((1, new, LANES), per_pair), pl.BlockSpec((1, new, LANES), per_pair),
                  pl.BlockSpec((1, new, LANES), per_pair), pl.BlockSpec((1, past, LANES), per_pair),
                  pl.BlockSpec((1, past, LANES), per_pair),
                  pl.BlockSpec((1, nk, LANES), lambda i, p: (i, 0, 0))],
        out_specs=pl.BlockSpec((1, new, LANES), per_pair),
        out_shape=jax.ShapeDtypeStruct((b, new, FOX_WIDTH), BF16),
        compiler_params=_cparams(2), name="fox_sample",
    )(q, kn, vn, kc, vc, c)


def _retention_kernel(lg_ref, q_ref, k_ref, v_ref, g_ref, gn_ref, s0_ref, o_ref, st_ref, *, blk):
    @pl.when(pl.program_id(2) == 0)
    def _():
        st_ref[0, 0] = s0_ref[0, 0]

    lg = lg_ref[0, 0:1, 0:1]
    n = lax.broadcasted_iota(jnp.int32, (blk, blk), 0)
    m = lax.broadcasted_iota(jnp.int32, (blk, blk), 1)
    shift = CHUNK.bit_length() - 1
    dist = jnp.abs(n - m).astype(F32)
    decay = jnp.where((m >> shift) <= (n >> shift), jnp.exp(lg * dist), 0.0)
    q = q_ref[0]
    k = k_ref[0]
    v = v_ref[0]
    state = st_ref[0, 0]
    pos = lax.broadcasted_iota(jnp.int32, (blk, RET_KEY_DIM), 0).astype(F32)
    s = _dot_nt(q, k) * decay
    qd = (q.astype(F32) * jnp.exp(lg * (pos + 1.0))).astype(BF16)
    y = _dot(s.astype(BF16), v) + _dot(qd, state.astype(BF16))
    kd = k.astype(F32) * jnp.exp(lg * (blk - 1.0 - pos))
    st_ref[0, 0] = jnp.exp(lg * blk) * state + _dot(kd.T.astype(BF16), v)

    yn = y * lax.rsqrt(jnp.mean(y * y, axis=-1, keepdims=True) + EPS) * gn_ref[0]
    g = g_ref[0]
    o_ref[0] = (g * jax.nn.sigmoid(g) * yn).astype(BF16)


def _retention(lg, q, k, v, g, gn, s0, blk):
    b, t, _ = q.shape
    qk = lambda i, h, j: (i, j, h)
    per_head_state = lambda i, h, j: (i, h, 0, 0)
    return pl.pallas_call(
        functools.partial(_retention_kernel, blk=blk),
        grid=(b, N_RET_HEADS, t // blk),
        in_specs=[pl.BlockSpec((1, 8, LANES), lambda i, h, j: (h, 0, 0)),
                  pl.BlockSpec((1, blk, RET_KEY_DIM), qk), pl.BlockSpec((1, blk, RET_KEY_DIM), qk),
                  pl.BlockSpec((1, blk, RET_VAL_DIM), qk), pl.BlockSpec((1, blk, RET_VAL_DIM), qk),
                  pl.BlockSpec((1, 1, RET_VAL_DIM), lambda i, h, j: (h, 0, 0)),
                  pl.BlockSpec((1, 1, RET_KEY_DIM, RET_VAL_DIM), per_head_state)],
        out_specs=[pl.BlockSpec((1, blk, RET_VAL_DIM), qk),
                   pl.BlockSpec((1, 1, RET_KEY_DIM, RET_VAL_DIM), per_head_state)],
        out_shape=[jax.ShapeDtypeStruct((b, t, RET_V_WIDTH), BF16),
                   jax.ShapeDtypeStruct((b, N_RET_HEADS, RET_KEY_DIM, RET_VAL_DIM), F32)],
        compiler_params=_cparams(3), name="retention",
    )(lg, q, k, v, g, gn, s0)


def _mixer_kernel(x_ref, oa_ref, ob_ref, g_ref, wg_ref, wa_ref, wb_ref, wo_ref, x1_ref):
    x = x_ref[...]
    h = _rmsnorm(x, g_ref[...]).astype(BF16)
    gate_a = jax.nn.sigmoid(_dot(h, wg_ref[:, 0:D_MODEL]))
    gate_b = jax.nn.sigmoid(_dot(h, wg_ref[:, D_MODEL:2 * D_MODEL]))
    merged = gate_a * _dot(oa_ref[...], wa_ref[...]) + gate_b * _dot(ob_ref[...], wb_ref[...])
    x1_ref[...] = x + _dot(merged.astype(BF16), wo_ref[...])


def _mixer(x, oa, ob, g, wg, wa, wb, wo):
    n = x.shape[0]
    tm = MIX_TOKEN_TILE
    tok = lambda i: (i, 0)
    fixed = lambda i: (0, 0)
    return pl.pallas_call(
        _mixer_kernel,
        grid=(n // tm,),
        in_specs=[pl.BlockSpec((tm, D_MODEL), tok), pl.BlockSpec((tm, FOX_WIDTH), tok),
                  pl.BlockSpec((tm, RET_V_WIDTH), tok), pl.BlockSpec((1, D_MODEL), fixed),
                  pl.BlockSpec((D_MODEL, 2 * D_MODEL), fixed), pl.BlockSpec((FOX_WIDTH, D_MODEL), fixed),
                  pl.BlockSpec((RET_V_WIDTH, D_MODEL), fixed), pl.BlockSpec((D_MODEL, D_MODEL), fixed)],
        out_specs=pl.BlockSpec((tm, D_MODEL), tok),
        out_shape=jax.ShapeDtypeStruct((n, D_MODEL), F32),
        compiler_params=_cparams(1), name="mixer",
    )(x, oa, ob, g, wg, wa, wb, wo)


def _mlp_kernel(x_ref, g_ref, wu_ref, wd_ref, gf_ref, y_ref):
    x = x_ref[...]
    h = _rmsnorm(x, g_ref[...]).astype(BF16)
    acc = x
    for c in range(D_FF // D_MODEL):
        cols = slice(c * D_MODEL, (c + 1) * D_MODEL)
        u = jnp.square(jnp.maximum(_dot(h, wu_ref[:, cols]), 0.0)).astype(BF16)
        acc = acc + _dot(u, wd_ref[cols, :])
    y_ref[...] = _rmsnorm(acc, gf_ref[...])


def _mlp(x, g, wu, wd, gf):
    n = x.shape[0]
    tm = MIX_TOKEN_TILE
    tok = lambda i: (i, 0)
    fixed = lambda i: (0, 0)
    return pl.pallas_call(
        _mlp_kernel,
        grid=(n // tm,),
        in_specs=[pl.BlockSpec((tm, D_MODEL), tok), pl.BlockSpec((1, D_MODEL), fixed),
                  pl.BlockSpec((D_MODEL, D_FF), fixed), pl.BlockSpec((D_FF, D_MODEL), fixed),
                  pl.BlockSpec((1, D_MODEL), fixed)],
        out_specs=pl.BlockSpec((tm, D_MODEL), tok),
        out_shape=jax.ShapeDtypeStruct((n, D_MODEL), F32),
        compiler_params=_cparams(1), name="mlp",
    )(x, g, wu, wd, gf)


def _rope_tables(pos):
    inv_freq = ROPE_BASE ** (-jnp.arange(0, RET_KEY_DIM, 2, dtype=F32) / RET_KEY_DIM)
    ang = pos.astype(F32)[:, None] * inv_freq[None, :]
    return jnp.cos(ang), jnp.sin(ang)


def _group(x, pos, params, ret_blk, attend, state0):
    b, t, _ = x.shape
    n = b * t
    xf = x.reshape(n, D_MODEL)
    cos, sin = _rope_tables(pos)
    reps = max(1, TOKEN_TILE // t)
    cos = jnp.tile(cos, (reps, 1))
    sin = jnp.tile(sin, (reps, 1))

    q, k, v, lf, lf_pad = _proj_fox(xf, params["g_attn"], params["w_fox"], params["b_forget"])
    q_r, k_r = _proj_ret_qk(xf, params["g_attn"], params["w_ret_qk"], cos, sin)
    v_r, g_r = _proj_ret_vg(xf, params["g_attn"], params["w_ret_vg"])

    o_a = attend(q.reshape(b, t, FOX_WIDTH), k.reshape(b, t, FOX_WIDTH), v.reshape(b, t, FOX_WIDTH),
                 lf_pad.reshape(b, t, LANES))
    o_b, state = _retention(params["lg"], q_r.reshape(b, t, RET_QK_WIDTH), k_r.reshape(b, t, RET_QK_WIDTH),
                            v_r.reshape(b, t, RET_V_WIDTH), g_r.reshape(b, t, RET_V_WIDTH),
                            params["g_ret_norm"], state0, ret_blk)
    x1 = _mixer(xf, o_a.reshape(n, FOX_WIDTH), o_b.reshape(n, RET_V_WIDTH), params["g_attn"],
                params["w_gates"], params["w_branch_a"], params["w_branch_b"], params["w_out"])
    y = _mlp(x1, params["g_mlp"], params["w_up"], params["w_down"], params["g_final"])
    shape5 = (1, b, t, N_FOX_HEADS, FOX_HEAD_DIM)
    return (y.reshape(b, t, D_MODEL), k.reshape(shape5), v.reshape(shape5),
            lf.reshape(1, b, t, N_FOX_HEADS), state[None])


def kernel(x_prompt, x_sample, cache_fox_k, cache_fox_v, cache_fox_logf, state_ret, g_attn, w_in, b_forget,
           g_ret_norm, w_branch, w_out, g_mlp, w_up, w_down, g_final):
    assert w_in.shape[0] == 1, "single-layer trunk"
    wi = w_in[0]
    fox_cols = 3 * FOX_WIDTH + N_FOX_HEADS
    qk_end = fox_cols + 2 * RET_QK_WIDTH
    vg_end = qk_end + 2 * RET_V_WIDTH
    lane_pad = LANES - N_FOX_HEADS
    lg = jnp.log(1.0 - 2.0 ** (-5.0 - jnp.arange(N_RET_HEADS, dtype=F32)))
    params = {
        "g_attn": g_attn[0].reshape(1, D_MODEL),
        "w_fox": jnp.pad(wi[:, :fox_cols], ((0, 0), (0, lane_pad))).astype(BF16),
        "b_forget": jnp.pad(b_forget[0], (0, lane_pad)).reshape(1, LANES),
        "w_ret_qk": wi[:, fox_cols:qk_end].astype(BF16),
        "w_ret_vg": wi[:, qk_end:vg_end].astype(BF16),
        "w_gates": wi[:, vg_end:].astype(BF16),
        "lg": jnp.broadcast_to(lg[:, None, None], (N_RET_HEADS, 8, LANES)),
        "g_ret_norm": g_ret_norm[0].reshape(N_RET_HEADS, 1, RET_VAL_DIM),
        "w_branch_a": w_branch[0, :FOX_WIDTH].astype(BF16),
        "w_branch_b": w_branch[0, FOX_WIDTH:].astype(BF16),
        "w_out": w_out[0].astype(BF16),
        "g_mlp": g_mlp[0].reshape(1, D_MODEL),
        "w_up": w_up[0].astype(BF16),
        "w_down": w_down[0].astype(BF16),
        "g_final": g_final.reshape(1, D_MODEL),
    }
    bp, tp, _ = x_prompt.shape
    bs, ts, _ = x_sample.shape
    past = cache_fox_k.shape[2]

    def attend_prompt(q, k, v, lf_pad):
        return _fox_prompt(q, k, v, _cumsum_pieces(lf_pad, CUMSUM_TILE_PROMPT))

    def attend_sample(q, k, v, lf_pad):
        lf_all = jnp.concatenate([jnp.pad(cache_fox_logf[0], ((0, 0), (0, 0), (0, lane_pad))), lf_pad], axis=1)
        c = _cumsum_pieces(lf_all, CUMSUM_TILE_SAMPLE)
        return _fox_sample(q, k, v, cache_fox_k[0].reshape(bs, past, FOX_WIDTH),
                           cache_fox_v[0].reshape(bs, past, FOX_WIDTH), c)

    zero_state = jnp.zeros((bp, N_RET_HEADS, RET_KEY_DIM, RET_VAL_DIM), F32)
    yp, kp, vp, fp, sp = _group(x_prompt, jnp.arange(tp), params, RET_BLOCK, attend_prompt, zero_state)
    ys, ks, vs, fs, ss = _group(x_sample, past + jnp.arange(ts), params, CHUNK, attend_sample, state_ret[0])
    return (yp, ys, kp, vp, fp, sp, ks, vs, fs, ss)
```

```python
import functools

import jax
import jax.numpy as jnp
from jax import lax
from jax.experimental import pallas as pl
from jax.experimental.pallas import tpu as pltpu

D_MODEL = 1024
N_FOX_HEADS = 16
FOX_HEAD_DIM = 64
FOX_WIDTH = N_FOX_HEADS * FOX_HEAD_DIM
N_RET_HEADS = 4
RET_KEY_DIM = 256
RET_VAL_DIM = 512
RET_QK_WIDTH = N_RET_HEADS * RET_KEY_DIM
RET_V_WIDTH = N_RET_HEADS * RET_VAL_DIM
D_FF = 4 * D_MODEL
CHUNK = 64
ROPE_BASE = 10000.0
EPS = 1e-6

LANES = 128
HEADS_PER_LANE_BLOCK = LANES // FOX_HEAD_DIM
N_HEAD_PAIRS = N_FOX_HEADS // HEADS_PER_LANE_BLOCK
N_C_PIECES = 3
ONES_ROWS = 16
MASK_VALUE = -1e30
VMEM_LIMIT = 56 * 1024 * 1024

TOKEN_TILE = 512
MIX_TOKEN_TILE = 256
FOX_BLOCK = 256
FOX_Q_BLOCK = 512
SAMPLE_PAIRS_PER_STEP = 4
RET_BLOCK = 256
CUMSUM_TILE = 512

BF16 = jnp.bfloat16
F32 = jnp.float32


def _cparams(n_axes):
    return pltpu.CompilerParams(dimension_semantics=("arbitrary",) * n_axes,
                                vmem_limit_bytes=VMEM_LIMIT)


def _dot(a, b):
    return jnp.dot(a, b, preferred_element_type=F32)


def _dot_nt(a, b):
    return lax.dot_general(a, b, (((1,), (1,)), ((), ())), preferred_element_type=F32)


def _rmsnorm(x, g):
    return x * lax.rsqrt(jnp.mean(x * x, axis=-1, keepdims=True) + EPS) * g


def _log_sigmoid(z):
    return -(jnp.maximum(-z, 0.0) + jnp.log(1.0 + jnp.exp(-jnp.abs(z))))


def _split3(x):
    a = x.astype(BF16)
    r = x - a.astype(F32)
    b = r.astype(BF16)
    c = (r - b.astype(F32)).astype(BF16)
    return a, b, c


def _indicator(idx, lo, hi):
    return jnp.where(idx >= lo, jnp.where(idx < hi, 1.0, 0.0), 0.0)


def _proj_fox_t_kernel(x_ref, g_ref, wt_ref, wf_ref, bcol_ref, brow_ref, qt_ref, kt_ref, vt_ref, lft_ref, lfp_ref):
    h = _rmsnorm(x_ref[0], g_ref[...])
    ht = h.T.astype(BF16)
    w = FOX_WIDTH
    qt_ref[0] = (_dot(wt_ref[0:w, :], ht) * (FOX_HEAD_DIM ** -0.5)).astype(BF16)
    kt_ref[0] = _dot(wt_ref[w:2 * w, :], ht)
    vt_ref[0] = _dot(wt_ref[2 * w:3 * w, :], ht)
    lft_ref[0] = _log_sigmoid(_dot(wt_ref[3 * w:3 * w + N_FOX_HEADS, :], ht) + bcol_ref[:, 0:1])
    lfp_ref[0] = _log_sigmoid(_dot(h.astype(BF16), wf_ref[...]) + brow_ref[...])


def _proj_fox_t(x, g, wt, wf, bcol, brow):
    b, t, _ = x.shape
    tm = TOKEN_TILE
    fixed = lambda i, j: (0, 0)
    feat = lambda i, j: (i, 0, j)
    return pl.pallas_call(
        _proj_fox_t_kernel,
        grid=(b, t // tm),
        in_specs=[pl.BlockSpec((1, tm, D_MODEL), lambda i, j: (i, j, 0)), pl.BlockSpec((1, D_MODEL), fixed),
                  pl.BlockSpec(wt.shape, fixed), pl.BlockSpec(wf.shape, fixed),
                  pl.BlockSpec(bcol.shape, fixed), pl.BlockSpec(brow.shape, fixed)],
        out_specs=[pl.BlockSpec((1, FOX_WIDTH, tm), feat), pl.BlockSpec((1, FOX_WIDTH, tm), feat),
                   pl.BlockSpec((1, FOX_WIDTH, tm), feat), pl.BlockSpec((1, N_FOX_HEADS, tm), feat),
                   pl.BlockSpec((1, tm, LANES), lambda i, j: (i, j, 0))],
        out_shape=[jax.ShapeDtypeStruct((b, FOX_WIDTH, t), BF16), jax.ShapeDtypeStruct((b, FOX_WIDTH, t), F32),
                   jax.ShapeDtypeStruct((b, FOX_WIDTH, t), F32), jax.ShapeDtypeStruct((b, N_FOX_HEADS, t), F32),
                   jax.ShapeDtypeStruct((b, t, LANES), F32)],
        compiler_params=_cparams(2), name="proj_fox_t",
    )(x, g, wt, wf, bcol, brow)


def _proj_fox_kernel(x_ref, g_ref, w_ref, b_ref, q_ref, k_ref, v_ref, lf_ref):
    h = _rmsnorm(x_ref[...], g_ref[...]).astype(BF16)
    w = FOX_WIDTH
    q_ref[...] = (_dot(h, w_ref[:, 0:w]) * (FOX_HEAD_DIM ** -0.5)).astype(BF16)
    k_ref[...] = _dot(h, w_ref[:, w:2 * w])
    v_ref[...] = _dot(h, w_ref[:, 2 * w:3 * w])
    lf = _log_sigmoid(_dot(h, w_ref[:, 3 * w:3 * w + LANES]) + b_ref[...])
    lf_ref[...] = lf[:, 0:N_FOX_HEADS]


def _proj_fox(x, g, w, b):
    n = x.shape[0]
    tm = TOKEN_TILE
    tok = lambda i: (i, 0)
    fixed = lambda i: (0, 0)
    return pl.pallas_call(
        _proj_fox_kernel,
        grid=(n // tm,),
        in_specs=[pl.BlockSpec((tm, D_MODEL), tok), pl.BlockSpec((1, D_MODEL), fixed),
                  pl.BlockSpec(w.shape, fixed), pl.BlockSpec((1, LANES), fixed)],
        out_specs=[pl.BlockSpec((tm, FOX_WIDTH), tok), pl.BlockSpec((tm, FOX_WIDTH), tok),
                   pl.BlockSpec((tm, FOX_WIDTH), tok), pl.BlockSpec((tm, N_FOX_HEADS), tok)],
        out_shape=[jax.ShapeDtypeStruct((n, FOX_WIDTH), BF16), jax.ShapeDtypeStruct((n, FOX_WIDTH), F32),
                   jax.ShapeDtypeStruct((n, FOX_WIDTH), F32), jax.ShapeDtypeStruct((n, N_FOX_HEADS), F32)],
        compiler_params=_cparams(1), name="proj_fox",
    )(x, g, w, b)


def _proj_ret_qk_kernel(x_ref, g_ref, w_ref, cos_ref, sin_ref, q_ref, k_ref):
    h = _rmsnorm(x_ref[...], g_ref[...]).astype(BF16)
    cos = cos_ref[...]
    sin = sin_ref[...]
    half = RET_KEY_DIM // 2
    for out_ref, base, scale in ((q_ref, 0, 1.0), (k_ref, RET_QK_WIDTH, RET_KEY_DIM ** -0.5)):
        for hd in range(N_RET_HEADS):
            lo = hd * RET_KEY_DIM
            z = _dot(h, w_ref[:, base + lo:base + lo + RET_KEY_DIM])
            x1 = z[:, :half]
            x2 = z[:, half:]
            out_ref[:, lo:lo + half] = ((x1 * cos - x2 * sin) * scale).astype(BF16)
            out_ref[:, lo + half:lo + RET_KEY_DIM] = ((x1 * sin + x2 * cos) * scale).astype(BF16)


def _proj_ret_qk(x, g, w, cos, sin):
    n = x.shape[0]
    tm = TOKEN_TILE
    period = cos.shape[0] // tm
    tok = lambda i: (i, 0)
    fixed = lambda i: (0, 0)
    pos = lambda i: (i % period, 0)
    half = RET_KEY_DIM // 2
    return pl.pallas_call(
        _proj_ret_qk_kernel,
        grid=(n // tm,),
        in_specs=[pl.BlockSpec((tm, D_MODEL), tok), pl.BlockSpec((1, D_MODEL), fixed),
                  pl.BlockSpec((D_MODEL, 2 * RET_QK_WIDTH), fixed),
                  pl.BlockSpec((tm, half), pos), pl.BlockSpec((tm, half), pos)],
        out_specs=[pl.BlockSpec((tm, RET_QK_WIDTH), tok), pl.BlockSpec((tm, RET_QK_WIDTH), tok)],
        out_shape=[jax.ShapeDtypeStruct((n, RET_QK_WIDTH), BF16)] * 2,
        compiler_params=_cparams(1), name="proj_ret_qk",
    )(x, g, w, cos, sin)


def _proj_ret_vg_kernel(x_ref, g_ref, w_ref, v_ref, gr_ref):
    h = _rmsnorm(x_ref[...], g_ref[...]).astype(BF16)
    v_ref[...] = _dot(h, w_ref[:, 0:RET_V_WIDTH]).astype(BF16)
    gr_ref[...] = _dot(h, w_ref[:, RET_V_WIDTH:2 * RET_V_WIDTH])


def _proj_ret_vg(x, g, w):
    n = x.shape[0]
    tm = TOKEN_TILE
    tok = lambda i: (i, 0)
    fixed = lambda i: (0, 0)
    return pl.pallas_call(
        _proj_ret_vg_kernel,
        grid=(n // tm,),
        in_specs=[pl.BlockSpec((tm, D_MODEL), tok), pl.BlockSpec((1, D_MODEL), fixed),
                  pl.BlockSpec((D_MODEL, 2 * RET_V_WIDTH), fixed)],
        out_specs=[pl.BlockSpec((tm, RET_V_WIDTH), tok), pl.BlockSpec((tm, RET_V_WIDTH), tok)],
        out_shape=[jax.ShapeDtypeStruct((n, RET_V_WIDTH), BF16), jax.ShapeDtypeStruct((n, RET_V_WIDTH), F32)],
        compiler_params=_cparams(1), name="proj_ret_vg",
    )(x, g, w)


def _cumsum_kernel(lf_ref, c_ref, carry_ref, *, tb):
    @pl.when(pl.program_id(1) == 0)
    def _():
        carry_ref[...] = jnp.zeros_like(carry_ref)

    row = lax.broadcasted_iota(jnp.int32, (tb, tb), 0)
    col = lax.broadcasted_iota(jnp.int32, (tb, tb), 1)
    tri = jnp.where(col <= row, 1.0, 0.0).astype(BF16)
    c = carry_ref[0:1, :]
    for piece in _split3(lf_ref[0]):
        c = c + _dot(tri, piece)
    carry_ref[...] = jnp.broadcast_to(c[tb - 1:tb, :], carry_ref.shape)

    prow = lax.broadcasted_iota(jnp.int32, (LANES, LANES), 0)
    pcol = lax.broadcasted_iota(jnp.int32, (LANES, LANES), 1)
    out = jnp.zeros((tb, LANES), F32)
    for p, piece in enumerate(_split3(-c)):
        place = jnp.where(pcol == prow + p * N_FOX_HEADS, jnp.where(prow < N_FOX_HEADS, 1.0, 0.0), 0.0)
        out = out + _dot(piece, place.astype(BF16))
    c_ref[0] = out.astype(BF16)


def _cumsum_pieces(lf, tb):
    b, t, _ = lf.shape
    blk = lambda i, j: (i, j, 0)
    return pl.pallas_call(
        functools.partial(_cumsum_kernel, tb=tb),
        grid=(b, t // tb),
        in_specs=[pl.BlockSpec((1, tb, LANES), blk)],
        out_specs=pl.BlockSpec((1, tb, LANES), blk),
        out_shape=jax.ShapeDtypeStruct((b, t, LANES), BF16),
        scratch_shapes=[pltpu.VMEM((8, LANES), F32)],
        compiler_params=_cparams(2), name="cumsum",
    )(lf)


def _cumsum_lanes_kernel(lf_ref, c_ref):
    heads, t = lf_ref.shape[1], lf_ref.shape[2]
    row = lax.broadcasted_iota(jnp.int32, (LANES, LANES), 0)
    col = lax.broadcasted_iota(jnp.int32, (LANES, LANES), 1)
    tri = jnp.where(row <= col, 1.0, 0.0).astype(BF16)
    ones = jnp.ones((LANES, LANES), BF16)
    carry = jnp.zeros((heads, LANES), F32)
    for seg in range(t // LANES):
        lanes = slice(seg * LANES, (seg + 1) * LANES)
        stack = jnp.concatenate(_split3(lf_ref[0, :, lanes]), axis=0)
        within = _dot(stack, tri)
        total = _dot(stack, ones)
        fold = lambda a: a[0:heads] + a[heads:2 * heads] + a[2 * heads:3 * heads]
        c_ref[0, :, lanes] = fold(within) + carry
        carry = carry + fold(total)


def _cumsum_lanes(lf):
    b, heads, t = lf.shape
    blk = lambda i: (i, 0, 0)
    return pl.pallas_call(
        _cumsum_lanes_kernel,
        grid=(b,),
        in_specs=[pl.BlockSpec((1, heads, t), blk)],
        out_specs=pl.BlockSpec((1, heads, t), blk),
        out_shape=jax.ShapeDtypeStruct((b, heads, t), F32),
        compiler_params=_cparams(1), name="cumsum_lanes",
    )(lf)


def _piece_selector(shape, axis, head):
    idx = lax.broadcasted_iota(jnp.int32, shape, axis)
    hit = jnp.where(idx < N_C_PIECES * N_FOX_HEADS, jnp.where((idx & (N_FOX_HEADS - 1)) == head, 1.0, 0.0), 0.0)
    return hit


_ROW_M, _ROW_MX, _ROW_ALPHA = 0, 1, 2


def _fox_prompt_kernel(qt_ref, kt_ref, vt_ref, c_ref, o_ref, kaug_ref, vta_ref, qa_ref, s_ref, p_ref, acc_ref,
                       stat_ref, *, t, tq, blk):
    pair = pl.program_id(1)
    qi = pl.program_id(2)
    nblk = t // blk
    diag_blocks = tq // blk
    d = FOX_HEAD_DIM
    heads = range(HEADS_PER_LANE_BLOCK)

    @pl.when(qi == 0)
    def _build():
        ones = jnp.ones((ONES_ROWS, blk), F32)

        def body(j, carry):
            off = pl.multiple_of(j * blk, blk)
            kaug_ref[pl.ds(off, blk), 0:LANES] = kt_ref[0, :, pl.ds(off, blk)].T.astype(BF16)
            kaug_ref[pl.ds(off, blk), LANES:2 * LANES] = c_ref[0, pl.ds(off, blk), :]
            for h in heads:
                vh = vt_ref[0, h * d:(h + 1) * d, pl.ds(off, blk)]
                vta_ref[j, h] = jnp.concatenate([vh, ones], axis=0).astype(BF16)
            return carry

        lax.fori_loop(0, nblk, body, 0)

    qt = qt_ref[0]
    row = lax.broadcasted_iota(jnp.int32, (LANES, tq), 0)
    for h in heads:
        in_head = _indicator(row, h * d, (h + 1) * d).astype(BF16)
        sel = _piece_selector((LANES, tq), 0, pair * HEADS_PER_LANE_BLOCK + h).astype(BF16)
        qa_ref[h] = jnp.concatenate([qt * in_head, sel], axis=0)

    acc_ref[...] = jnp.zeros_like(acc_ref)
    stat_ref[...] = jnp.full(stat_ref.shape, MASK_VALUE, F32)
    krow = lax.broadcasted_iota(jnp.int32, (blk, tq), 0)
    qcol = lax.broadcasted_iota(jnp.int32, (blk, tq), 1)

    def stat(h, slot, kind):
        r = (h * 2 + slot) * 4 + kind
        return slice(r, r + 1)

    def stage_a(j, slot, diag):
        off = pl.multiple_of(j * blk, blk)
        kb = kaug_ref[pl.ds(off, blk), :]
        for h in heads:
            s = _dot(kb, qa_ref[h])
            if diag is not None:
                s = jnp.where(krow + diag * blk <= qcol, s, MASK_VALUE)
            s_ref[slot, h] = s
            stat_ref[stat(h, slot, _ROW_MX), :] = jnp.max(s, axis=0, keepdims=True)

    def stage_b(slot):
        for h in heads:
            m_old = stat_ref[stat(h, 0, _ROW_M), :]
            m_new = jnp.maximum(m_old, stat_ref[stat(h, slot, _ROW_MX), :])
            stat_ref[stat(h, slot, _ROW_ALPHA), :] = jnp.exp(m_old - m_new)
            stat_ref[stat(h, 0, _ROW_M), :] = m_new
            p_ref[slot, h] = jnp.exp(s_ref[slot, h] - m_new).astype(BF16)

    def stage_c(j, slot):
        for h in heads:
            alpha = stat_ref[stat(h, slot, _ROW_ALPHA), :]
            acc_ref[h] = acc_ref[h] * alpha + _dot(vta_ref[j, h], p_ref[slot, h])

    nfull = qi * diag_blocks

    @pl.when(qi > 0)
    def _pipelined():
        stage_a(0, 0, None)
        stage_b(0)
        stage_a(1, 1, None)

        def body(u, carry):
            j = u * 2
            stage_c(j - 2, 0)
            stage_b(1)
            stage_a(j, 0, None)
            stage_c(j - 1, 1)
            stage_b(0)
            stage_a(j + 1, 1, None)
            return carry

        lax.fori_loop(1, qi, body, 0)
        for dg in range(diag_blocks):
            stage_c(nfull + dg - 2, dg % 2)
            stage_b((dg + 1) % 2)
            stage_a(nfull + dg, dg % 2, dg)
        stage_c(nfull + diag_blocks - 2, 0)
        stage_b(1)
        stage_c(nfull + diag_blocks - 1, 1)

    @pl.when(qi == 0)
    def _first_block():
        for dg in range(diag_blocks):
            stage_a(dg, dg % 2, dg)
            stage_b(dg % 2)
            stage_c(dg, dg % 2)

    outs = []
    for h in heads:
        a = acc_ref[h]
        outs.append(a[0:d] / a[d:d + 1])
    o_ref[0] = jnp.concatenate(outs, axis=0).T.astype(BF16)


def _fox_prompt(qt, kt, vt, c):
    b, _, t = qt.shape
    blk = FOX_BLOCK
    tq = FOX_Q_BLOCK
    assert tq == 2 * blk, "the pipeline keeps two key blocks in flight and slot = block parity"
    nblk = t // blk
    hp = HEADS_PER_LANE_BLOCK
    acc_rows = FOX_HEAD_DIM + ONES_ROWS
    per_pair_all_time = lambda i, p, j: (i, p, 0)
    return pl.pallas_call(
        functools.partial(_fox_prompt_kernel, t=t, tq=tq, blk=blk),
        grid=(b, N_HEAD_PAIRS, t // tq),
        in_specs=[pl.BlockSpec((1, LANES, tq), lambda i, p, j: (i, p, j)),
                  pl.BlockSpec((1, LANES, t), per_pair_all_time),
                  pl.BlockSpec((1, LANES, t), per_pair_all_time),
                  pl.BlockSpec((1, t, LANES), lambda i, p, j: (i, 0, 0))],
        out_specs=pl.BlockSpec((1, tq, LANES), lambda i, p, j: (i, j, p)),
        out_shape=jax.ShapeDtypeStruct((b, t, FOX_WIDTH), BF16),
        scratch_shapes=[pltpu.VMEM((t, 2 * LANES), BF16),
                        pltpu.VMEM((nblk, hp, acc_rows, blk), BF16),
                        pltpu.VMEM((hp, 2 * LANES, tq), BF16),
                        pltpu.VMEM((2, hp, blk, tq), F32),
                        pltpu.VMEM((2, hp, blk, tq), BF16),
                        pltpu.VMEM((hp, acc_rows, tq), F32),
                        pltpu.VMEM((hp * 2 * 4, tq), F32)],
        compiler_params=_cparams(3), name="fox_prompt",
    )(qt, kt, vt, c)


def _fox_sample_kernel(q_ref, kn_ref, vn_ref, kc_ref, vc_ref, c_ref, o_ref, *, past, new, pairs):
    group = pl.program_id(1)
    d = FOX_HEAD_DIM
    rows = HEADS_PER_LANE_BLOCK * new
    lane = lax.broadcasted_iota(jnp.int32, (rows, LANES), 1)
    qrow = lax.broadcasted_iota(jnp.int32, (rows, LANES), 0)
    head_lanes = jnp.where(qrow < new, _indicator(lane, 0, d), _indicator(lane, d, 2 * d)).astype(BF16)
    out_lane = lax.broadcasted_iota(jnp.int32, (new, LANES), 1)
    nrow = lax.broadcasted_iota(jnp.int32, (rows, new), 0)
    ncol = lax.broadcasted_iota(jnp.int32, (rows, new), 1)
    causal = ncol <= (nrow & (new - 1))
    for pp in range(pairs):
        lanes = slice(pp * LANES, (pp + 1) * LANES)
        head0 = (group * pairs + pp) * HEADS_PER_LANE_BLOCK
        tpad = c_ref.shape[2]
        bias = jnp.concatenate(
            [jnp.broadcast_to(c_ref[0, pl.ds(head0 + h, 1), :], (new, tpad)) for h in range(HEADS_PER_LANE_BLOCK)],
            axis=0)
        q = q_ref[0, :, lanes]
        q2 = jnp.concatenate([q, q], axis=0) * head_lanes
        s_c = _dot(q2, kc_ref[0, lanes, :].astype(BF16)) - bias[:, 0:past]
        s_n = _dot_nt(q2, kn_ref[0, :, lanes].astype(BF16)) - bias[:, past:past + new]
        s_n = jnp.where(causal, s_n, MASK_VALUE)
        m = jnp.maximum(jnp.max(s_c, axis=-1, keepdims=True), jnp.max(s_n, axis=-1, keepdims=True))
        e_c = jnp.exp(s_c - m)
        e_n = jnp.exp(s_n - m)
        l = jnp.sum(e_c, axis=-1, keepdims=True) + jnp.sum(e_n, axis=-1, keepdims=True)
        o2 = _dot_nt(e_c.astype(BF16), vc_ref[0, lanes, :].astype(BF16))
        o2 = (o2 + _dot(e_n.astype(BF16), vn_ref[0, :, lanes].astype(BF16))) / l
        o = jnp.where(out_lane < d, o2[0:new], o2[new:rows])
        o_ref[0, :, lanes] = o.astype(BF16)


def _fox_sample(q, kn, vn, kct, vct, c):
    b, new, _ = q.shape
    past = kct.shape[2]
    pairs = SAMPLE_PAIRS_PER_STEP
    width = pairs * LANES
    tok = lambda i, g: (i, 0, g)
    feat = lambda i, g: (i, g, 0)
    return pl.pallas_call(
        functools.partial(_fox_sample_kernel, past=past, new=new, pairs=pairs),
        grid=(b, N_HEAD_PAIRS // pairs),
        in_specs=[pl.BlockSpec((1, new, width), tok), pl.BlockSpec((1, new, width), tok),
                  pl.BlockSpec((1, new, width), tok), pl.BlockSpec((1, width, past), feat),
                  pl.BlockSpec((1, width, past), feat),
                  pl.BlockSpec((1,) + c.shape[1:], lambda i, g: (i, 0, 0))],
        out_specs=pl.BlockSpec((1, new, width), tok),
        out_shape=jax.ShapeDtypeStruct((b, new, FOX_WIDTH), BF16),
        compiler_params=_cparams(2), name="fox_sample",
    )(q, kn, vn, kct, vct, c)


def _retention_kernel(lg_ref, q_ref, k_ref, v_ref, g_ref, gn_ref, s0_ref, o_ref, st_ref,
                      decay_ref, qdec_ref, kdec_ref, *, blk):
    lg = lg_ref[0, 0:1, 0:1]

    @pl.when(pl.program_id(2) == 0)
    def _():
        st_ref[0, 0] = s0_ref[0, 0]
        n = lax.broadcasted_iota(jnp.int32, (blk, blk), 0)
        m = lax.broadcasted_iota(jnp.int32, (blk, blk), 1)
        shift = CHUNK.bit_length() - 1
        dist = jnp.abs(n - m).astype(F32)
        decay_ref[...] = jnp.where((m >> shift) <= (n >> shift), jnp.exp(lg * dist), 0.0)
        pos = lax.broadcasted_iota(jnp.int32, (blk, RET_KEY_DIM), 0).astype(F32)
        qdec_ref[...] = jnp.exp(lg * (pos + 1.0))
        kdec_ref[...] = jnp.exp(lg * (blk - 1.0 - pos))

    q = q_ref[0]
    k = k_ref[0]
    v = v_ref[0]
    state = st_ref[0, 0]
    s = _dot_nt(q, k) * decay_ref[...]
    qd = (q.astype(F32) * qdec_ref[...]).astype(BF16)
    y = _dot(s.astype(BF16), v) + _dot(qd, state.astype(BF16))
    kd = k.astype(F32) * kdec_ref[...]
    st_ref[0, 0] = jnp.exp(lg * blk) * state + _dot(kd.T.astype(BF16), v)

    yn = y * lax.rsqrt(jnp.mean(y * y, axis=-1, keepdims=True) + EPS) * gn_ref[0]
    g = g_ref[0]
    o_ref[0] = (g * jax.nn.sigmoid(g) * yn).astype(BF16)


def _retention(lg, q, k, v, g, gn, s0, blk):
    b, t, _ = q.shape
    qk = lambda i, h, j: (i, j, h)
    per_head_state = lambda i, h, j: (i, h, 0, 0)
    return pl.pallas_call(
        functools.partial(_retention_kernel, blk=blk),
        grid=(b, N_RET_HEADS, t // blk),
        in_specs=[pl.BlockSpec((1, 8, LANES), lambda i, h, j: (h, 0, 0)),
                  pl.BlockSpec((1, blk, RET_KEY_DIM), qk), pl.BlockSpec((1, blk, RET_KEY_DIM), qk),
                  pl.BlockSpec((1, blk, RET_VAL_DIM), qk), pl.BlockSpec((1, blk, RET_VAL_DIM), qk),
                  pl.BlockSpec((1, 1, RET_VAL_DIM), lambda i, h, j: (h, 0, 0)),
                  pl.BlockSpec((1, 1, RET_KEY_DIM, RET_VAL_DIM), per_head_state)],
        out_specs=[pl.BlockSpec((1, blk, RET_VAL_DIM), qk),
                   pl.BlockSpec((1, 1, RET_KEY_DIM, RET_VAL_DIM), per_head_state)],
        out_shape=[jax.ShapeDtypeStruct((b, t, RET_V_WIDTH), BF16),
                   jax.ShapeDtypeStruct((b, N_RET_HEADS, RET_KEY_DIM, RET_VAL_DIM), F32)],
        scratch_shapes=[pltpu.VMEM((blk, blk), F32), pltpu.VMEM((blk, RET_KEY_DIM), F32),
                        pltpu.VMEM((blk, RET_KEY_DIM), F32)],
        compiler_params=_cparams(3), name="retention",
    )(lg, q, k, v, g, gn, s0)


def _mixer_kernel(x_ref, oa_ref, ob_ref, g_ref, wg_ref, wa_ref, wb_ref, wo_ref, x1_ref):
    x = x_ref[...]
    h = _rmsnorm(x, g_ref[...]).astype(BF16)
    gate_a = jax.nn.sigmoid(_dot(h, wg_ref[:, 0:D_MODEL]))
    gate_b = jax.nn.sigmoid(_dot(h, wg_ref[:, D_MODEL:2 * D_MODEL]))
    merged = gate_a * _dot(oa_ref[...], wa_ref[...]) + gate_b * _dot(ob_ref[...], wb_ref[...])
    x1_ref[...] = x + _dot(merged.astype(BF16), wo_ref[...])


def _mixer(x, oa, ob, g, wg, wa, wb, wo):
    n = x.shape[0]
    tm = MIX_TOKEN_TILE
    tok = lambda i: (i, 0)
    fixed = lambda i: (0, 0)
    return pl.pallas_call(
        _mixer_kernel,
        grid=(n // tm,),
        in_specs=[pl.BlockSpec((tm, D_MODEL), tok), pl.BlockSpec((tm, FOX_WIDTH), tok),
                  pl.BlockSpec((tm, RET_V_WIDTH), tok), pl.BlockSpec((1, D_MODEL), fixed),
                  pl.BlockSpec((D_MODEL, 2 * D_MODEL), fixed), pl.BlockSpec((FOX_WIDTH, D_MODEL), fixed),
                  pl.BlockSpec((RET_V_WIDTH, D_MODEL), fixed), pl.BlockSpec((D_MODEL, D_MODEL), fixed)],
        out_specs=pl.BlockSpec((tm, D_MODEL), tok),
        out_shape=jax.ShapeDtypeStruct((n, D_MODEL), F32),
        compiler_params=_cparams(1), name="mixer",
    )(x, oa, ob, g, wg, wa, wb, wo)


def _mlp_kernel(x_ref, g_ref, wu_ref, wd_ref, gf_ref, y_ref):
    x = x_ref[...]
    h = _rmsnorm(x, g_ref[...]).astype(BF16)
    acc = x
    for c in range(D_FF // D_MODEL):
        cols = slice(c * D_MODEL, (c + 1) * D_MODEL)
        u = jnp.square(jnp.maximum(_dot(h, wu_ref[:, cols]), 0.0)).astype(BF16)
        acc = acc + _dot(u, wd_ref[cols, :])
    y_ref[...] = _rmsnorm(acc, gf_ref[...])


def _mlp(x, g, wu, wd, gf):
    n = x.shape[0]
    tm = MIX_TOKEN_TILE
    tok = lambda i: (i, 0)
    fixed = lambda i: (0, 0)
    return pl.pallas_call(
        _mlp_kernel,
        grid=(n // tm,),
        in_specs=[pl.BlockSpec((tm, D_MODEL), tok), pl.BlockSpec((1, D_MODEL), fixed),
                  pl.BlockSpec((D_MODEL, D_FF), fixed), pl.BlockSpec((D_FF, D_MODEL), fixed),
                  pl.BlockSpec((1, D_MODEL), fixed)],
        out_specs=pl.BlockSpec((tm, D_MODEL), tok),
        out_shape=jax.ShapeDtypeStruct((n, D_MODEL), F32),
        compiler_params=_cparams(1), name="mlp",
    )(x, g, wu, wd, gf)


def _rope_tables(pos):
    inv_freq = ROPE_BASE ** (-jnp.arange(0, RET_KEY_DIM, 2, dtype=F32) / RET_KEY_DIM)
    ang = pos.astype(F32)[:, None] * inv_freq[None, :]
    return jnp.cos(ang), jnp.sin(ang)


def _after_attention(x, pos, o_a, params, ret_blk, state0):
    b, t, _ = x.shape
    n = b * t
    xf = x.reshape(n, D_MODEL)
    cos, sin = _rope_tables(pos)
    reps = max(1, TOKEN_TILE // t)
    q_r, k_r = _proj_ret_qk(xf, params["g_attn"], params["w_ret_qk"], jnp.tile(cos, (reps, 1)),
                            jnp.tile(sin, (reps, 1)))
    v_r, g_r = _proj_ret_vg(xf, params["g_attn"], params["w_ret_vg"])
    o_b, state = _retention(params["lg"], q_r.reshape(b, t, RET_QK_WIDTH), k_r.reshape(b, t, RET_QK_WIDTH),
                            v_r.reshape(b, t, RET_V_WIDTH), g_r.reshape(b, t, RET_V_WIDTH),
                            params["g_ret_norm"], state0, ret_blk)
    x1 = _mixer(xf, o_a.reshape(n, FOX_WIDTH), o_b.reshape(n, RET_V_WIDTH), params["g_attn"],
                params["w_gates"], params["w_branch_a"], params["w_branch_b"], params["w_out"])
    y = _mlp(x1, params["g_mlp"], params["w_up"], params["w_down"], params["g_final"])
    return y.reshape(b, t, D_MODEL), state[None]


def kernel(x_prompt, x_sample, cache_fox_k, cache_fox_v, cache_fox_logf, state_ret, g_attn, w_in, b_forget,
           g_ret_norm, w_branch, w_out, g_mlp, w_up, w_down, g_final):
    assert w_in.shape[0] == 1, "single-layer trunk"
    wi = w_in[0]
    fox_cols = 3 * FOX_WIDTH + N_FOX_HEADS
    qk_end = fox_cols + 2 * RET_QK_WIDTH
    vg_end = qk_end + 2 * RET_V_WIDTH
    lane_pad = LANES - N_FOX_HEADS
    lg = jnp.log(1.0 - 2.0 ** (-5.0 - jnp.arange(N_RET_HEADS, dtype=F32)))
    w_fox = jnp.pad(wi[:, :fox_cols], ((0, 0), (0, lane_pad))).astype(BF16)
    b_row = jnp.pad(b_forget[0], (0, lane_pad)).reshape(1, LANES)
    params = {
        "g_attn": g_attn[0].reshape(1, D_MODEL),
        "w_ret_qk": wi[:, fox_cols:qk_end].astype(BF16),
        "w_ret_vg": wi[:, qk_end:vg_end].astype(BF16),
        "w_gates": wi[:, vg_end:].astype(BF16),
        "lg": jnp.broadcast_to(lg[:, None, None], (N_RET_HEADS, 8, LANES)),
        "g_ret_norm": g_ret_norm[0].reshape(N_RET_HEADS, 1, RET_VAL_DIM),
        "w_branch_a": w_branch[0, :FOX_WIDTH].astype(BF16),
        "w_branch_b": w_branch[0, FOX_WIDTH:].astype(BF16),
        "w_out": w_out[0].astype(BF16),
        "g_mlp": g_mlp[0].reshape(1, D_MODEL),
        "w_up": w_up[0].astype(BF16),
        "w_down": w_down[0].astype(BF16),
        "g_final": g_final.reshape(1, D_MODEL),
    }
    bp, tp, _ = x_prompt.shape
    bs, ts, _ = x_sample.shape
    past = cache_fox_k.shape[2]

    qt, kt, vt, lft, lf_pad = _proj_fox_t(
        x_prompt, params["g_attn"], wi[:, :fox_cols].T.astype(BF16), w_fox[:, 3 * FOX_WIDTH:],
        jnp.broadcast_to(b_forget[0][:, None], (N_FOX_HEADS, LANES)), b_row)
    o_a = _fox_prompt(qt, kt, vt, _cumsum_pieces(lf_pad, CUMSUM_TILE))
    zero_state = jnp.zeros((bp, N_RET_HEADS, RET_KEY_DIM, RET_VAL_DIM), F32)
    yp, sp = _after_attention(x_prompt, jnp.arange(tp), o_a, params, RET_BLOCK, zero_state)
    to_heads = lambda a: jnp.transpose(a.reshape(bp, N_FOX_HEADS, FOX_HEAD_DIM, tp), (0, 3, 1, 2))[None]
    kp, vp = to_heads(kt), to_heads(vt)
    fp = jnp.transpose(lft, (0, 2, 1))[None]

    q, k, v, lf = _proj_fox(x_sample.reshape(bs * ts, D_MODEL), params["g_attn"], w_fox, b_row)
    feature_major = lambda a: jnp.transpose(a[0], (0, 2, 3, 1)).reshape(bs, FOX_WIDTH, past)
    lf_time = jnp.concatenate([jnp.transpose(cache_fox_logf[0], (0, 2, 1)),
                               jnp.transpose(lf.reshape(bs, ts, N_FOX_HEADS), (0, 2, 1))], axis=2)
    lf_time = jnp.pad(lf_time, ((0, 0), (0, 0), (0, -(past + ts) % LANES)))
    o_a = _fox_sample(q.reshape(bs, ts, FOX_WIDTH), k.reshape(bs, ts, FOX_WIDTH), v.reshape(bs, ts, FOX_WIDTH),
                      feature_major(cache_fox_k), feature_major(cache_fox_v), _cumsum_lanes(lf_time))
    ys, ss = _after_attention(x_sample, past + jnp.arange(ts), o_a, params, CHUNK, state_ret[0])
    shape5 = (1, bs, ts, N_FOX_HEADS, FOX_HEAD_DIM)
    return (yp, ys, kp, vp, fp, sp, k.reshape(shape5), v.reshape(shape5),
            lf.reshape(1, bs, ts, N_FOX_HEADS), ss)
```

```python
import functools

import jax
import jax.numpy as jnp
from jax import lax
from jax.experimental import pallas as pl
from jax.experimental.pallas import tpu as pltpu

D_MODEL = 1024
N_FOX_HEADS = 16
FOX_HEAD_DIM = 64
FOX_WIDTH = N_FOX_HEADS * FOX_HEAD_DIM
N_RET_HEADS = 4
RET_KEY_DIM = 256
RET_VAL_DIM = 512
RET_QK_WIDTH = N_RET_HEADS * RET_KEY_DIM
RET_V_WIDTH = N_RET_HEADS * RET_VAL_DIM
D_FF = 4 * D_MODEL
CHUNK = 64
ROPE_BASE = 10000.0
EPS = 1e-6

LANES = 128
HEADS_PER_LANE_BLOCK = LANES // FOX_HEAD_DIM
N_HEAD_PAIRS = N_FOX_HEADS // HEADS_PER_LANE_BLOCK
N_C_PIECES = 3
ONES_ROWS = 16
MASK_VALUE = -1e30
LOG2E = 1.4426950408889634
VMEM_LIMIT = 56 * 1024 * 1024

TOKEN_TILE = 512
MIX_TOKEN_TILE = 256
FOX_BLOCK = 256
FOX_Q_BLOCK = 512
SAMPLE_PAIRS_PER_STEP = 4
RET_BLOCK = 256
CUMSUM_TILE = 512

BF16 = jnp.bfloat16
F32 = jnp.float32


def _cparams(n_axes):
    return pltpu.CompilerParams(dimension_semantics=("arbitrary",) * n_axes,
                                vmem_limit_bytes=VMEM_LIMIT)


def _dot(a, b):
    return jnp.dot(a, b, preferred_element_type=F32)


def _dot_nt(a, b):
    return lax.dot_general(a, b, (((1,), (1,)), ((), ())), preferred_element_type=F32)


def _rmsnorm(x, g):
    return x * lax.rsqrt(jnp.mean(x * x, axis=-1, keepdims=True) + EPS) * g


def _log_sigmoid(z):
    return -(jnp.maximum(-z, 0.0) + jnp.log(1.0 + jnp.exp(-jnp.abs(z))))


def _split3(x):
    a = x.astype(BF16)
    r = x - a.astype(F32)
    b = r.astype(BF16)
    c = (r - b.astype(F32)).astype(BF16)
    return a, b, c


def _indicator(idx, lo, hi):
    return jnp.where(idx >= lo, jnp.where(idx < hi, 1.0, 0.0), 0.0)


def _proj_fox_t_kernel(x_ref, g_ref, wt_ref, wf_ref, bcol_ref, brow_ref, qt_ref, kt_ref, vt_ref, lft_ref, lfp_ref):
    h = _rmsnorm(x_ref[0], g_ref[...])
    ht = h.T.astype(BF16)
    w = FOX_WIDTH
    qt_ref[0] = (_dot(wt_ref[0:w, :], ht) * (FOX_HEAD_DIM ** -0.5 * LOG2E)).astype(BF16)
    kt_ref[0] = _dot(wt_ref[w:2 * w, :], ht)
    vt_ref[0] = _dot(wt_ref[2 * w:3 * w, :], ht)
    lft_ref[0] = _log_sigmoid(_dot(wt_ref[3 * w:3 * w + N_FOX_HEADS, :], ht) + bcol_ref[:, 0:1])
    lfp_ref[0] = _log_sigmoid(_dot(h.astype(BF16), wf_ref[...]) + brow_ref[...])


def _proj_fox_t(x, g, wt, wf, bcol, brow):
    b, t, _ = x.shape
    tm = TOKEN_TILE
    fixed = lambda i, j: (0, 0)
    feat = lambda i, j: (i, 0, j)
    return pl.pallas_call(
        _proj_fox_t_kernel,
        grid=(b, t // tm),
        in_specs=[pl.BlockSpec((1, tm, D_MODEL), lambda i, j: (i, j, 0)), pl.BlockSpec((1, D_MODEL), fixed),
                  pl.BlockSpec(wt.shape, fixed), pl.BlockSpec(wf.shape, fixed),
                  pl.BlockSpec(bcol.shape, fixed), pl.BlockSpec(brow.shape, fixed)],
        out_specs=[pl.BlockSpec((1, FOX_WIDTH, tm), feat), pl.BlockSpec((1, FOX_WIDTH, tm), feat),
                   pl.BlockSpec((1, FOX_WIDTH, tm), feat), pl.BlockSpec((1, N_FOX_HEADS, tm), feat),
                   pl.BlockSpec((1, tm, LANES), lambda i, j: (i, j, 0))],
        out_shape=[jax.ShapeDtypeStruct((b, FOX_WIDTH, t), BF16), jax.ShapeDtypeStruct((b, FOX_WIDTH, t), F32),
                   jax.ShapeDtypeStruct((b, FOX_WIDTH, t), F32), jax.ShapeDtypeStruct((b, N_FOX_HEADS, t), F32),
                   jax.ShapeDtypeStruct((b, t, LANES), F32)],
        compiler_params=_cparams(2), name="proj_fox_t",
    )(x, g, wt, wf, bcol, brow)


def _proj_fox_kernel(x_ref, g_ref, w_ref, b_ref, q_ref, k_ref, v_ref, lf_ref):
    h = _rmsnorm(x_ref[...], g_ref[...]).astype(BF16)
    w = FOX_WIDTH
    q_ref[...] = (_dot(h, w_ref[:, 0:w]) * (FOX_HEAD_DIM ** -0.5)).astype(BF16)
    k_ref[...] = _dot(h, w_ref[:, w:2 * w])
    v_ref[...] = _dot(h, w_ref[:, 2 * w:3 * w])
    lf = _log_sigmoid(_dot(h, w_ref[:, 3 * w:3 * w + LANES]) + b_ref[...])
    lf_ref[...] = lf[:, 0:N_FOX_HEADS]


def _proj_fox(x, g, w, b):
    n = x.shape[0]
    tm = TOKEN_TILE
    tok = lambda i: (i, 0)
    fixed = lambda i: (0, 0)
    return pl.pallas_call(
        _proj_fox_kernel,
        grid=(n // tm,),
        in_specs=[pl.BlockSpec((tm, D_MODEL), tok), pl.BlockSpec((1, D_MODEL), fixed),
                  pl.BlockSpec(w.shape, fixed), pl.BlockSpec((1, LANES), fixed)],
        out_specs=[pl.BlockSpec((tm, FOX_WIDTH), tok), pl.BlockSpec((tm, FOX_WIDTH), tok),
                   pl.BlockSpec((tm, FOX_WIDTH), tok), pl.BlockSpec((tm, N_FOX_HEADS), tok)],
        out_shape=[jax.ShapeDtypeStruct((n, FOX_WIDTH), BF16), jax.ShapeDtypeStruct((n, FOX_WIDTH), F32),
                   jax.ShapeDtypeStruct((n, FOX_WIDTH), F32), jax.ShapeDtypeStruct((n, N_FOX_HEADS), F32)],
        compiler_params=_cparams(1), name="proj_fox",
    )(x, g, w, b)


def _proj_ret_qk_kernel(x_ref, g_ref, w_ref, cos_ref, sin_ref, q_ref, k_ref):
    h = _rmsnorm(x_ref[...], g_ref[...]).astype(BF16)
    cos = cos_ref[...]
    sin = sin_ref[...]
    half = RET_KEY_DIM // 2
    for out_ref, base, scale in ((q_ref, 0, 1.0), (k_ref, RET_QK_WIDTH, RET_KEY_DIM ** -0.5)):
        for hd in range(N_RET_HEADS):
            lo = hd * RET_KEY_DIM
            z = _dot(h, w_ref[:, base + lo:base + lo + RET_KEY_DIM])
            x1 = z[:, :half]
            x2 = z[:, half:]
            out_ref[:, lo:lo + half] = ((x1 * cos - x2 * sin) * scale).astype(BF16)
            out_ref[:, lo + half:lo + RET_KEY_DIM] = ((x1 * sin + x2 * cos) * scale).astype(BF16)


def _proj_ret_qk(x, g, w, cos, sin):
    n = x.shape[0]
    tm = TOKEN_TILE
    period = cos.shape[0] // tm
    tok = lambda i: (i, 0)
    fixed = lambda i: (0, 0)
    pos = lambda i: (i % period, 0)
    half = RET_KEY_DIM // 2
    return pl.pallas_call(
        _proj_ret_qk_kernel,
        grid=(n // tm,),
        in_specs=[pl.BlockSpec((tm, D_MODEL), tok), pl.BlockSpec((1, D_MODEL), fixed),
                  pl.BlockSpec((D_MODEL, 2 * RET_QK_WIDTH), fixed),
                  pl.BlockSpec((tm, half), pos), pl.BlockSpec((tm, half), pos)],
        out_specs=[pl.BlockSpec((tm, RET_QK_WIDTH), tok), pl.BlockSpec((tm, RET_QK_WIDTH), tok)],
        out_shape=[jax.ShapeDtypeStruct((n, RET_QK_WIDTH), BF16)] * 2,
        compiler_params=_cparams(1), name="proj_ret_qk",
    )(x, g, w, cos, sin)


def _proj_ret_vg_kernel(x_ref, g_ref, w_ref, v_ref, gr_ref):
    h = _rmsnorm(x_ref[...], g_ref[...]).astype(BF16)
    v_ref[...] = _dot(h, w_ref[:, 0:RET_V_WIDTH]).astype(BF16)
    gr_ref[...] = _dot(h, w_ref[:, RET_V_WIDTH:2 * RET_V_WIDTH])


def _proj_ret_vg(x, g, w):
    n = x.shape[0]
    tm = TOKEN_TILE
    tok = lambda i: (i, 0)
    fixed = lambda i: (0, 0)
    return pl.pallas_call(
        _proj_ret_vg_kernel,
        grid=(n // tm,),
        in_specs=[pl.BlockSpec((tm, D_MODEL), tok), pl.BlockSpec((1, D_MODEL), fixed),
                  pl.BlockSpec((D_MODEL, 2 * RET_V_WIDTH), fixed)],
        out_specs=[pl.BlockSpec((tm, RET_V_WIDTH), tok), pl.BlockSpec((tm, RET_V_WIDTH), tok)],
        out_shape=[jax.ShapeDtypeStruct((n, RET_V_WIDTH), BF16), jax.ShapeDtypeStruct((n, RET_V_WIDTH), F32)],
        compiler_params=_cparams(1), name="proj_ret_vg",
    )(x, g, w)


def _cumsum_kernel(lf_ref, c_ref, carry_ref, *, tb, scale):
    @pl.when(pl.program_id(1) == 0)
    def _():
        carry_ref[...] = jnp.zeros_like(carry_ref)

    row = lax.broadcasted_iota(jnp.int32, (tb, tb), 0)
    col = lax.broadcasted_iota(jnp.int32, (tb, tb), 1)
    tri = jnp.where(col <= row, 1.0, 0.0).astype(BF16)
    c = carry_ref[0:1, :]
    for piece in _split3(lf_ref[0]):
        c = c + _dot(tri, piece)
    carry_ref[...] = jnp.broadcast_to(c[tb - 1:tb, :], carry_ref.shape)

    prow = lax.broadcasted_iota(jnp.int32, (LANES, LANES), 0)
    pcol = lax.broadcasted_iota(jnp.int32, (LANES, LANES), 1)
    out = jnp.zeros((tb, LANES), F32)
    for p, piece in enumerate(_split3(-c * scale)):
        place = jnp.where(pcol == prow + p * N_FOX_HEADS, jnp.where(prow < N_FOX_HEADS, 1.0, 0.0), 0.0)
        out = out + _dot(piece, place.astype(BF16))
    c_ref[0] = out.astype(BF16)


def _cumsum_pieces(lf, tb, scale):
    b, t, _ = lf.shape
    blk = lambda i, j: (i, j, 0)
    return pl.pallas_call(
        functools.partial(_cumsum_kernel, tb=tb, scale=scale),
        grid=(b, t // tb),
        in_specs=[pl.BlockSpec((1, tb, LANES), blk)],
        out_specs=pl.BlockSpec((1, tb, LANES), blk),
        out_shape=jax.ShapeDtypeStruct((b, t, LANES), BF16),
        scratch_shapes=[pltpu.VMEM((8, LANES), F32)],
        compiler_params=_cparams(2), name="cumsum",
    )(lf)


def _cumsum_lanes_kernel(lf_ref, c_ref):
    heads, t = lf_ref.shape[1], lf_ref.shape[2]
    row = lax.broadcasted_iota(jnp.int32, (LANES, LANES), 0)
    col = lax.broadcasted_iota(jnp.int32, (LANES, LANES), 1)
    tri = jnp.where(row <= col, 1.0, 0.0).astype(BF16)
    ones = jnp.ones((LANES, LANES), BF16)
    carry = jnp.zeros((heads, LANES), F32)
    for seg in range(t // LANES):
        lanes = slice(seg * LANES, (seg + 1) * LANES)
        stack = jnp.concatenate(_split3(lf_ref[0, :, lanes]), axis=0)
        within = _dot(stack, tri)
        total = _dot(stack, ones)
        fold = lambda a: a[0:heads] + a[heads:2 * heads] + a[2 * heads:3 * heads]
        c_ref[0, :, lanes] = fold(within) + carry
        carry = carry + fold(total)


def _cumsum_lanes(lf):
    b, heads, t = lf.shape
    blk = lambda i: (i, 0, 0)
    return pl.pallas_call(
        _cumsum_lanes_kernel,
        grid=(b,),
        in_specs=[pl.BlockSpec((1, heads, t), blk)],
        out_specs=pl.BlockSpec((1, heads, t), blk),
        out_shape=jax.ShapeDtypeStruct((b, heads, t), F32),
        compiler_params=_cparams(1), name="cumsum_lanes",
    )(lf)


def _piece_selector(shape, axis, head):
    idx = lax.broadcasted_iota(jnp.int32, shape, axis)
    hit = jnp.where(idx < N_C_PIECES * N_FOX_HEADS, jnp.where((idx & (N_FOX_HEADS - 1)) == head, 1.0, 0.0), 0.0)
    return hit


_ROW_M, _ROW_MX, _ROW_ALPHA = 0, 1, 2


def _fox_prompt_kernel(qt_ref, kt_ref, vt_ref, c_ref, o_ref, kaug_ref, vta_ref, qa_ref, s_ref, p_ref, acc_ref,
                       stat_ref, *, t, tq, blk):
    pair = pl.program_id(1)
    qi = pl.program_id(2)
    nblk = t // blk
    diag_blocks = tq // blk
    d = FOX_HEAD_DIM
    heads = range(HEADS_PER_LANE_BLOCK)

    @pl.when(qi == 0)
    def _build():
        ones = jnp.ones((ONES_ROWS, blk), F32)

        def body(j, carry):
            off = pl.multiple_of(j * blk, blk)
            kaug_ref[pl.ds(off, blk), 0:LANES] = kt_ref[0, :, pl.ds(off, blk)].T.astype(BF16)
            kaug_ref[pl.ds(off, blk), LANES:2 * LANES] = c_ref[0, pl.ds(off, blk), :]
            for h in heads:
                vh = vt_ref[0, h * d:(h + 1) * d, pl.ds(off, blk)]
                vta_ref[j, h] = jnp.concatenate([vh, ones], axis=0).astype(BF16)
            return carry

        lax.fori_loop(0, nblk, body, 0)

    qt = qt_ref[0]
    row = lax.broadcasted_iota(jnp.int32, (LANES, tq), 0)
    for h in heads:
        in_head = _indicator(row, h * d, (h + 1) * d).astype(BF16)
        sel = _piece_selector((LANES, tq), 0, pair * HEADS_PER_LANE_BLOCK + h).astype(BF16)
        qa_ref[h] = jnp.concatenate([qt * in_head, sel], axis=0)

    acc_ref[...] = jnp.zeros_like(acc_ref)
    stat_ref[...] = jnp.full(stat_ref.shape, MASK_VALUE, F32)
    krow = lax.broadcasted_iota(jnp.int32, (blk, tq), 0)
    qcol = lax.broadcasted_iota(jnp.int32, (blk, tq), 1)

    def stat(h, slot, kind):
        r = (h * 2 + slot) * 4 + kind
        return slice(r, r + 1)

    def stage_a(j, slot, diag):
        off = pl.multiple_of(j * blk, blk)
        kb = kaug_ref[pl.ds(off, blk), :]
        for h in heads:
            s = _dot(kb, qa_ref[h])
            if diag is not None:
                s = jnp.where(krow + diag * blk <= qcol, s, MASK_VALUE)
            s_ref[slot, h] = s
            stat_ref[stat(h, slot, _ROW_MX), :] = jnp.max(s, axis=0, keepdims=True)

    def stage_b(slot):
        for h in heads:
            m_old = stat_ref[stat(h, 0, _ROW_M), :]
            m_new = jnp.maximum(m_old, stat_ref[stat(h, slot, _ROW_MX), :])
            stat_ref[stat(h, slot, _ROW_ALPHA), :] = jnp.exp2(m_old - m_new)
            stat_ref[stat(h, 0, _ROW_M), :] = m_new
            p_ref[slot, h] = jnp.exp2(s_ref[slot, h] - m_new).astype(BF16)

    def stage_c(j, slot):
        for h in heads:
            alpha = stat_ref[stat(h, slot, _ROW_ALPHA), :]
            acc_ref[h] = acc_ref[h] * alpha + _dot(vta_ref[j, h], p_ref[slot, h])

    nfull = qi * diag_blocks

    @pl.when(qi > 0)
    def _pipelined():
        stage_a(0, 0, None)
        stage_b(0)
        stage_a(1, 1, None)

        def two_steps(j):
            stage_c(j - 2, 0)
            stage_b(1)
            stage_a(j, 0, None)
            stage_c(j - 1, 1)
            stage_b(0)
            stage_a(j + 1, 1, None)

        def body(u, carry):
            two_steps(2 + 4 * u)
            two_steps(4 + 4 * u)
            return carry

        pairs_left = qi - 1
        lax.fori_loop(0, pairs_left // 2, body, 0)

        @pl.when(pairs_left % 2 == 1)
        def _odd_group():
            two_steps(2 * pairs_left)
        for dg in range(diag_blocks):
            stage_c(nfull + dg - 2, dg % 2)
            stage_b((dg + 1) % 2)
            stage_a(nfull + dg, dg % 2, dg)
        stage_c(nfull + diag_blocks - 2, 0)
        stage_b(1)
        stage_c(nfull + diag_blocks - 1, 1)

    @pl.when(qi == 0)
    def _first_block():
        for dg in range(diag_blocks):
            stage_a(dg, dg % 2, dg)
            stage_b(dg % 2)
            stage_c(dg, dg % 2)

    outs = []
    for h in heads:
        a = acc_ref[h]
        outs.append(a[0:d] / a[d:d + 1])
    o_ref[0] = jnp.concatenate(outs, axis=0).T.astype(BF16)


def _fox_prompt(qt, kt, vt, c):
    b, _, t = qt.shape
    blk = FOX_BLOCK
    tq = FOX_Q_BLOCK
    assert tq == 2 * blk, "the pipeline keeps two key blocks in flight and slot = block parity"
    nblk = t // blk
    hp = HEADS_PER_LANE_BLOCK
    acc_rows = FOX_HEAD_DIM + ONES_ROWS
    per_pair_all_time = lambda i, p, j: (i, p, 0)
    return pl.pallas_call(
        functools.partial(_fox_prompt_kernel, t=t, tq=tq, blk=blk),
        grid=(b, N_HEAD_PAIRS, t // tq),
        in_specs=[pl.BlockSpec((1, LANES, tq), lambda i, p, j: (i, p, j)),
                  pl.BlockSpec((1, LANES, t), per_pair_all_time),
                  pl.BlockSpec((1, LANES, t), per_pair_all_time),
                  pl.BlockSpec((1, t, LANES), lambda i, p, j: (i, 0, 0))],
        out_specs=pl.BlockSpec((1, tq, LANES), lambda i, p, j: (i, j, p)),
        out_shape=jax.ShapeDtypeStruct((b, t, FOX_WIDTH), BF16),
        scratch_shapes=[pltpu.VMEM((t, 2 * LANES), BF16),
                        pltpu.VMEM((nblk, hp, acc_rows, blk), BF16),
                        pltpu.VMEM((hp, 2 * LANES, tq), BF16),
                        pltpu.VMEM((2, hp, blk, tq), F32),
                        pltpu.VMEM((2, hp, blk, tq), BF16),
                        pltpu.VMEM((hp, acc_rows, tq), F32),
                        pltpu.VMEM((hp * 2 * 4, tq), F32)],
        compiler_params=_cparams(3), name="fox_prompt",
    )(qt, kt, vt, c)


def _fox_sample_kernel(q_ref, kn_ref, vn_ref, kc_ref, vc_ref, c_ref, o_ref, *, past, new, pairs):
    group = pl.program_id(1)
    d = FOX_HEAD_DIM
    rows = HEADS_PER_LANE_BLOCK * new
    lane = lax.broadcasted_iota(jnp.int32, (rows, LANES), 1)
    qrow = lax.broadcasted_iota(jnp.int32, (rows, LANES), 0)
    head_lanes = jnp.where(qrow < new, _indicator(lane, 0, d), _indicator(lane, d, 2 * d)).astype(BF16)
    out_lane = lax.broadcasted_iota(jnp.int32, (new, LANES), 1)
    nrow = lax.broadcasted_iota(jnp.int32, (rows, new), 0)
    ncol = lax.broadcasted_iota(jnp.int32, (rows, new), 1)
    causal = ncol <= (nrow & (new - 1))
    for pp in range(pairs):
        lanes = slice(pp * LANES, (pp + 1) * LANES)
        head0 = (group * pairs + pp) * HEADS_PER_LANE_BLOCK
        tpad = c_ref.shape[2]
        bias = jnp.concatenate(
            [jnp.broadcast_to(c_ref[0, pl.ds(head0 + h, 1), :], (new, tpad)) for h in range(HEADS_PER_LANE_BLOCK)],
            axis=0)
        q = q_ref[0, :, lanes]
        q2 = jnp.concatenate([q, q], axis=0) * head_lanes
        s_c = _dot(q2, kc_ref[0, lanes, :].astype(BF16)) - bias[:, 0:past]
        s_n = _dot_nt(q2, kn_ref[0, :, lanes].astype(BF16)) - bias[:, past:past + new]
        s_n = jnp.where(causal, s_n, MASK_VALUE)
        m = jnp.maximum(jnp.max(s_c, axis=-1, keepdims=True), jnp.max(s_n, axis=-1, keepdims=True))
        e_c = jnp.exp(s_c - m)
        e_n = jnp.exp(s_n - m)
        l = jnp.sum(e_c, axis=-1, keepdims=True) + jnp.sum(e_n, axis=-1, keepdims=True)
        o2 = _dot_nt(e_c.astype(BF16), vc_ref[0, lanes, :].astype(BF16))
        o2 = (o2 + _dot(e_n.astype(BF16), vn_ref[0, :, lanes].astype(BF16))) / l
        o = jnp.where(out_lane < d, o2[0:new], o2[new:rows])
        o_ref[0, :, lanes] = o.astype(BF16)


def _fox_sample(q, kn, vn, kct, vct, c):
    b, new, _ = q.shape
    past = kct.shape[2]
    pairs = SAMPLE_PAIRS_PER_STEP
    width = pairs * LANES
    tok = lambda i, g: (i, 0, g)
    feat = lambda i, g: (i, g, 0)
    return pl.pallas_call(
        functools.partial(_fox_sample_kernel, past=past, new=new, pairs=pairs),
        grid=(b, N_HEAD_PAIRS // pairs),
        in_specs=[pl.BlockSpec((1, new, width), tok), pl.BlockSpec((1, new, width), tok),
                  pl.BlockSpec((1, new, width), tok), pl.BlockSpec((1, width, past), feat),
                  pl.BlockSpec((1, width, past), feat),
                  pl.BlockSpec((1,) + c.shape[1:], lambda i, g: (i, 0, 0))],
        out_specs=pl.BlockSpec((1, new, width), tok),
        out_shape=jax.ShapeDtypeStruct((b, new, FOX_WIDTH), BF16),
        compiler_params=_cparams(2), name="fox_sample",
    )(q, kn, vn, kct, vct, c)


def _retention_kernel(lg_ref, q_ref, k_ref, v_ref, g_ref, gn_ref, s0_ref, o_ref, st_ref,
                      decay_ref, qdec_ref, kdec_ref, *, blk):
    lg = lg_ref[0, 0:1, 0:1]

    @pl.when(pl.program_id(2) == 0)
    def _():
        st_ref[0, 0] = s0_ref[0, 0]
        n = lax.broadcasted_iota(jnp.int32, (blk, blk), 0)
        m = lax.broadcasted_iota(jnp.int32, (blk, blk), 1)
        shift = CHUNK.bit_length() - 1
        dist = jnp.abs(n - m).astype(F32)
        decay_ref[...] = jnp.where((m >> shift) <= (n >> shift), jnp.exp(lg * dist), 0.0)
        pos = lax.broadcasted_iota(jnp.int32, (blk, RET_KEY_DIM), 0).astype(F32)
        qdec_ref[...] = jnp.exp(lg * (pos + 1.0))
        kdec_ref[...] = jnp.exp(lg * (blk - 1.0 - pos))

    q = q_ref[0]
    k = k_ref[0]
    v = v_ref[0]
    state = st_ref[0, 0]
    s = _dot_nt(q, k) * decay_ref[...]
    qd = (q.astype(F32) * qdec_ref[...]).astype(BF16)
    y = _dot(s.astype(BF16), v) + _dot(qd, state.astype(BF16))
    kd = k.astype(F32) * kdec_ref[...]
    st_ref[0, 0] = jnp.exp(lg * blk) * state + _dot(kd.T.astype(BF16), v)

    yn = y * lax.rsqrt(jnp.mean(y * y, axis=-1, keepdims=True) + EPS) * gn_ref[0]
    g = g_ref[0]
    o_ref[0] = (g * jax.nn.sigmoid(g) * yn).astype(BF16)


def _retention(lg, q, k, v, g, gn, s0, blk):
    b, t, _ = q.shape
    qk = lambda i, h, j: (i, j, h)
    per_head_state = lambda i, h, j: (i, h, 0, 0)
    return pl.pallas_call(
        functools.partial(_retention_kernel, blk=blk),
        grid=(b, N_RET_HEADS, t // blk),
        in_specs=[pl.BlockSpec((1, 8, LANES), lambda i, h, j: (h, 0, 0)),
                  pl.BlockSpec((1, blk, RET_KEY_DIM), qk), pl.BlockSpec((1, blk, RET_KEY_DIM), qk),
                  pl.BlockSpec((1, blk, RET_VAL_DIM), qk), pl.BlockSpec((1, blk, RET_VAL_DIM), qk),
                  pl.BlockSpec((1, 1, RET_VAL_DIM), lambda i, h, j: (h, 0, 0)),
                  pl.BlockSpec((1, 1, RET_KEY_DIM, RET_VAL_DIM), per_head_state)],
        out_specs=[pl.BlockSpec((1, blk, RET_VAL_DIM), qk),
                   pl.BlockSpec((1, 1, RET_KEY_DIM, RET_VAL_DIM), per_head_state)],
        out_shape=[jax.ShapeDtypeStruct((b, t, RET_V_WIDTH), BF16),
                   jax.ShapeDtypeStruct((b, N_RET_HEADS, RET_KEY_DIM, RET_VAL_DIM), F32)],
        scratch_shapes=[pltpu.VMEM((blk, blk), F32), pltpu.VMEM((blk, RET_KEY_DIM), F32),
                        pltpu.VMEM((blk, RET_KEY_DIM), F32)],
        compiler_params=_cparams(3), name="retention",
    )(lg, q, k, v, g, gn, s0)


def _mixer_kernel(x_ref, oa_ref, ob_ref, g_ref, wg_ref, wa_ref, wb_ref, wo_ref, x1_ref):
    x = x_ref[...]
    h = _rmsnorm(x, g_ref[...]).astype(BF16)
    gate_a = jax.nn.sigmoid(_dot(h, wg_ref[:, 0:D_MODEL]))
    gate_b = jax.nn.sigmoid(_dot(h, wg_ref[:, D_MODEL:2 * D_MODEL]))
    merged = gate_a * _dot(oa_ref[...], wa_ref[...]) + gate_b * _dot(ob_ref[...], wb_ref[...])
    x1_ref[...] = x + _dot(merged.astype(BF16), wo_ref[...])


def _mixer(x, oa, ob, g, wg, wa, wb, wo):
    n = x.shape[0]
    tm = MIX_TOKEN_TILE
    tok = lambda i: (i, 0)
    fixed = lambda i: (0, 0)
    return pl.pallas_call(
        _mixer_kernel,
        grid=(n // tm,),
        in_specs=[pl.BlockSpec((tm, D_MODEL), tok), pl.BlockSpec((tm, FOX_WIDTH), tok),
                  pl.BlockSpec((tm, RET_V_WIDTH), tok), pl.BlockSpec((1, D_MODEL), fixed),
                  pl.BlockSpec((D_MODEL, 2 * D_MODEL), fixed), pl.BlockSpec((FOX_WIDTH, D_MODEL), fixed),
                  pl.BlockSpec((RET_V_WIDTH, D_MODEL), fixed), pl.BlockSpec((D_MODEL, D_MODEL), fixed)],
        out_specs=pl.BlockSpec((tm, D_MODEL), tok),
        out_shape=jax.ShapeDtypeStruct((n, D_MODEL), F32),
        compiler_params=_cparams(1), name="mixer",
    )(x, oa, ob, g, wg, wa, wb, wo)


def _mlp_kernel(x_ref, g_ref, wu_ref, wd_ref, gf_ref, y_ref):
    x = x_ref[...]
    h = _rmsnorm(x, g_ref[...]).astype(BF16)
    acc = x
    for c in range(D_FF // D_MODEL):
        cols = slice(c * D_MODEL, (c + 1) * D_MODEL)
        u = jnp.square(jnp.maximum(_dot(h, wu_ref[:, cols]), 0.0)).astype(BF16)
        acc = acc + _dot(u, wd_ref[cols, :])
    y_ref[...] = _rmsnorm(acc, gf_ref[...])


def _mlp(x, g, wu, wd, gf):
    n = x.shape[0]
    tm = MIX_TOKEN_TILE
    tok = lambda i: (i, 0)
    fixed = lambda i: (0, 0)
    return pl.pallas_call(
        _mlp_kernel,
        grid=(n // tm,),
        in_specs=[pl.BlockSpec((tm, D_MODEL), tok), pl.BlockSpec((1, D_MODEL), fixed),
                  pl.BlockSpec((D_MODEL, D_FF), fixed), pl.BlockSpec((D_FF, D_MODEL), fixed),
                  pl.BlockSpec((1, D_MODEL), fixed)],
        out_specs=pl.BlockSpec((tm, D_MODEL), tok),
        out_shape=jax.ShapeDtypeStruct((n, D_MODEL), F32),
        compiler_params=_cparams(1), name="mlp",
    )(x, g, wu, wd, gf)


def _rope_tables(pos):
    inv_freq = ROPE_BASE ** (-jnp.arange(0, RET_KEY_DIM, 2, dtype=F32) / RET_KEY_DIM)
    ang = pos.astype(F32)[:, None] * inv_freq[None, :]
    return jnp.cos(ang), jnp.sin(ang)


def _after_attention(x, pos, o_a, params, ret_blk, state0):
    b, t, _ = x.shape
    n = b * t
    xf = x.reshape(n, D_MODEL)
    cos, sin = _rope_tables(pos)
    reps = max(1, TOKEN_TILE // t)
    q_r, k_r = _proj_ret_qk(xf, params["g_attn"], params["w_ret_qk"], jnp.tile(cos, (reps, 1)),
                            jnp.tile(sin, (reps, 1)))
    v_r, g_r = _proj_ret_vg(xf, params["g_attn"], params["w_ret_vg"])
    o_b, state = _retention(params["lg"], q_r.reshape(b, t, RET_QK_WIDTH), k_r.reshape(b, t, RET_QK_WIDTH),
                            v_r.reshape(b, t, RET_V_WIDTH), g_r.reshape(b, t, RET_V_WIDTH),
                            params["g_ret_norm"], state0, ret_blk)
    x1 = _mixer(xf, o_a.reshape(n, FOX_WIDTH), o_b.reshape(n, RET_V_WIDTH), params["g_attn"],
                params["w_gates"], params["w_branch_a"], params["w_branch_b"], params["w_out"])
    y = _mlp(x1, params["g_mlp"], params["w_up"], params["w_down"], params["g_final"])
    return y.reshape(b, t, D_MODEL), state[None]


def kernel(x_prompt, x_sample, cache_fox_k, cache_fox_v, cache_fox_logf, state_ret, g_attn, w_in, b_forget,
           g_ret_norm, w_branch, w_out, g_mlp, w_up, w_down, g_final):
    assert w_in.shape[0] == 1, "single-layer trunk"
    wi = w_in[0]
    fox_cols = 3 * FOX_WIDTH + N_FOX_HEADS
    qk_end = fox_cols + 2 * RET_QK_WIDTH
    vg_end = qk_end + 2 * RET_V_WIDTH
    lane_pad = LANES - N_FOX_HEADS
    lg = jnp.log(1.0 - 2.0 ** (-5.0 - jnp.arange(N_RET_HEADS, dtype=F32)))
    w_fox = jnp.pad(wi[:, :fox_cols], ((0, 0), (0, lane_pad))).astype(BF16)
    b_row = jnp.pad(b_forget[0], (0, lane_pad)).reshape(1, LANES)
    params = {
        "g_attn": g_attn[0].reshape(1, D_MODEL),
        "w_ret_qk": wi[:, fox_cols:qk_end].astype(BF16),
        "w_ret_vg": wi[:, qk_end:vg_end].astype(BF16),
        "w_gates": wi[:, vg_end:].astype(BF16),
        "lg": jnp.broadcast_to(lg[:, None, None], (N_RET_HEADS, 8, LANES)),
        "g_ret_norm": g_ret_norm[0].reshape(N_RET_HEADS, 1, RET_VAL_DIM),
        "w_branch_a": w_branch[0, :FOX_WIDTH].astype(BF16),
        "w_branch_b": w_branch[0, FOX_WIDTH:].astype(BF16),
        "w_out": w_out[0].astype(BF16),
        "g_mlp": g_mlp[0].reshape(1, D_MODEL),
        "w_up": w_up[0].astype(BF16),
        "w_down": w_down[0].astype(BF16),
        "g_final": g_final.reshape(1, D_MODEL),
    }
    bp, tp, _ = x_prompt.shape
    bs, ts, _ = x_sample.shape
    past = cache_fox_k.shape[2]

    qt, kt, vt, lft, lf_pad = _proj_fox_t(
        x_prompt, params["g_attn"], wi[:, :fox_cols].T.astype(BF16), w_fox[:, 3 * FOX_WIDTH:],
        jnp.broadcast_to(b_forget[0][:, None], (N_FOX_HEADS, LANES)), b_row)
    o_a = _fox_prompt(qt, kt, vt, _cumsum_pieces(lf_pad, CUMSUM_TILE, LOG2E))
    zero_state = jnp.zeros((bp, N_RET_HEADS, RET_KEY_DIM, RET_VAL_DIM), F32)
    yp, sp = _after_attention(x_prompt, jnp.arange(tp), o_a, params, RET_BLOCK, zero_state)
    to_heads = lambda a: jnp.transpose(a.reshape(bp, N_FOX_HEADS, FOX_HEAD_DIM, tp), (0, 3, 1, 2))[None]
    kp, vp = to_heads(kt), to_heads(vt)
    fp = jnp.transpose(lft, (0, 2, 1))[None]

    q, k, v, lf = _proj_fox(x_sample.reshape(bs * ts, D_MODEL), params["g_attn"], w_fox, b_row)
    feature_major = lambda a: jnp.transpose(a[0], (0, 2, 3, 1)).reshape(bs, FOX_WIDTH, past)
    lf_time = jnp.concatenate([jnp.transpose(cache_fox_logf[0], (0, 2, 1)),
                               jnp.transpose(lf.reshape(bs, ts, N_FOX_HEADS), (0, 2, 1))], axis=2)
    lf_time = jnp.pad(lf_time, ((0, 0), (0, 0), (0, -(past + ts) % LANES)))
    o_a = _fox_sample(q.reshape(bs, ts, FOX_WIDTH), k.reshape(bs, ts, FOX_WIDTH), v.reshape(bs, ts, FOX_WIDTH),
                      feature_major(cache_fox_k), feature_major(cache_fox_v), _cumsum_lanes(lf_time))
    ys, ss = _after_attention(x_sample, past + jnp.arange(ts), o_a, params, CHUNK, state_ret[0])
    shape5 = (1, bs, ts, N_FOX_HEADS, FOX_HEAD_DIM)
    return (yp, ys, kp, vp, fp, sp, k.reshape(shape5), v.reshape(shape5),
            lf.reshape(1, bs, ts, N_FOX_HEADS), ss)
```

```python
import functools

import jax
import jax.numpy as jnp
from jax import lax
from jax.experimental import pallas as pl
from jax.experimental.pallas import tpu as pltpu

D_MODEL = 1024
N_FOX_HEADS = 16
FOX_HEAD_DIM = 64
FOX_WIDTH = N_FOX_HEADS * FOX_HEAD_DIM
N_RET_HEADS = 4
RET_KEY_DIM = 256
RET_VAL_DIM = 512
RET_QK_WIDTH = N_RET_HEADS * RET_KEY_DIM
RET_V_WIDTH = N_RET_HEADS * RET_VAL_DIM
D_FF = 4 * D_MODEL
CHUNK = 64
ROPE_BASE = 10000.0
EPS = 1e-6

LANES = 128
HEADS_PER_LANE_BLOCK = LANES // FOX_HEAD_DIM
N_HEAD_PAIRS = N_FOX_HEADS // HEADS_PER_LANE_BLOCK
N_C_PIECES = 3
ONES_ROWS = 16
MASK_VALUE = -1e30
LOG2E = 1.4426950408889634
VMEM_LIMIT = 56 * 1024 * 1024

TOKEN_TILE = 512
MIX_TOKEN_TILE = 256
FOX_BLOCK = 256
FOX_Q_BLOCK = 512
SAMPLE_PAIRS_PER_STEP = 4
RET_BLOCK = 256
CUMSUM_TILE = 512
SKIP_EXPONENT = 160.0
NORM_SLACK = 1.01

BF16 = jnp.bfloat16
F32 = jnp.float32


def _cparams(n_axes):
    return pltpu.CompilerParams(dimension_semantics=("arbitrary",) * n_axes,
                                vmem_limit_bytes=VMEM_LIMIT)


def _dot(a, b):
    return jnp.dot(a, b, preferred_element_type=F32)


def _dot_nt(a, b):
    return lax.dot_general(a, b, (((1,), (1,)), ((), ())), preferred_element_type=F32)


def _rmsnorm(x, g):
    return x * lax.rsqrt(jnp.mean(x * x, axis=-1, keepdims=True) + EPS) * g


def _log_sigmoid(z):
    return -(jnp.maximum(-z, 0.0) + jnp.log(1.0 + jnp.exp(-jnp.abs(z))))


def _split3(x):
    a = x.astype(BF16)
    r = x - a.astype(F32)
    b = r.astype(BF16)
    c = (r - b.astype(F32)).astype(BF16)
    return a, b, c


def _indicator(idx, lo, hi):
    return jnp.where(idx >= lo, jnp.where(idx < hi, 1.0, 0.0), 0.0)


def _proj_fox_t_kernel(x_ref, g_ref, wt_ref, wf_ref, bcol_ref, brow_ref, qt_ref, kt_ref, vt_ref, lft_ref, lfp_ref,
                       qn_ref, kn_ref):
    h = _rmsnorm(x_ref[0], g_ref[...])
    ht = h.T.astype(BF16)
    w = FOX_WIDTH
    qs = _dot(wt_ref[0:w, :], ht) * (FOX_HEAD_DIM ** -0.5 * LOG2E)
    kf = _dot(wt_ref[w:2 * w, :], ht)
    qt_ref[0] = qs.astype(BF16)
    kt_ref[0] = kf
    for src, dst in ((qs, qn_ref), (kf, kn_ref)):
        sq = jnp.sum((src * src).reshape(N_FOX_HEADS, FOX_HEAD_DIM, src.shape[1]), axis=1)
        dst[0, 0] = jnp.broadcast_to(jnp.sqrt(jnp.max(sq, axis=1, keepdims=True)), (N_FOX_HEADS, LANES))
    vt_ref[0] = _dot(wt_ref[2 * w:3 * w, :], ht)
    lft_ref[0] = _log_sigmoid(_dot(wt_ref[3 * w:3 * w + N_FOX_HEADS, :], ht) + bcol_ref[:, 0:1])
    lfp_ref[0] = _log_sigmoid(_dot(h.astype(BF16), wf_ref[...]) + brow_ref[...])


def _proj_fox_t(x, g, wt, wf, bcol, brow):
    b, t, _ = x.shape
    tm = TOKEN_TILE
    fixed = lambda i, j: (0, 0)
    feat = lambda i, j: (i, 0, j)
    return pl.pallas_call(
        _proj_fox_t_kernel,
        grid=(b, t // tm),
        in_specs=[pl.BlockSpec((1, tm, D_MODEL), lambda i, j: (i, j, 0)), pl.BlockSpec((1, D_MODEL), fixed),
                  pl.BlockSpec(wt.shape, fixed), pl.BlockSpec(wf.shape, fixed),
                  pl.BlockSpec(bcol.shape, fixed), pl.BlockSpec(brow.shape, fixed)],
        out_specs=[pl.BlockSpec((1, FOX_WIDTH, tm), feat), pl.BlockSpec((1, FOX_WIDTH, tm), feat),
                   pl.BlockSpec((1, FOX_WIDTH, tm), feat), pl.BlockSpec((1, N_FOX_HEADS, tm), feat),
                   pl.BlockSpec((1, tm, LANES), lambda i, j: (i, j, 0)),
                   pl.BlockSpec((1, 1, N_FOX_HEADS, LANES), lambda i, j: (i, j, 0, 0)),
                   pl.BlockSpec((1, 1, N_FOX_HEADS, LANES), lambda i, j: (i, j, 0, 0))],
        out_shape=[jax.ShapeDtypeStruct((b, FOX_WIDTH, t), BF16), jax.ShapeDtypeStruct((b, FOX_WIDTH, t), F32),
                   jax.ShapeDtypeStruct((b, FOX_WIDTH, t), F32), jax.ShapeDtypeStruct((b, N_FOX_HEADS, t), F32),
                   jax.ShapeDtypeStruct((b, t, LANES), F32),
                   jax.ShapeDtypeStruct((b, t // tm, N_FOX_HEADS, LANES), F32),
                   jax.ShapeDtypeStruct((b, t // tm, N_FOX_HEADS, LANES), F32)],
        compiler_params=_cparams(2), name="proj_fox_t",
    )(x, g, wt, wf, bcol, brow)


def _proj_fox_kernel(x_ref, g_ref, w_ref, b_ref, q_ref, k_ref, v_ref, lf_ref):
    h = _rmsnorm(x_ref[...], g_ref[...]).astype(BF16)
    w = FOX_WIDTH
    q_ref[...] = (_dot(h, w_ref[:, 0:w]) * (FOX_HEAD_DIM ** -0.5)).astype(BF16)
    k_ref[...] = _dot(h, w_ref[:, w:2 * w])
    v_ref[...] = _dot(h, w_ref[:, 2 * w:3 * w])
    lf = _log_sigmoid(_dot(h, w_ref[:, 3 * w:3 * w + LANES]) + b_ref[...])
    lf_ref[...] = lf[:, 0:N_FOX_HEADS]


def _proj_fox(x, g, w, b):
    n = x.shape[0]
    tm = TOKEN_TILE
    tok = lambda i: (i, 0)
    fixed = lambda i: (0, 0)
    return pl.pallas_call(
        _proj_fox_kernel,
        grid=(n // tm,),
        in_specs=[pl.BlockSpec((tm, D_MODEL), tok), pl.BlockSpec((1, D_MODEL), fixed),
                  pl.BlockSpec(w.shape, fixed), pl.BlockSpec((1, LANES), fixed)],
        out_specs=[pl.BlockSpec((tm, FOX_WIDTH), tok), pl.BlockSpec((tm, FOX_WIDTH), tok),
                   pl.BlockSpec((tm, FOX_WIDTH), tok), pl.BlockSpec((tm, N_FOX_HEADS), tok)],
        out_shape=[jax.ShapeDtypeStruct((n, FOX_WIDTH), BF16), jax.ShapeDtypeStruct((n, FOX_WIDTH), F32),
                   jax.ShapeDtypeStruct((n, FOX_WIDTH), F32), jax.ShapeDtypeStruct((n, N_FOX_HEADS), F32)],
        compiler_params=_cparams(1), name="proj_fox",
    )(x, g, w, b)


def _proj_ret_qk_kernel(x_ref, g_ref, w_ref, cos_ref, sin_ref, q_ref, k_ref):
    h = _rmsnorm(x_ref[...], g_ref[...]).astype(BF16)
    cos = cos_ref[...]
    sin = sin_ref[...]
    half = RET_KEY_DIM // 2
    for out_ref, base, scale in ((q_ref, 0, 1.0), (k_ref, RET_QK_WIDTH, RET_KEY_DIM ** -0.5)):
        for hd in range(N_RET_HEADS):
            lo = hd * RET_KEY_DIM
            z = _dot(h, w_ref[:, base + lo:base + lo + RET_KEY_DIM])
            x1 = z[:, :half]
            x2 = z[:, half:]
            out_ref[:, lo:lo + half] = ((x1 * cos - x2 * sin) * scale).astype(BF16)
            out_ref[:, lo + half:lo + RET_KEY_DIM] = ((x1 * sin + x2 * cos) * scale).astype(BF16)


def _proj_ret_qk(x, g, w, cos, sin):
    n = x.shape[0]
    tm = TOKEN_TILE
    period = cos.shape[0] // tm
    tok = lambda i: (i, 0)
    fixed = lambda i: (0, 0)
    pos = lambda i: (i % period, 0)
    half = RET_KEY_DIM // 2
    return pl.pallas_call(
        _proj_ret_qk_kernel,
        grid=(n // tm,),
        in_specs=[pl.BlockSpec((tm, D_MODEL), tok), pl.BlockSpec((1, D_MODEL), fixed),
                  pl.BlockSpec((D_MODEL, 2 * RET_QK_WIDTH), fixed),
                  pl.BlockSpec((tm, half), pos), pl.BlockSpec((tm, half), pos)],
        out_specs=[pl.BlockSpec((tm, RET_QK_WIDTH), tok), pl.BlockSpec((tm, RET_QK_WIDTH), tok)],
        out_shape=[jax.ShapeDtypeStruct((n, RET_QK_WIDTH), BF16)] * 2,
        compiler_params=_cparams(1), name="proj_ret_qk",
    )(x, g, w, cos, sin)


def _proj_ret_vg_kernel(x_ref, g_ref, w_ref, v_ref, gr_ref):
    h = _rmsnorm(x_ref[...], g_ref[...]).astype(BF16)
    v_ref[...] = _dot(h, w_ref[:, 0:RET_V_WIDTH]).astype(BF16)
    gr_ref[...] = _dot(h, w_ref[:, RET_V_WIDTH:2 * RET_V_WIDTH])


def _proj_ret_vg(x, g, w):
    n = x.shape[0]
    tm = TOKEN_TILE
    tok = lambda i: (i, 0)
    fixed = lambda i: (0, 0)
    return pl.pallas_call(
        _proj_ret_vg_kernel,
        grid=(n // tm,),
        in_specs=[pl.BlockSpec((tm, D_MODEL), tok), pl.BlockSpec((1, D_MODEL), fixed),
                  pl.BlockSpec((D_MODEL, 2 * RET_V_WIDTH), fixed)],
        out_specs=[pl.BlockSpec((tm, RET_V_WIDTH), tok), pl.BlockSpec((tm, RET_V_WIDTH), tok)],
        out_shape=[jax.ShapeDtypeStruct((n, RET_V_WIDTH), BF16), jax.ShapeDtypeStruct((n, RET_V_WIDTH), F32)],
        compiler_params=_cparams(1), name="proj_ret_vg",
    )(x, g, w)


def _cumsum_kernel(lf_ref, c_ref, cend_ref, carry_ref, *, tb, scale):
    @pl.when(pl.program_id(1) == 0)
    def _():
        carry_ref[...] = jnp.zeros_like(carry_ref)

    row = lax.broadcasted_iota(jnp.int32, (tb, tb), 0)
    col = lax.broadcasted_iota(jnp.int32, (tb, tb), 1)
    tri = jnp.where(col <= row, 1.0, 0.0).astype(BF16)
    c = carry_ref[0:1, :]
    for piece in _split3(lf_ref[0]):
        c = c + _dot(tri, piece)
    carry_ref[...] = jnp.broadcast_to(c[tb - 1:tb, :], carry_ref.shape)
    erow = lax.broadcasted_iota(jnp.int32, (8, LANES), 0)
    ends = jnp.zeros((8, LANES), F32)
    for e in range(tb // FOX_BLOCK):
        ends = jnp.where(erow == e, c[(e + 1) * FOX_BLOCK - 1:(e + 1) * FOX_BLOCK, :] * scale, ends)
    cend_ref[0, 0] = ends

    prow = lax.broadcasted_iota(jnp.int32, (LANES, LANES), 0)
    pcol = lax.broadcasted_iota(jnp.int32, (LANES, LANES), 1)
    out = jnp.zeros((tb, LANES), F32)
    for p, piece in enumerate(_split3(-c * scale)):
        place = jnp.where(pcol == prow + p * N_FOX_HEADS, jnp.where(prow < N_FOX_HEADS, 1.0, 0.0), 0.0)
        out = out + _dot(piece, place.astype(BF16))
    c_ref[0] = out.astype(BF16)


def _cumsum_pieces(lf, tb, scale):
    b, t, _ = lf.shape
    blk = lambda i, j: (i, j, 0)
    return pl.pallas_call(
        functools.partial(_cumsum_kernel, tb=tb, scale=scale),
        grid=(b, t // tb),
        in_specs=[pl.BlockSpec((1, tb, LANES), blk)],
        out_specs=[pl.BlockSpec((1, tb, LANES), blk), pl.BlockSpec((1, 1, 8, LANES), lambda i, j: (i, j, 0, 0))],
        out_shape=[jax.ShapeDtypeStruct((b, t, LANES), BF16), jax.ShapeDtypeStruct((b, t // tb, 8, LANES), F32)],
        scratch_shapes=[pltpu.VMEM((8, LANES), F32)],
        compiler_params=_cparams(2), name="cumsum",
    )(lf)


def _cumsum_lanes_kernel(lf_ref, c_ref):
    heads, t = lf_ref.shape[1], lf_ref.shape[2]
    row = lax.broadcasted_iota(jnp.int32, (LANES, LANES), 0)
    col = lax.broadcasted_iota(jnp.int32, (LANES, LANES), 1)
    tri = jnp.where(row <= col, 1.0, 0.0).astype(BF16)
    ones = jnp.ones((LANES, LANES), BF16)
    carry = jnp.zeros((heads, LANES), F32)
    for seg in range(t // LANES):
        lanes = slice(seg * LANES, (seg + 1) * LANES)
        stack = jnp.concatenate(_split3(lf_ref[0, :, lanes]), axis=0)
        within = _dot(stack, tri)
        total = _dot(stack, ones)
        fold = lambda a: a[0:heads] + a[heads:2 * heads] + a[2 * heads:3 * heads]
        c_ref[0, :, lanes] = fold(within) + carry
        carry = carry + fold(total)


def _cumsum_lanes(lf):
    b, heads, t = lf.shape
    blk = lambda i: (i, 0, 0)
    return pl.pallas_call(
        _cumsum_lanes_kernel,
        grid=(b,),
        in_specs=[pl.BlockSpec((1, heads, t), blk)],
        out_specs=pl.BlockSpec((1, heads, t), blk),
        out_shape=jax.ShapeDtypeStruct((b, heads, t), F32),
        compiler_params=_cparams(1), name="cumsum_lanes",
    )(lf)


def _piece_selector(shape, axis, head):
    idx = lax.broadcasted_iota(jnp.int32, shape, axis)
    hit = jnp.where(idx < N_C_PIECES * N_FOX_HEADS, jnp.where((idx & (N_FOX_HEADS - 1)) == head, 1.0, 0.0), 0.0)
    return hit


_ROW_M, _ROW_MX, _ROW_ALPHA = 0, 1, 2


def _fox_prompt_kernel(first_ref, qt_ref, kt_ref, vt_ref, c_ref, o_ref, kaug_ref, vta_ref, qa_ref, s_ref, p_ref, acc_ref,
                       stat_ref, *, t, tq, blk):
    pair = pl.program_id(1)
    qi = pl.program_id(2)
    nblk = t // blk
    diag_blocks = tq // blk
    d = FOX_HEAD_DIM
    heads = range(HEADS_PER_LANE_BLOCK)

    @pl.when(qi == 0)
    def _build():
        ones = jnp.ones((ONES_ROWS, blk), F32)

        def body(j, carry):
            off = pl.multiple_of(j * blk, blk)
            kaug_ref[pl.ds(off, blk), 0:LANES] = kt_ref[0, :, pl.ds(off, blk)].T.astype(BF16)
            kaug_ref[pl.ds(off, blk), LANES:2 * LANES] = c_ref[0, pl.ds(off, blk), :]
            for h in heads:
                vh = vt_ref[0, h * d:(h + 1) * d, pl.ds(off, blk)]
                vta_ref[j, h] = jnp.concatenate([vh, ones], axis=0).astype(BF16)
            return carry

        lax.fori_loop(0, nblk, body, 0)

    qt = qt_ref[0]
    row = lax.broadcasted_iota(jnp.int32, (LANES, tq), 0)
    for h in heads:
        in_head = _indicator(row, h * d, (h + 1) * d).astype(BF16)
        sel = _piece_selector((LANES, tq), 0, pair * HEADS_PER_LANE_BLOCK + h).astype(BF16)
        qa_ref[h] = jnp.concatenate([qt * in_head, sel], axis=0)

    acc_ref[...] = jnp.zeros_like(acc_ref)
    stat_ref[...] = jnp.full(stat_ref.shape, MASK_VALUE, F32)
    krow = lax.broadcasted_iota(jnp.int32, (blk, tq), 0)
    qcol = lax.broadcasted_iota(jnp.int32, (blk, tq), 1)

    def stat(h, slot, kind):
        r = (h * 2 + slot) * 4 + kind
        return slice(r, r + 1)

    def stage_a(j, slot, diag):
        off = pl.multiple_of(j * blk, blk)
        kb = kaug_ref[pl.ds(off, blk), :]
        for h in heads:
            s = _dot(kb, qa_ref[h])
            if diag is not None:
                s = jnp.where(krow + diag * blk <= qcol, s, MASK_VALUE)
            s_ref[slot, h] = s
            stat_ref[stat(h, slot, _ROW_MX), :] = jnp.max(s, axis=0, keepdims=True)

    def stage_b(slot):
        for h in heads:
            m_old = stat_ref[stat(h, 0, _ROW_M), :]
            m_new = jnp.maximum(m_old, stat_ref[stat(h, slot, _ROW_MX), :])
            stat_ref[stat(h, slot, _ROW_ALPHA), :] = jnp.exp2(m_old - m_new)
            stat_ref[stat(h, 0, _ROW_M), :] = m_new
            p_ref[slot, h] = jnp.exp2(s_ref[slot, h] - m_new).astype(BF16)

    def stage_c(j, slot):
        for h in heads:
            alpha = stat_ref[stat(h, slot, _ROW_ALPHA), :]
            acc_ref[h] = acc_ref[h] * alpha + _dot(vta_ref[j, h], p_ref[slot, h])

    nfull = qi * diag_blocks
    j0 = first_ref[(pl.program_id(0) * N_HEAD_PAIRS + pair) * (t // tq) + qi]

    @pl.when(qi > 0)
    def _pipelined():
        stage_a(j0, 0, None)
        stage_b(0)
        stage_a(j0 + 1, 1, None)

        def two_steps(j):
            stage_c(j - 2, 0)
            stage_b(1)
            stage_a(j, 0, None)
            stage_c(j - 1, 1)
            stage_b(0)
            stage_a(j + 1, 1, None)

        def body(u, carry):
            two_steps(j0 + 2 + 4 * u)
            two_steps(j0 + 4 + 4 * u)
            return carry

        pairs_left = (nfull - j0) // 2 - 1
        lax.fori_loop(0, pairs_left // 2, body, 0)

        @pl.when(pairs_left % 2 == 1)
        def _odd_group():
            two_steps(j0 + 2 * pairs_left)
        for dg in range(diag_blocks):
            stage_c(nfull + dg - 2, dg % 2)
            stage_b((dg + 1) % 2)
            stage_a(nfull + dg, dg % 2, dg)
        stage_c(nfull + diag_blocks - 2, 0)
        stage_b(1)
        stage_c(nfull + diag_blocks - 1, 1)

    @pl.when(qi == 0)
    def _first_block():
        for dg in range(diag_blocks):
            stage_a(dg, dg % 2, dg)
            stage_b(dg % 2)
            stage_c(dg, dg % 2)

    outs = []
    for h in heads:
        a = acc_ref[h]
        outs.append(a[0:d] / a[d:d + 1])
    o_ref[0] = jnp.concatenate(outs, axis=0).T.astype(BF16)


def _first_key_block(c_end, q_norm, k_norm, t, tq, blk):
    b = c_end.shape[0]
    nblk, nq, per = t // blk, t // tq, tq // blk
    ce = c_end[:, :, :CUMSUM_TILE // blk, :N_FOX_HEADS].reshape(b, nblk, N_FOX_HEADS)
    qn = q_norm[:, :, :, 0]
    kmax = jnp.max(k_norm[:, :, :, 0], axis=1)
    before = jnp.concatenate([jnp.zeros((b, 1, N_FOX_HEADS), F32), ce[:, per - 1:-1:per]], axis=1)
    decay = before[:, :, None, :] - ce[:, None, :, :]
    bound = (2.0 * NORM_SLACK * NORM_SLACK) * (qn * kmax[:, None, :])[:, :, None, :] + decay
    below = (jnp.arange(nblk)[None, :] < (jnp.arange(nq) * per)[:, None])[None, :, :, None]
    skippable = jnp.logical_and(bound <= -SKIP_EXPONENT, below)
    block = jnp.arange(nblk, dtype=jnp.int32)[None, None, :, None]
    n_skip = jnp.min(jnp.where(skippable, nblk, block), axis=2)
    n_skip = jnp.min(n_skip.reshape(b, nq, N_HEAD_PAIRS, HEADS_PER_LANE_BLOCK), axis=3)
    first = jnp.clip((n_skip // 2) * 2, 0, jnp.maximum(jnp.arange(nq) * per - 2, 0)[None, :, None])
    return jnp.transpose(first, (0, 2, 1)).reshape(-1).astype(jnp.int32)


def _fox_prompt(first, qt, kt, vt, c):
    b, _, t = qt.shape
    blk = FOX_BLOCK
    tq = FOX_Q_BLOCK
    assert tq == 2 * blk, "the pipeline keeps two key blocks in flight and slot = block parity"
    nblk = t // blk
    hp = HEADS_PER_LANE_BLOCK
    acc_rows = FOX_HEAD_DIM + ONES_ROWS
    per_pair_all_time = lambda i, p, j, first: (i, p, 0)
    grid_spec = pltpu.PrefetchScalarGridSpec(
        num_scalar_prefetch=1,
        grid=(b, N_HEAD_PAIRS, t // tq),
        in_specs=[pl.BlockSpec((1, LANES, tq), lambda i, p, j, first: (i, p, j)),
                  pl.BlockSpec((1, LANES, t), per_pair_all_time),
                  pl.BlockSpec((1, LANES, t), per_pair_all_time),
                  pl.BlockSpec((1, t, LANES), lambda i, p, j, first: (i, 0, 0))],
        out_specs=pl.BlockSpec((1, tq, LANES), lambda i, p, j, first: (i, j, p)),
        scratch_shapes=[pltpu.VMEM((t, 2 * LANES), BF16),
                        pltpu.VMEM((nblk, hp, acc_rows, blk), BF16),
                        pltpu.VMEM((hp, 2 * LANES, tq), BF16),
                        pltpu.VMEM((2, hp, blk, tq), F32),
                        pltpu.VMEM((2, hp, blk, tq), BF16),
                        pltpu.VMEM((hp, acc_rows, tq), F32),
                        pltpu.VMEM((hp * 2 * 4, tq), F32)])
    return pl.pallas_call(
        functools.partial(_fox_prompt_kernel, t=t, tq=tq, blk=blk),
        grid_spec=grid_spec,
        out_shape=jax.ShapeDtypeStruct((b, t, FOX_WIDTH), BF16),
        compiler_params=_cparams(3), name="fox_prompt",
    )(first, qt, kt, vt, c)


def _fox_sample_kernel(q_ref, kn_ref, vn_ref, kc_ref, vc_ref, c_ref, o_ref, *, past, new, pairs):
    group = pl.program_id(1)
    d = FOX_HEAD_DIM
    rows = HEADS_PER_LANE_BLOCK * new
    lane = lax.broadcasted_iota(jnp.int32, (rows, LANES), 1)
    qrow = lax.broadcasted_iota(jnp.int32, (rows, LANES), 0)
    head_lanes = jnp.where(qrow < new, _indicator(lane, 0, d), _indicator(lane, d, 2 * d)).astype(BF16)
    out_lane = lax.broadcasted_iota(jnp.int32, (new, LANES), 1)
    nrow = lax.broadcasted_iota(jnp.int32, (rows, new), 0)
    ncol = lax.broadcasted_iota(jnp.int32, (rows, new), 1)
    causal = ncol <= (nrow & (new - 1))
    for pp in range(pairs):
        lanes = slice(pp * LANES, (pp + 1) * LANES)
        head0 = (group * pairs + pp) * HEADS_PER_LANE_BLOCK
        tpad = c_ref.shape[2]
        bias = jnp.concatenate(
            [jnp.broadcast_to(c_ref[0, pl.ds(head0 + h, 1), :], (new, tpad)) for h in range(HEADS_PER_LANE_BLOCK)],
            axis=0)
        q = q_ref[0, :, lanes]
        q2 = jnp.concatenate([q, q], axis=0) * head_lanes
        s_c = _dot(q2, kc_ref[0, lanes, :].astype(BF16)) - bias[:, 0:past]
        s_n = _dot_nt(q2, kn_ref[0, :, lanes].astype(BF16)) - bias[:, past:past + new]
        s_n = jnp.where(causal, s_n, MASK_VALUE)
        m = jnp.maximum(jnp.max(s_c, axis=-1, keepdims=True), jnp.max(s_n, axis=-1, keepdims=True))
        e_c = jnp.exp(s_c - m)
        e_n = jnp.exp(s_n - m)
        l = jnp.sum(e_c, axis=-1, keepdims=True) + jnp.sum(e_n, axis=-1, keepdims=True)
        o2 = _dot_nt(e_c.astype(BF16), vc_ref[0, lanes, :].astype(BF16))
        o2 = (o2 + _dot(e_n.astype(BF16), vn_ref[0, :, lanes].astype(BF16))) / l
        o = jnp.where(out_lane < d, o2[0:new], o2[new:rows])
        o_ref[0, :, lanes] = o.astype(BF16)


def _fox_sample(q, kn, vn, kct, vct, c):
    b, new, _ = q.shape
    past = kct.shape[2]
    pairs = SAMPLE_PAIRS_PER_STEP
    width = pairs * LANES
    tok = lambda i, g: (i, 0, g)
    feat = lambda i, g: (i, g, 0)
    return pl.pallas_call(
        functools.partial(_fox_sample_kernel, past=past, new=new, pairs=pairs),
        grid=(b, N_HEAD_PAIRS // pairs),
        in_specs=[pl.BlockSpec((1, new, width), tok), pl.BlockSpec((1, new, width), tok),
                  pl.BlockSpec((1, new, width), tok), pl.BlockSpec((1, width, past), feat),
                  pl.BlockSpec((1, width, past), feat),
                  pl.BlockSpec((1,) + c.shape[1:], lambda i, g: (i, 0, 0))],
        out_specs=pl.BlockSpec((1, new, width), tok),
        out_shape=jax.ShapeDtypeStruct((b, new, FOX_WIDTH), BF16),
        compiler_params=_cparams(2), name="fox_sample",
    )(q, kn, vn, kct, vct, c)


def _retention_kernel(lg_ref, q_ref, k_ref, v_ref, g_ref, gn_ref, s0_ref, o_ref, st_ref,
                      decay_ref, qdec_ref, kdec_ref, *, blk):
    lg = lg_ref[0, 0:1, 0:1]

    @pl.when(pl.program_id(2) == 0)
    def _():
        st_ref[0, 0] = s0_ref[0, 0]
        n = lax.broadcasted_iota(jnp.int32, (blk, blk), 0)
        m = lax.broadcasted_iota(jnp.int32, (blk, blk), 1)
        shift = CHUNK.bit_length() - 1
        dist = jnp.abs(n - m).astype(F32)
        decay_ref[...] = jnp.where((m >> shift) <= (n >> shift), jnp.exp(lg * dist), 0.0)
        pos = lax.broadcasted_iota(jnp.int32, (blk, RET_KEY_DIM), 0).astype(F32)
        qdec_ref[...] = jnp.exp(lg * (pos + 1.0))
        kdec_ref[...] = jnp.exp(lg * (blk - 1.0 - pos))

    q = q_ref[0]
    k = k_ref[0]
    v = v_ref[0]
    state = st_ref[0, 0]
    s = _dot_nt(q, k) * decay_ref[...]
    qd = (q.astype(F32) * qdec_ref[...]).astype(BF16)
    y = _dot(s.astype(BF16), v) + _dot(qd, state.astype(BF16))
    kd = k.astype(F32) * kdec_ref[...]
    st_ref[0, 0] = jnp.exp(lg * blk) * state + _dot(kd.T.astype(BF16), v)

    yn = y * lax.rsqrt(jnp.mean(y * y, axis=-1, keepdims=True) + EPS) * gn_ref[0]
    g = g_ref[0]
    o_ref[0] = (g * jax.nn.sigmoid(g) * yn).astype(BF16)


def _retention(lg, q, k, v, g, gn, s0, blk):
    b, t, _ = q.shape
    qk = lambda i, h, j: (i, j, h)
    per_head_state = lambda i, h, j: (i, h, 0, 0)
    return pl.pallas_call(
        functools.partial(_retention_kernel, blk=blk),
        grid=(b, N_RET_HEADS, t // blk),
        in_specs=[pl.BlockSpec((1, 8, LANES), lambda i, h, j: (h, 0, 0)),
                  pl.BlockSpec((1, blk, RET_KEY_DIM), qk), pl.BlockSpec((1, blk, RET_KEY_DIM), qk),
                  pl.BlockSpec((1, blk, RET_VAL_DIM), qk), pl.BlockSpec((1, blk, RET_VAL_DIM), qk),
                  pl.BlockSpec((1, 1, RET_VAL_DIM), lambda i, h, j: (h, 0, 0)),
                  pl.BlockSpec((1, 1, RET_KEY_DIM, RET_VAL_DIM), per_head_state)],
        out_specs=[pl.BlockSpec((1, blk, RET_VAL_DIM), qk),
                   pl.BlockSpec((1, 1, RET_KEY_DIM, RET_VAL_DIM), per_head_state)],
        out_shape=[jax.ShapeDtypeStruct((b, t, RET_V_WIDTH), BF16),
                   jax.ShapeDtypeStruct((b, N_RET_HEADS, RET_KEY_DIM, RET_VAL_DIM), F32)],
        scratch_shapes=[pltpu.VMEM((blk, blk), F32), pltpu.VMEM((blk, RET_KEY_DIM), F32),
                        pltpu.VMEM((blk, RET_KEY_DIM), F32)],
        compiler_params=_cparams(3), name="retention",
    )(lg, q, k, v, g, gn, s0)


def _mixer_kernel(x_ref, oa_ref, ob_ref, g_ref, wg_ref, wa_ref, wb_ref, wo_ref, x1_ref):
    x = x_ref[...]
    h = _rmsnorm(x, g_ref[...]).astype(BF16)
    gate_a = jax.nn.sigmoid(_dot(h, wg_ref[:, 0:D_MODEL]))
    gate_b = jax.nn.sigmoid(_dot(h, wg_ref[:, D_MODEL:2 * D_MODEL]))
    merged = gate_a * _dot(oa_ref[...], wa_ref[...]) + gate_b * _dot(ob_ref[...], wb_ref[...])
    x1_ref[...] = x + _dot(merged.astype(BF16), wo_ref[...])


def _mixer(x, oa, ob, g, wg, wa, wb, wo):
    n = x.shape[0]
    tm = MIX_TOKEN_TILE
    tok = lambda i: (i, 0)
    fixed = lambda i: (0, 0)
    return pl.pallas_call(
        _mixer_kernel,
        grid=(n // tm,),
        in_specs=[pl.BlockSpec((tm, D_MODEL), tok), pl.BlockSpec((tm, FOX_WIDTH), tok),
                  pl.BlockSpec((tm, RET_V_WIDTH), tok), pl.BlockSpec((1, D_MODEL), fixed),
                  pl.BlockSpec((D_MODEL, 2 * D_MODEL), fixed), pl.BlockSpec((FOX_WIDTH, D_MODEL), fixed),
                  pl.BlockSpec((RET_V_WIDTH, D_MODEL), fixed), pl.BlockSpec((D_MODEL, D_MODEL), fixed)],
        out_specs=pl.BlockSpec((tm, D_MODEL), tok),
        out_shape=jax.ShapeDtypeStruct((n, D_MODEL), F32),
        compiler_params=_cparams(1), name="mixer",
    )(x, oa, ob, g, wg, wa, wb, wo)


def _mlp_kernel(x_ref, g_ref, wu_ref, wd_ref, gf_ref, y_ref):
    x = x_ref[...]
    h = _rmsnorm(x, g_ref[...]).astype(BF16)
    acc = x
    for c in range(D_FF // D_MODEL):
        cols = slice(c * D_MODEL, (c + 1) * D_MODEL)
        u = jnp.square(jnp.maximum(_dot(h, wu_ref[:, cols]), 0.0)).astype(BF16)
        acc = acc + _dot(u, wd_ref[cols, :])
    y_ref[...] = _rmsnorm(acc, gf_ref[...])


def _mlp(x, g, wu, wd, gf):
    n = x.shape[0]
    tm = MIX_TOKEN_TILE
    tok = lambda i: (i, 0)
    fixed = lambda i: (0, 0)
    return pl.pallas_call(
        _mlp_kernel,
        grid=(n // tm,),
        in_specs=[pl.BlockSpec((tm, D_MODEL), tok), pl.BlockSpec((1, D_MODEL), fixed),
                  pl.BlockSpec((D_MODEL, D_FF), fixed), pl.BlockSpec((D_FF, D_MODEL), fixed),
                  pl.BlockSpec((1, D_MODEL), fixed)],
        out_specs=pl.BlockSpec((tm, D_MODEL), tok),
        out_shape=jax.ShapeDtypeStruct((n, D_MODEL), F32),
        compiler_params=_cparams(1), name="mlp",
    )(x, g, wu, wd, gf)


def _rope_tables(pos):
    inv_freq = ROPE_BASE ** (-jnp.arange(0, RET_KEY_DIM, 2, dtype=F32) / RET_KEY_DIM)
    ang = pos.astype(F32)[:, None] * inv_freq[None, :]
    return jnp.cos(ang), jnp.sin(ang)


def _after_attention(x, pos, o_a, params, ret_blk, state0):
    b, t, _ = x.shape
    n = b * t
    xf = x.reshape(n, D_MODEL)
    cos, sin = _rope_tables(pos)
    reps = max(1, TOKEN_TILE // t)
    q_r, k_r = _proj_ret_qk(xf, params["g_attn"], params["w_ret_qk"], jnp.tile(cos, (reps, 1)),
                            jnp.tile(sin, (reps, 1)))
    v_r, g_r = _proj_ret_vg(xf, params["g_attn"], params["w_ret_vg"])
    o_b, state = _retention(params["lg"], q_r.reshape(b, t, RET_QK_WIDTH), k_r.reshape(b, t, RET_QK_WIDTH),
                            v_r.reshape(b, t, RET_V_WIDTH), g_r.reshape(b, t, RET_V_WIDTH),
                            params["g_ret_norm"], state0, ret_blk)
    x1 = _mixer(xf, o_a.reshape(n, FOX_WIDTH), o_b.reshape(n, RET_V_WIDTH), params["g_attn"],
                params["w_gates"], params["w_branch_a"], params["w_branch_b"], params["w_out"])
    y = _mlp(x1, params["g_mlp"], params["w_up"], params["w_down"], params["g_final"])
    return y.reshape(b, t, D_MODEL), state[None]


def kernel(x_prompt, x_sample, cache_fox_k, cache_fox_v, cache_fox_logf, state_ret, g_attn, w_in, b_forget,
           g_ret_norm, w_branch, w_out, g_mlp, w_up, w_down, g_final):
    assert w_in.shape[0] == 1, "single-layer trunk"
    wi = w_in[0]
    fox_cols = 3 * FOX_WIDTH + N_FOX_HEADS
    qk_end = fox_cols + 2 * RET_QK_WIDTH
    vg_end = qk_end + 2 * RET_V_WIDTH
    lane_pad = LANES - N_FOX_HEADS
    lg = jnp.log(1.0 - 2.0 ** (-5.0 - jnp.arange(N_RET_HEADS, dtype=F32)))
    w_fox = jnp.pad(wi[:, :fox_cols], ((0, 0), (0, lane_pad))).astype(BF16)
    b_row = jnp.pad(b_forget[0], (0, lane_pad)).reshape(1, LANES)
    params = {
        "g_attn": g_attn[0].reshape(1, D_MODEL),
        "w_ret_qk": wi[:, fox_cols:qk_end].astype(BF16),
        "w_ret_vg": wi[:, qk_end:vg_end].astype(BF16),
        "w_gates": wi[:, vg_end:].astype(BF16),
        "lg": jnp.broadcast_to(lg[:, None, None], (N_RET_HEADS, 8, LANES)),
        "g_ret_norm": g_ret_norm[0].reshape(N_RET_HEADS, 1, RET_VAL_DIM),
        "w_branch_a": w_branch[0, :FOX_WIDTH].astype(BF16),
        "w_branch_b": w_branch[0, FOX_WIDTH:].astype(BF16),
        "w_out": w_out[0].astype(BF16),
        "g_mlp": g_mlp[0].reshape(1, D_MODEL),
        "w_up": w_up[0].astype(BF16),
        "w_down": w_down[0].astype(BF16),
        "g_final": g_final.reshape(1, D_MODEL),
    }
    bp, tp, _ = x_prompt.shape
    bs, ts, _ = x_sample.shape
    past = cache_fox_k.shape[2]

    qt, kt, vt, lft, lf_pad, q_norm, k_norm = _proj_fox_t(
        x_prompt, params["g_attn"], wi[:, :fox_cols].T.astype(BF16), w_fox[:, 3 * FOX_WIDTH:],
        jnp.broadcast_to(b_forget[0][:, None], (N_FOX_HEADS, LANES)), b_row)
    c_pieces, c_end = _cumsum_pieces(lf_pad, CUMSUM_TILE, LOG2E)
    first = _first_key_block(c_end, q_norm, k_norm, tp, FOX_Q_BLOCK, FOX_BLOCK)
    o_a = _fox_prompt(first, qt, kt, vt, c_pieces)
    zero_state = jnp.zeros((bp, N_RET_HEADS, RET_KEY_DIM, RET_VAL_DIM), F32)
    yp, sp = _after_attention(x_prompt, jnp.arange(tp), o_a, params, RET_BLOCK, zero_state)
    to_heads = lambda a: jnp.transpose(a.reshape(bp, N_FOX_HEADS, FOX_HEAD_DIM, tp), (0, 3, 1, 2))[None]
    kp, vp = to_heads(kt), to_heads(vt)
    fp = jnp.transpose(lft, (0, 2, 1))[None]

    q, k, v, lf = _proj_fox(x_sample.reshape(bs * ts, D_MODEL), params["g_attn"], w_fox, b_row)
    feature_major = lambda a: jnp.transpose(a[0], (0, 2, 3, 1)).reshape(bs, FOX_WIDTH, past)
    lf_time = jnp.concatenate([jnp.transpose(cache_fox_logf[0], (0, 2, 1)),
                               jnp.transpose(lf.reshape(bs, ts, N_FOX_HEADS), (0, 2, 1))], axis=2)
    lf_time = jnp.pad(lf_time, ((0, 0), (0, 0), (0, -(past + ts) % LANES)))
    o_a = _fox_sample(q.reshape(bs, ts, FOX_WIDTH), k.reshape(bs, ts, FOX_WIDTH), v.reshape(bs, ts, FOX_WIDTH),
                      feature_major(cache_fox_k), feature_major(cache_fox_v), _cumsum_lanes(lf_time))
    ys, ss = _after_attention(x_sample, past + jnp.arange(ts), o_a, params, CHUNK, state_ret[0])
    shape5 = (1, bs, ts, N_FOX_HEADS, FOX_HEAD_DIM)
    return (yp, ys, kp, vp, fp, sp, k.reshape(shape5), v.reshape(shape5),
            lf.reshape(1, bs, ts, N_FOX_HEADS), ss)
```

```python
import functools

import jax
import jax.numpy as jnp
from jax import lax
from jax.experimental import pallas as pl
from jax.experimental.pallas import tpu as pltpu

D_MODEL = 1024
N_FOX_HEADS = 16
FOX_HEAD_DIM = 64
FOX_WIDTH = N_FOX_HEADS * FOX_HEAD_DIM
N_RET_HEADS = 4
RET_KEY_DIM = 256
RET_VAL_DIM = 512
RET_QK_WIDTH = N_RET_HEADS * RET_KEY_DIM
RET_V_WIDTH = N_RET_HEADS * RET_VAL_DIM
D_FF = 4 * D_MODEL
CHUNK = 64
ROPE_BASE = 10000.0
EPS = 1e-6

LANES = 128
HEADS_PER_LANE_BLOCK = LANES // FOX_HEAD_DIM
N_HEAD_PAIRS = N_FOX_HEADS // HEADS_PER_LANE_BLOCK
N_C_PIECES = 3
ONES_ROWS = 16
MASK_VALUE = -1e30
LOG2E = 1.4426950408889634
VMEM_LIMIT = 56 * 1024 * 1024

TOKEN_TILE = 512
MIX_TOKEN_TILE = 256
FOX_BLOCK = 256
FOX_Q_BLOCK = 512
SAMPLE_PAIRS_PER_STEP = 4
RET_BLOCK = 256
RET_HEADS_PER_STEP = 4
CUMSUM_TILE = 512
SKIP_EXPONENT = 160.0
NORM_SLACK = 1.01

BF16 = jnp.bfloat16
F32 = jnp.float32


def _cparams(n_axes):
    return pltpu.CompilerParams(dimension_semantics=("arbitrary",) * n_axes,
                                vmem_limit_bytes=VMEM_LIMIT)


def _dot(a, b):
    return jnp.dot(a, b, preferred_element_type=F32)


def _dot_nt(a, b):
    return lax.dot_general(a, b, (((1,), (1,)), ((), ())), preferred_element_type=F32)


def _rmsnorm(x, g):
    return x * lax.rsqrt(jnp.mean(x * x, axis=-1, keepdims=True) + EPS) * g


def _log_sigmoid(z):
    return -(jnp.maximum(-z, 0.0) + jnp.log(1.0 + jnp.exp(-jnp.abs(z))))


def _split3(x):
    a = x.astype(BF16)
    r = x - a.astype(F32)
    b = r.astype(BF16)
    c = (r - b.astype(F32)).astype(BF16)
    return a, b, c


def _indicator(idx, lo, hi):
    return jnp.where(idx >= lo, jnp.where(idx < hi, 1.0, 0.0), 0.0)


def _proj_fox_t_kernel(x_ref, g_ref, wt_ref, wf_ref, bcol_ref, brow_ref, qt_ref, kt_ref, vt_ref, lft_ref, lfp_ref,
                       qn_ref, kn_ref):
    h = _rmsnorm(x_ref[0], g_ref[...])
    ht = h.T.astype(BF16)
    w = FOX_WIDTH
    qs = _dot(wt_ref[0:w, :], ht) * (FOX_HEAD_DIM ** -0.5 * LOG2E)
    kf = _dot(wt_ref[w:2 * w, :], ht)
    qt_ref[0] = qs.astype(BF16)
    kt_ref[0] = kf
    for src, dst in ((qs, qn_ref), (kf, kn_ref)):
        sq = jnp.sum((src * src).reshape(N_FOX_HEADS, FOX_HEAD_DIM, src.shape[1]), axis=1)
        dst[0, 0] = jnp.broadcast_to(jnp.sqrt(jnp.max(sq, axis=1, keepdims=True)), (N_FOX_HEADS, LANES))
    vt_ref[0] = _dot(wt_ref[2 * w:3 * w, :], ht)
    lft_ref[0] = _log_sigmoid(_dot(wt_ref[3 * w:3 * w + N_FOX_HEADS, :], ht) + bcol_ref[:, 0:1])
    lfp_ref[0] = _log_sigmoid(_dot(h.astype(BF16), wf_ref[...]) + brow_ref[...])


def _proj_fox_t(x, g, wt, wf, bcol, brow):
    b, t, _ = x.shape
    tm = TOKEN_TILE
    fixed = lambda i, j: (0, 0)
    feat = lambda i, j: (i, 0, j)
    return pl.pallas_call(
        _proj_fox_t_kernel,
        grid=(b, t // tm),
        in_specs=[pl.BlockSpec((1, tm, D_MODEL), lambda i, j: (i, j, 0)), pl.BlockSpec((1, D_MODEL), fixed),
                  pl.BlockSpec(wt.shape, fixed), pl.BlockSpec(wf.shape, fixed),
                  pl.BlockSpec(bcol.shape, fixed), pl.BlockSpec(brow.shape, fixed)],
        out_specs=[pl.BlockSpec((1, FOX_WIDTH, tm), feat), pl.BlockSpec((1, FOX_WIDTH, tm), feat),
                   pl.BlockSpec((1, FOX_WIDTH, tm), feat), pl.BlockSpec((1, N_FOX_HEADS, tm), feat),
                   pl.BlockSpec((1, tm, LANES), lambda i, j: (i, j, 0)),
                   pl.BlockSpec((1, 1, N_FOX_HEADS, LANES), lambda i, j: (i, j, 0, 0)),
                   pl.BlockSpec((1, 1, N_FOX_HEADS, LANES), lambda i, j: (i, j, 0, 0))],
        out_shape=[jax.ShapeDtypeStruct((b, FOX_WIDTH, t), BF16), jax.ShapeDtypeStruct((b, FOX_WIDTH, t), F32),
                   jax.ShapeDtypeStruct((b, FOX_WIDTH, t), F32), jax.ShapeDtypeStruct((b, N_FOX_HEADS, t), F32),
                   jax.ShapeDtypeStruct((b, t, LANES), F32),
                   jax.ShapeDtypeStruct((b, t // tm, N_FOX_HEADS, LANES), F32),
                   jax.ShapeDtypeStruct((b, t // tm, N_FOX_HEADS, LANES), F32)],
        compiler_params=_cparams(2), name="proj_fox_t",
    )(x, g, wt, wf, bcol, brow)


def _proj_fox_kernel(x_ref, g_ref, w_ref, b_ref, q_ref, k_ref, v_ref, lf_ref):
    h = _rmsnorm(x_ref[...], g_ref[...]).astype(BF16)
    w = FOX_WIDTH
    q_ref[...] = (_dot(h, w_ref[:, 0:w]) * (FOX_HEAD_DIM ** -0.5)).astype(BF16)
    k_ref[...] = _dot(h, w_ref[:, w:2 * w])
    v_ref[...] = _dot(h, w_ref[:, 2 * w:3 * w])
    lf = _log_sigmoid(_dot(h, w_ref[:, 3 * w:3 * w + LANES]) + b_ref[...])
    lf_ref[...] = lf[:, 0:N_FOX_HEADS]


def _proj_fox(x, g, w, b):
    n = x.shape[0]
    tm = TOKEN_TILE
    tok = lambda i: (i, 0)
    fixed = lambda i: (0, 0)
    return pl.pallas_call(
        _proj_fox_kernel,
        grid=(n // tm,),
        in_specs=[pl.BlockSpec((tm, D_MODEL), tok), pl.BlockSpec((1, D_MODEL), fixed),
                  pl.BlockSpec(w.shape, fixed), pl.BlockSpec((1, LANES), fixed)],
        out_specs=[pl.BlockSpec((tm, FOX_WIDTH), tok), pl.BlockSpec((tm, FOX_WIDTH), tok),
                   pl.BlockSpec((tm, FOX_WIDTH), tok), pl.BlockSpec((tm, N_FOX_HEADS), tok)],
        out_shape=[jax.ShapeDtypeStruct((n, FOX_WIDTH), BF16), jax.ShapeDtypeStruct((n, FOX_WIDTH), F32),
                   jax.ShapeDtypeStruct((n, FOX_WIDTH), F32), jax.ShapeDtypeStruct((n, N_FOX_HEADS), F32)],
        compiler_params=_cparams(1), name="proj_fox",
    )(x, g, w, b)


def _proj_ret_qk_kernel(x_ref, g_ref, w_ref, cos_ref, sin_ref, q_ref, k_ref):
    h = _rmsnorm(x_ref[...], g_ref[...]).astype(BF16)
    cos = cos_ref[...]
    sin = sin_ref[...]
    half = RET_KEY_DIM // 2
    for out_ref, base, scale in ((q_ref, 0, 1.0), (k_ref, RET_QK_WIDTH, RET_KEY_DIM ** -0.5)):
        for hd in range(N_RET_HEADS):
            lo = hd * RET_KEY_DIM
            z = _dot(h, w_ref[:, base + lo:base + lo + RET_KEY_DIM])
            x1 = z[:, :half]
            x2 = z[:, half:]
            out_ref[:, lo:lo + half] = ((x1 * cos - x2 * sin) * scale).astype(BF16)
            out_ref[:, lo + half:lo + RET_KEY_DIM] = ((x1 * sin + x2 * cos) * scale).astype(BF16)


def _proj_ret_qk(x, g, w, cos, sin):
    n = x.shape[0]
    tm = TOKEN_TILE
    period = cos.shape[0] // tm
    tok = lambda i: (i, 0)
    fixed = lambda i: (0, 0)
    pos = lambda i: (i % period, 0)
    half = RET_KEY_DIM // 2
    return pl.pallas_call(
        _proj_ret_qk_kernel,
        grid=(n // tm,),
        in_specs=[pl.BlockSpec((tm, D_MODEL), tok), pl.BlockSpec((1, D_MODEL), fixed),
                  pl.BlockSpec((D_MODEL, 2 * RET_QK_WIDTH), fixed),
                  pl.BlockSpec((tm, half), pos), pl.BlockSpec((tm, half), pos)],
        out_specs=[pl.BlockSpec((tm, RET_QK_WIDTH), tok), pl.BlockSpec((tm, RET_QK_WIDTH), tok)],
        out_shape=[jax.ShapeDtypeStruct((n, RET_QK_WIDTH), BF16)] * 2,
        compiler_params=_cparams(1), name="proj_ret_qk",
    )(x, g, w, cos, sin)


def _proj_ret_vg_kernel(x_ref, g_ref, w_ref, v_ref, gr_ref):
    h = _rmsnorm(x_ref[...], g_ref[...]).astype(BF16)
    v_ref[...] = _dot(h, w_ref[:, 0:RET_V_WIDTH]).astype(BF16)
    gr_ref[...] = _dot(h, w_ref[:, RET_V_WIDTH:2 * RET_V_WIDTH])


def _proj_ret_vg(x, g, w):
    n = x.shape[0]
    tm = TOKEN_TILE
    tok = lambda i: (i, 0)
    fixed = lambda i: (0, 0)
    return pl.pallas_call(
        _proj_ret_vg_kernel,
        grid=(n // tm,),
        in_specs=[pl.BlockSpec((tm, D_MODEL), tok), pl.BlockSpec((1, D_MODEL), fixed),
                  pl.BlockSpec((D_MODEL, 2 * RET_V_WIDTH), fixed)],
        out_specs=[pl.BlockSpec((tm, RET_V_WIDTH), tok), pl.BlockSpec((tm, RET_V_WIDTH), tok)],
        out_shape=[jax.ShapeDtypeStruct((n, RET_V_WIDTH), BF16), jax.ShapeDtypeStruct((n, RET_V_WIDTH), F32)],
        compiler_params=_cparams(1), name="proj_ret_vg",
    )(x, g, w)


def _cumsum_kernel(lf_ref, c_ref, cend_ref, carry_ref, *, tb, scale):
    @pl.when(pl.program_id(1) == 0)
    def _():
        carry_ref[...] = jnp.zeros_like(carry_ref)

    row = lax.broadcasted_iota(jnp.int32, (tb, tb), 0)
    col = lax.broadcasted_iota(jnp.int32, (tb, tb), 1)
    tri = jnp.where(col <= row, 1.0, 0.0).astype(BF16)
    c = carry_ref[0:1, :]
    for piece in _split3(lf_ref[0]):
        c = c + _dot(tri, piece)
    carry_ref[...] = jnp.broadcast_to(c[tb - 1:tb, :], carry_ref.shape)
    erow = lax.broadcasted_iota(jnp.int32, (8, LANES), 0)
    ends = jnp.zeros((8, LANES), F32)
    for e in range(tb // FOX_BLOCK):
        ends = jnp.where(erow == e, c[(e + 1) * FOX_BLOCK - 1:(e + 1) * FOX_BLOCK, :] * scale, ends)
    cend_ref[0, 0] = ends

    prow = lax.broadcasted_iota(jnp.int32, (LANES, LANES), 0)
    pcol = lax.broadcasted_iota(jnp.int32, (LANES, LANES), 1)
    out = jnp.zeros((tb, LANES), F32)
    for p, piece in enumerate(_split3(-c * scale)):
        place = jnp.where(pcol == prow + p * N_FOX_HEADS, jnp.where(prow < N_FOX_HEADS, 1.0, 0.0), 0.0)
        out = out + _dot(piece, place.astype(BF16))
    c_ref[0] = out.astype(BF16)


def _cumsum_pieces(lf, tb, scale):
    b, t, _ = lf.shape
    blk = lambda i, j: (i, j, 0)
    return pl.pallas_call(
        functools.partial(_cumsum_kernel, tb=tb, scale=scale),
        grid=(b, t // tb),
        in_specs=[pl.BlockSpec((1, tb, LANES), blk)],
        out_specs=[pl.BlockSpec((1, tb, LANES), blk), pl.BlockSpec((1, 1, 8, LANES), lambda i, j: (i, j, 0, 0))],
        out_shape=[jax.ShapeDtypeStruct((b, t, LANES), BF16), jax.ShapeDtypeStruct((b, t // tb, 8, LANES), F32)],
        scratch_shapes=[pltpu.VMEM((8, LANES), F32)],
        compiler_params=_cparams(2), name="cumsum",
    )(lf)


def _cumsum_lanes_kernel(lf_ref, c_ref):
    heads, t = lf_ref.shape[1], lf_ref.shape[2]
    row = lax.broadcasted_iota(jnp.int32, (LANES, LANES), 0)
    col = lax.broadcasted_iota(jnp.int32, (LANES, LANES), 1)
    tri = jnp.where(row <= col, 1.0, 0.0).astype(BF16)
    ones = jnp.ones((LANES, LANES), BF16)
    carry = jnp.zeros((heads, LANES), F32)
    for seg in range(t // LANES):
        lanes = slice(seg * LANES, (seg + 1) * LANES)
        stack = jnp.concatenate(_split3(lf_ref[0, :, lanes]), axis=0)
        within = _dot(stack, tri)
        total = _dot(stack, ones)
        fold = lambda a: a[0:heads] + a[heads:2 * heads] + a[2 * heads:3 * heads]
        c_ref[0, :, lanes] = fold(within) + carry
        carry = carry + fold(total)


def _cumsum_lanes(lf):
    b, heads, t = lf.shape
    blk = lambda i: (i, 0, 0)
    return pl.pallas_call(
        _cumsum_lanes_kernel,
        grid=(b,),
        in_specs=[pl.BlockSpec((1, heads, t), blk)],
        out_specs=pl.BlockSpec((1, heads, t), blk),
        out_shape=jax.ShapeDtypeStruct((b, heads, t), F32),
        compiler_params=_cparams(1), name="cumsum_lanes",
    )(lf)


def _piece_selector(shape, axis, head):
    idx = lax.broadcasted_iota(jnp.int32, shape, axis)
    hit = jnp.where(idx < N_C_PIECES * N_FOX_HEADS, jnp.where((idx & (N_FOX_HEADS - 1)) == head, 1.0, 0.0), 0.0)
    return hit


_ROW_M, _ROW_MX, _ROW_ALPHA = 0, 1, 2


def _fox_prompt_kernel(first_ref, qt_ref, kt_ref, vt_ref, c_ref, o_ref, kaug_ref, vta_ref, qa_ref, s_ref, p_ref, acc_ref,
                       stat_ref, *, t, tq, blk):
    pair = pl.program_id(1)
    qi = pl.program_id(2)
    nblk = t // blk
    diag_blocks = tq // blk
    d = FOX_HEAD_DIM
    heads = range(HEADS_PER_LANE_BLOCK)

    @pl.when(qi == 0)
    def _build():
        ones = jnp.ones((ONES_ROWS, blk), F32)

        def body(j, carry):
            off = pl.multiple_of(j * blk, blk)
            kaug_ref[pl.ds(off, blk), 0:LANES] = kt_ref[0, :, pl.ds(off, blk)].T.astype(BF16)
            kaug_ref[pl.ds(off, blk), LANES:2 * LANES] = c_ref[0, pl.ds(off, blk), :]
            for h in heads:
                vh = vt_ref[0, h * d:(h + 1) * d, pl.ds(off, blk)]
                vta_ref[j, h] = jnp.concatenate([vh, ones], axis=0).astype(BF16)
            return carry

        lax.fori_loop(0, nblk, body, 0)

    qt = qt_ref[0]
    row = lax.broadcasted_iota(jnp.int32, (LANES, tq), 0)
    for h in heads:
        in_head = _indicator(row, h * d, (h + 1) * d).astype(BF16)
        sel = _piece_selector((LANES, tq), 0, pair * HEADS_PER_LANE_BLOCK + h).astype(BF16)
        qa_ref[h] = jnp.concatenate([qt * in_head, sel], axis=0)

    acc_ref[...] = jnp.zeros_like(acc_ref)
    stat_ref[...] = jnp.full(stat_ref.shape, MASK_VALUE, F32)
    krow = lax.broadcasted_iota(jnp.int32, (blk, tq), 0)
    qcol = lax.broadcasted_iota(jnp.int32, (blk, tq), 1)

    def stat(h, slot, kind):
        r = (h * 2 + slot) * 4 + kind
        return slice(r, r + 1)

    def stage_a(j, slot, diag):
        off = pl.multiple_of(j * blk, blk)
        kb = kaug_ref[pl.ds(off, blk), :]
        for h in heads:
            s = _dot(kb, qa_ref[h])
            if diag is not None:
                s = jnp.where(krow + diag * blk <= qcol, s, MASK_VALUE)
            s_ref[slot, h] = s
            stat_ref[stat(h, slot, _ROW_MX), :] = jnp.max(s, axis=0, keepdims=True)

    def stage_b(slot):
        for h in heads:
            m_old = stat_ref[stat(h, 0, _ROW_M), :]
            m_new = jnp.maximum(m_old, stat_ref[stat(h, slot, _ROW_MX), :])
            stat_ref[stat(h, slot, _ROW_ALPHA), :] = jnp.exp2(m_old - m_new)
            stat_ref[stat(h, 0, _ROW_M), :] = m_new
            p_ref[slot, h] = jnp.exp2(s_ref[slot, h] - m_new).astype(BF16)

    def stage_c(j, slot):
        for h in heads:
            alpha = stat_ref[stat(h, slot, _ROW_ALPHA), :]
            acc_ref[h] = acc_ref[h] * alpha + _dot(vta_ref[j, h], p_ref[slot, h])

    nfull = qi * diag_blocks
    j0 = first_ref[(pl.program_id(0) * N_HEAD_PAIRS + pair) * (t // tq) + qi]

    @pl.when(qi > 0)
    def _pipelined():
        stage_a(j0, 0, None)
        stage_b(0)
        stage_a(j0 + 1, 1, None)

        def two_steps(j):
            stage_c(j - 2, 0)
            stage_b(1)
            stage_a(j, 0, None)
            stage_c(j - 1, 1)
            stage_b(0)
            stage_a(j + 1, 1, None)

        def body(u, carry):
            two_steps(j0 + 2 + 4 * u)
            two_steps(j0 + 4 + 4 * u)
            return carry

        pairs_left = (nfull - j0) // 2 - 1
        lax.fori_loop(0, pairs_left // 2, body, 0)

        @pl.when(pairs_left % 2 == 1)
        def _odd_group():
            two_steps(j0 + 2 * pairs_left)
        for dg in range(diag_blocks):
            stage_c(nfull + dg - 2, dg % 2)
            stage_b((dg + 1) % 2)
            stage_a(nfull + dg, dg % 2, dg)
        stage_c(nfull + diag_blocks - 2, 0)
        stage_b(1)
        stage_c(nfull + diag_blocks - 1, 1)

    @pl.when(qi == 0)
    def _first_block():
        for dg in range(diag_blocks):
            stage_a(dg, dg % 2, dg)
            stage_b(dg % 2)
            stage_c(dg, dg % 2)

    outs = []
    for h in heads:
        a = acc_ref[h]
        outs.append(a[0:d] / a[d:d + 1])
    o_ref[0] = jnp.concatenate(outs, axis=0).T.astype(BF16)


def _first_key_block(c_end, q_norm, k_norm, t, tq, blk):
    b = c_end.shape[0]
    nblk, nq, per = t // blk, t // tq, tq // blk
    ce = c_end[:, :, :CUMSUM_TILE // blk, :N_FOX_HEADS].reshape(b, nblk, N_FOX_HEADS)
    qn = q_norm[:, :, :, 0]
    kmax = jnp.max(k_norm[:, :, :, 0], axis=1)
    before = jnp.concatenate([jnp.zeros((b, 1, N_FOX_HEADS), F32), ce[:, per - 1:-1:per]], axis=1)
    decay = before[:, :, None, :] - ce[:, None, :, :]
    bound = (2.0 * NORM_SLACK * NORM_SLACK) * (qn * kmax[:, None, :])[:, :, None, :] + decay
    below = (jnp.arange(nblk)[None, :] < (jnp.arange(nq) * per)[:, None])[None, :, :, None]
    skippable = jnp.logical_and(bound <= -SKIP_EXPONENT, below)
    block = jnp.arange(nblk, dtype=jnp.int32)[None, None, :, None]
    n_skip = jnp.min(jnp.where(skippable, nblk, block), axis=2)
    n_skip = jnp.min(n_skip.reshape(b, nq, N_HEAD_PAIRS, HEADS_PER_LANE_BLOCK), axis=3)
    first = jnp.clip((n_skip // 2) * 2, 0, jnp.maximum(jnp.arange(nq) * per - 2, 0)[None, :, None])
    return jnp.transpose(first, (0, 2, 1)).reshape(-1).astype(jnp.int32)


def _fox_prompt(first, qt, kt, vt, c):
    b, _, t = qt.shape
    blk = FOX_BLOCK
    tq = FOX_Q_BLOCK
    assert tq == 2 * blk, "the pipeline keeps two key blocks in flight and slot = block parity"
    nblk = t // blk
    hp = HEADS_PER_LANE_BLOCK
    acc_rows = FOX_HEAD_DIM + ONES_ROWS
    per_pair_all_time = lambda i, p, j, first: (i, p, 0)
    grid_spec = pltpu.PrefetchScalarGridSpec(
        num_scalar_prefetch=1,
        grid=(b, N_HEAD_PAIRS, t // tq),
        in_specs=[pl.BlockSpec((1, LANES, tq), lambda i, p, j, first: (i, p, j)),
                  pl.BlockSpec((1, LANES, t), per_pair_all_time),
                  pl.BlockSpec((1, LANES, t), per_pair_all_time),
                  pl.BlockSpec((1, t, LANES), lambda i, p, j, first: (i, 0, 0))],
        out_specs=pl.BlockSpec((1, tq, LANES), lambda i, p, j, first: (i, j, p)),
        scratch_shapes=[pltpu.VMEM((t, 2 * LANES), BF16),
                        pltpu.VMEM((nblk, hp, acc_rows, blk), BF16),
                        pltpu.VMEM((hp, 2 * LANES, tq), BF16),
                        pltpu.VMEM((2, hp, blk, tq), F32),
                        pltpu.VMEM((2, hp, blk, tq), BF16),
                        pltpu.VMEM((hp, acc_rows, tq), F32),
                        pltpu.VMEM((hp * 2 * 4, tq), F32)])
    return pl.pallas_call(
        functools.partial(_fox_prompt_kernel, t=t, tq=tq, blk=blk),
        grid_spec=grid_spec,
        out_shape=jax.ShapeDtypeStruct((b, t, FOX_WIDTH), BF16),
        compiler_params=_cparams(3), name="fox_prompt",
    )(first, qt, kt, vt, c)


def _fox_sample_kernel(q_ref, kn_ref, vn_ref, kc_ref, vc_ref, c_ref, o_ref, *, past, new, pairs):
    group = pl.program_id(1)
    d = FOX_HEAD_DIM
    rows = HEADS_PER_LANE_BLOCK * new
    lane = lax.broadcasted_iota(jnp.int32, (rows, LANES), 1)
    qrow = lax.broadcasted_iota(jnp.int32, (rows, LANES), 0)
    head_lanes = jnp.where(qrow < new, _indicator(lane, 0, d), _indicator(lane, d, 2 * d)).astype(BF16)
    out_lane = lax.broadcasted_iota(jnp.int32, (new, LANES), 1)
    nrow = lax.broadcasted_iota(jnp.int32, (rows, new), 0)
    ncol = lax.broadcasted_iota(jnp.int32, (rows, new), 1)
    causal = ncol <= (nrow & (new - 1))
    for pp in range(pairs):
        lanes = slice(pp * LANES, (pp + 1) * LANES)
        head0 = (group * pairs + pp) * HEADS_PER_LANE_BLOCK
        tpad = c_ref.shape[2]
        bias = jnp.concatenate(
            [jnp.broadcast_to(c_ref[0, pl.ds(head0 + h, 1), :], (new, tpad)) for h in range(HEADS_PER_LANE_BLOCK)],
            axis=0)
        q = q_ref[0, :, lanes]
        q2 = jnp.concatenate([q, q], axis=0) * head_lanes
        s_c = _dot(q2, kc_ref[0, lanes, :].astype(BF16)) - bias[:, 0:past]
        s_n = _dot_nt(q2, kn_ref[0, :, lanes].astype(BF16)) - bias[:, past:past + new]
        s_n = jnp.where(causal, s_n, MASK_VALUE)
        m = jnp.maximum(jnp.max(s_c, axis=-1, keepdims=True), jnp.max(s_n, axis=-1, keepdims=True))
        e_c = jnp.exp(s_c - m)
        e_n = jnp.exp(s_n - m)
        l = jnp.sum(e_c, axis=-1, keepdims=True) + jnp.sum(e_n, axis=-1, keepdims=True)
        o2 = _dot_nt(e_c.astype(BF16), vc_ref[0, lanes, :].astype(BF16))
        o2 = (o2 + _dot(e_n.astype(BF16), vn_ref[0, :, lanes].astype(BF16))) / l
        o = jnp.where(out_lane < d, o2[0:new], o2[new:rows])
        o_ref[0, :, lanes] = o.astype(BF16)


def _fox_sample(q, kn, vn, kct, vct, c):
    b, new, _ = q.shape
    past = kct.shape[2]
    pairs = SAMPLE_PAIRS_PER_STEP
    width = pairs * LANES
    tok = lambda i, g: (i, 0, g)
    feat = lambda i, g: (i, g, 0)
    return pl.pallas_call(
        functools.partial(_fox_sample_kernel, past=past, new=new, pairs=pairs),
        grid=(b, N_HEAD_PAIRS // pairs),
        in_specs=[pl.BlockSpec((1, new, width), tok), pl.BlockSpec((1, new, width), tok),
                  pl.BlockSpec((1, new, width), tok), pl.BlockSpec((1, width, past), feat),
                  pl.BlockSpec((1, width, past), feat),
                  pl.BlockSpec((1,) + c.shape[1:], lambda i, g: (i, 0, 0))],
        out_specs=pl.BlockSpec((1, new, width), tok),
        out_shape=jax.ShapeDtypeStruct((b, new, FOX_WIDTH), BF16),
        compiler_params=_cparams(2), name="fox_sample",
    )(q, kn, vn, kct, vct, c)


def _retention_kernel(lg_ref, q_ref, k_ref, v_ref, g_ref, gn_ref, s0_ref, o_ref, st_ref,
                      decay_ref, qdec_ref, kdec_ref, *, blk, heads):
    @pl.when(pl.program_id(2) == 0)
    def _():
        n = lax.broadcasted_iota(jnp.int32, (blk, blk), 0)
        m = lax.broadcasted_iota(jnp.int32, (blk, blk), 1)
        shift = CHUNK.bit_length() - 1
        dist = jnp.abs(n - m).astype(F32)
        same_or_earlier_chunk = (m >> shift) <= (n >> shift)
        pos = lax.broadcasted_iota(jnp.int32, (blk, RET_KEY_DIM), 0).astype(F32)
        for h in range(heads):
            lg = lg_ref[h, 0:1, 0:1]
            st_ref[0, h] = s0_ref[0, h]
            decay_ref[h] = jnp.where(same_or_earlier_chunk, jnp.exp(lg * dist), 0.0)
            qdec_ref[h] = jnp.exp(lg * (pos + 1.0))
            kdec_ref[h] = jnp.exp(lg * (blk - 1.0 - pos))

    for h in range(heads):
        lg = lg_ref[h, 0:1, 0:1]
        keys = slice(h * RET_KEY_DIM, (h + 1) * RET_KEY_DIM)
        vals = slice(h * RET_VAL_DIM, (h + 1) * RET_VAL_DIM)
        q = q_ref[0, :, keys]
        k = k_ref[0, :, keys]
        v = v_ref[0, :, vals]
        state = st_ref[0, h]
        s = _dot_nt(q, k) * decay_ref[h]
        qd = (q.astype(F32) * qdec_ref[h]).astype(BF16)
        y = _dot(s.astype(BF16), v) + _dot(qd, state.astype(BF16))
        kd = k.astype(F32) * kdec_ref[h]
        st_ref[0, h] = jnp.exp(lg * blk) * state + _dot(kd.T.astype(BF16), v)

        yn = y * lax.rsqrt(jnp.mean(y * y, axis=-1, keepdims=True) + EPS) * gn_ref[h]
        g = g_ref[0, :, vals]
        o_ref[0, :, vals] = (g * jax.nn.sigmoid(g) * yn).astype(BF16)


def _retention(lg, q, k, v, g, gn, s0, blk):
    b, t, _ = q.shape
    heads = RET_HEADS_PER_STEP
    qk = lambda i, h, j: (i, j, h)
    per_group = lambda i, h, j: (h, 0, 0)
    group_state = lambda i, h, j: (i, h, 0, 0)
    return pl.pallas_call(
        functools.partial(_retention_kernel, blk=blk, heads=heads),
        grid=(b, N_RET_HEADS // heads, t // blk),
        in_specs=[pl.BlockSpec((heads, 8, LANES), per_group),
                  pl.BlockSpec((1, blk, heads * RET_KEY_DIM), qk), pl.BlockSpec((1, blk, heads * RET_KEY_DIM), qk),
                  pl.BlockSpec((1, blk, heads * RET_VAL_DIM), qk), pl.BlockSpec((1, blk, heads * RET_VAL_DIM), qk),
                  pl.BlockSpec((heads, 1, RET_VAL_DIM), per_group),
                  pl.BlockSpec((1, heads, RET_KEY_DIM, RET_VAL_DIM), group_state)],
        out_specs=[pl.BlockSpec((1, blk, heads * RET_VAL_DIM), qk),
                   pl.BlockSpec((1, heads, RET_KEY_DIM, RET_VAL_DIM), group_state)],
        out_shape=[jax.ShapeDtypeStruct((b, t, RET_V_WIDTH), BF16),
                   jax.ShapeDtypeStruct((b, N_RET_HEADS, RET_KEY_DIM, RET_VAL_DIM), F32)],
        scratch_shapes=[pltpu.VMEM((heads, blk, blk), F32), pltpu.VMEM((heads, blk, RET_KEY_DIM), F32),
                        pltpu.VMEM((heads, blk, RET_KEY_DIM), F32)],
        compiler_params=_cparams(3), name="retention",
    )(lg, q, k, v, g, gn, s0)


def _mixer_kernel(x_ref, oa_ref, ob_ref, g_ref, wg_ref, wa_ref, wb_ref, wo_ref, x1_ref):
    x = x_ref[...]
    h = _rmsnorm(x, g_ref[...]).astype(BF16)
    gate_a = jax.nn.sigmoid(_dot(h, wg_ref[:, 0:D_MODEL]))
    gate_b = jax.nn.sigmoid(_dot(h, wg_ref[:, D_MODEL:2 * D_MODEL]))
    merged = gate_a * _dot(oa_ref[...], wa_ref[...]) + gate_b * _dot(ob_ref[...], wb_ref[...])
    x1_ref[...] = x + _dot(merged.astype(BF16), wo_ref[...])


def _mixer(x, oa, ob, g, wg, wa, wb, wo):
    n = x.shape[0]
    tm = MIX_TOKEN_TILE
    tok = lambda i: (i, 0)
    fixed = lambda i: (0, 0)
    return pl.pallas_call(
        _mixer_kernel,
        grid=(n // tm,),
        in_specs=[pl.BlockSpec((tm, D_MODEL), tok), pl.BlockSpec((tm, FOX_WIDTH), tok),
                  pl.BlockSpec((tm, RET_V_WIDTH), tok), pl.BlockSpec((1, D_MODEL), fixed),
                  pl.BlockSpec((D_MODEL, 2 * D_MODEL), fixed), pl.BlockSpec((FOX_WIDTH, D_MODEL), fixed),
                  pl.BlockSpec((RET_V_WIDTH, D_MODEL), fixed), pl.BlockSpec((D_MODEL, D_MODEL), fixed)],
        out_specs=pl.BlockSpec((tm, D_MODEL), tok),
        out_shape=jax.ShapeDtypeStruct((n, D_MODEL), F32),
        compiler_params=_cparams(1), name="mixer",
    )(x, oa, ob, g, wg, wa, wb, wo)


def _mlp_kernel(x_ref, g_ref, wu_ref, wd_ref, gf_ref, y_ref):
    x = x_ref[...]
    h = _rmsnorm(x, g_ref[...]).astype(BF16)
    acc = x
    for c in range(D_FF // D_MODEL):
        cols = slice(c * D_MODEL, (c + 1) * D_MODEL)
        u = jnp.square(jnp.maximum(_dot(h, wu_ref[:, cols]), 0.0)).astype(BF16)
        acc = acc + _dot(u, wd_ref[cols, :])
    y_ref[...] = _rmsnorm(acc, gf_ref[...])


def _mlp(x, g, wu, wd, gf):
    n = x.shape[0]
    tm = MIX_TOKEN_TILE
    tok = lambda i: (i, 0)
    fixed = lambda i: (0, 0)
    return pl.pallas_call(
        _mlp_kernel,
        grid=(n // tm,),
        in_specs=[pl.BlockSpec((tm, D_MODEL), tok), pl.BlockSpec((1, D_MODEL), fixed),
                  pl.BlockSpec((D_MODEL, D_FF), fixed), pl.BlockSpec((D_FF, D_MODEL), fixed),
                  pl.BlockSpec((1, D_MODEL), fixed)],
        out_specs=pl.BlockSpec((tm, D_MODEL), tok),
        out_shape=jax.ShapeDtypeStruct((n, D_MODEL), F32),
        compiler_params=_cparams(1), name="mlp",
    )(x, g, wu, wd, gf)


def _rope_tables(pos):
    inv_freq = ROPE_BASE ** (-jnp.arange(0, RET_KEY_DIM, 2, dtype=F32) / RET_KEY_DIM)
    ang = pos.astype(F32)[:, None] * inv_freq[None, :]
    return jnp.cos(ang), jnp.sin(ang)


def _after_attention(x, pos, o_a, params, ret_blk, state0):
    b, t, _ = x.shape
    n = b * t
    xf = x.reshape(n, D_MODEL)
    cos, sin = _rope_tables(pos)
    reps = max(1, TOKEN_TILE // t)
    q_r, k_r = _proj_ret_qk(xf, params["g_attn"], params["w_ret_qk"], jnp.tile(cos, (reps, 1)),
                            jnp.tile(sin, (reps, 1)))
    v_r, g_r = _proj_ret_vg(xf, params["g_attn"], params["w_ret_vg"])
    o_b, state = _retention(params["lg"], q_r.reshape(b, t, RET_QK_WIDTH), k_r.reshape(b, t, RET_QK_WIDTH),
                            v_r.reshape(b, t, RET_V_WIDTH), g_r.reshape(b, t, RET_V_WIDTH),
                            params["g_ret_norm"], state0, ret_blk)
    x1 = _mixer(xf, o_a.reshape(n, FOX_WIDTH), o_b.reshape(n, RET_V_WIDTH), params["g_attn"],
                params["w_gates"], params["w_branch_a"], params["w_branch_b"], params["w_out"])
    y = _mlp(x1, params["g_mlp"], params["w_up"], params["w_down"], params["g_final"])
    return y.reshape(b, t, D_MODEL), state[None]


def kernel(x_prompt, x_sample, cache_fox_k, cache_fox_v, cache_fox_logf, state_ret, g_attn, w_in, b_forget,
           g_ret_norm, w_branch, w_out, g_mlp, w_up, w_down, g_final):
    assert w_in.shape[0] == 1, "single-layer trunk"
    wi = w_in[0]
    fox_cols = 3 * FOX_WIDTH + N_FOX_HEADS
    qk_end = fox_cols + 2 * RET_QK_WIDTH
    vg_end = qk_end + 2 * RET_V_WIDTH
    lane_pad = LANES - N_FOX_HEADS
    lg = jnp.log(1.0 - 2.0 ** (-5.0 - jnp.arange(N_RET_HEADS, dtype=F32)))
    w_fox = jnp.pad(wi[:, :fox_cols], ((0, 0), (0, lane_pad))).astype(BF16)
    b_row = jnp.pad(b_forget[0], (0, lane_pad)).reshape(1, LANES)
    params = {
        "g_attn": g_attn[0].reshape(1, D_MODEL),
        "w_ret_qk": wi[:, fox_cols:qk_end].astype(BF16),
        "w_ret_vg": wi[:, qk_end:vg_end].astype(BF16),
        "w_gates": wi[:, vg_end:].astype(BF16),
        "lg": jnp.broadcast_to(lg[:, None, None], (N_RET_HEADS, 8, LANES)),
        "g_ret_norm": g_ret_norm[0].reshape(N_RET_HEADS, 1, RET_VAL_DIM),
        "w_branch_a": w_branch[0, :FOX_WIDTH].astype(BF16),
        "w_branch_b": w_branch[0, FOX_WIDTH:].astype(BF16),
        "w_out": w_out[0].astype(BF16),
        "g_mlp": g_mlp[0].reshape(1, D_MODEL),
        "w_up": w_up[0].astype(BF16),
        "w_down": w_down[0].astype(BF16),
        "g_final": g_final.reshape(1, D_MODEL),
    }
    bp, tp, _ = x_prompt.shape
    bs, ts, _ = x_sample.shape
    past = cache_fox_k.shape[2]

    qt, kt, vt, lft, lf_pad, q_norm, k_norm = _proj_fox_t(
        x_prompt, params["g_attn"], wi[:, :fox_cols].T.astype(BF16), w_fox[:, 3 * FOX_WIDTH:],
        jnp.broadcast_to(b_forget[0][:, None], (N_FOX_HEADS, LANES)), b_row)
    c_pieces, c_end = _cumsum_pieces(lf_pad, CUMSUM_TILE, LOG2E)
    first = _first_key_block(c_end, q_norm, k_norm, tp, FOX_Q_BLOCK, FOX_BLOCK)
    o_a = _fox_prompt(first, qt, kt, vt, c_pieces)
    zero_state = jnp.zeros((bp, N_RET_HEADS, RET_KEY_DIM, RET_VAL_DIM), F32)
    yp, sp = _after_attention(x_prompt, jnp.arange(tp), o_a, params, RET_BLOCK, zero_state)
    to_heads = lambda a: jnp.transpose(a.reshape(bp, N_FOX_HEADS, FOX_HEAD_DIM, tp), (0, 3, 1, 2))[None]
    kp, vp = to_heads(kt), to_heads(vt)
    fp = jnp.transpose(lft, (0, 2, 1))[None]

    q, k, v, lf = _proj_fox(x_sample.reshape(bs * ts, D_MODEL), params["g_attn"], w_fox, b_row)
    feature_major = lambda a: jnp.transpose(a[0], (0, 2, 3, 1)).reshape(bs, FOX_WIDTH, past)
    lf_time = jnp.concatenate([jnp.transpose(cache_fox_logf[0], (0, 2, 1)),
                               jnp.transpose(lf.reshape(bs, ts, N_FOX_HEADS), (0, 2, 1))], axis=2)
    lf_time = jnp.pad(lf_time, ((0, 0), (0, 0), (0, -(past + ts) % LANES)))
    o_a = _fox_sample(q.reshape(bs, ts, FOX_WIDTH), k.reshape(bs, ts, FOX_WIDTH), v.reshape(bs, ts, FOX_WIDTH),
                      feature_major(cache_fox_k), feature_major(cache_fox_v), _cumsum_lanes(lf_time))
    ys, ss = _after_attention(x_sample, past + jnp.arange(ts), o_a, params, CHUNK, state_ret[0])
    shape5 = (1, bs, ts, N_FOX_HEADS, FOX_HEAD_DIM)
    return (yp, ys, kp, vp, fp, sp, k.reshape(shape5), v.reshape(shape5),
            lf.reshape(1, bs, ts, N_FOX_HEADS), ss)
```

```python
import functools

import jax
import jax.numpy as jnp
from jax import lax
from jax.experimental import pallas as pl
from jax.experimental.pallas import tpu as pltpu

D_MODEL = 1024
N_FOX_HEADS = 16
FOX_HEAD_DIM = 64
FOX_WIDTH = N_FOX_HEADS * FOX_HEAD_DIM
N_RET_HEADS = 4
RET_KEY_DIM = 256
RET_VAL_DIM = 512
RET_QK_WIDTH = N_RET_HEADS * RET_KEY_DIM
RET_V_WIDTH = N_RET_HEADS * RET_VAL_DIM
D_FF = 4 * D_MODEL
CHUNK = 64
ROPE_BASE = 10000.0
EPS = 1e-6

LANES = 128
HEADS_PER_LANE_BLOCK = LANES // FOX_HEAD_DIM
N_HEAD_PAIRS = N_FOX_HEADS // HEADS_PER_LANE_BLOCK
N_C_PIECES = 3
ONES_ROWS = 16
MASK_VALUE = -1e30
LOG2E = 1.4426950408889634
VMEM_LIMIT = 56 * 1024 * 1024

TOKEN_TILE = 512
MIX_TOKEN_TILE = 512
FOX_BLOCK = 256
FOX_Q_BLOCK = 512
SAMPLE_PAIRS_PER_STEP = 4
RET_BLOCK = 256
RET_HEADS_PER_STEP = 4
CUMSUM_TILE = 512
SKIP_EXPONENT = 160.0
NORM_SLACK = 1.01

BF16 = jnp.bfloat16
F32 = jnp.float32


def _cparams(n_axes):
    return pltpu.CompilerParams(dimension_semantics=("arbitrary",) * n_axes,
                                vmem_limit_bytes=VMEM_LIMIT)


def _resident(shape, index_map):
    return pl.BlockSpec(shape, index_map, pipeline_mode=pl.Buffered(1))


def _dot(a, b):
    return jnp.dot(a, b, preferred_element_type=F32)


def _dot_nt(a, b):
    return lax.dot_general(a, b, (((1,), (1,)), ((), ())), preferred_element_type=F32)


def _rmsnorm(x, g):
    return x * lax.rsqrt(jnp.mean(x * x, axis=-1, keepdims=True) + EPS) * g


def _log_sigmoid(z):
    return -(jnp.maximum(-z, 0.0) + jnp.log(1.0 + jnp.exp(-jnp.abs(z))))


def _split3(x):
    a = x.astype(BF16)
    r = x - a.astype(F32)
    b = r.astype(BF16)
    c = (r - b.astype(F32)).astype(BF16)
    return a, b, c


def _indicator(idx, lo, hi):
    return jnp.where(idx >= lo, jnp.where(idx < hi, 1.0, 0.0), 0.0)


def _proj_fox_t_kernel(x_ref, g_ref, wt_ref, wf_ref, bcol_ref, brow_ref, qt_ref, kt_ref, vt_ref, lft_ref, lfp_ref,
                       qn_ref, kn_ref):
    h = _rmsnorm(x_ref[0], g_ref[...])
    ht = h.T.astype(BF16)
    w = FOX_WIDTH
    qs = _dot(wt_ref[0:w, :], ht) * (FOX_HEAD_DIM ** -0.5 * LOG2E)
    kf = _dot(wt_ref[w:2 * w, :], ht)
    qt_ref[0] = qs.astype(BF16)
    kt_ref[0] = kf
    for src, dst in ((qs, qn_ref), (kf, kn_ref)):
        sq = jnp.sum((src * src).reshape(N_FOX_HEADS, FOX_HEAD_DIM, src.shape[1]), axis=1)
        dst[0, 0] = jnp.broadcast_to(jnp.sqrt(jnp.max(sq, axis=1, keepdims=True)), (N_FOX_HEADS, LANES))
    vt_ref[0] = _dot(wt_ref[2 * w:3 * w, :], ht)
    lft_ref[0] = _log_sigmoid(_dot(wt_ref[3 * w:3 * w + N_FOX_HEADS, :], ht) + bcol_ref[:, 0:1])
    lfp_ref[0] = _log_sigmoid(_dot(h.astype(BF16), wf_ref[...]) + brow_ref[...])


def _proj_fox_t(x, g, wt, wf, bcol, brow):
    b, t, _ = x.shape
    tm = TOKEN_TILE
    fixed = lambda i, j: (0, 0)
    feat = lambda i, j: (i, 0, j)
    return pl.pallas_call(
        _proj_fox_t_kernel,
        grid=(b, t // tm),
        in_specs=[pl.BlockSpec((1, tm, D_MODEL), lambda i, j: (i, j, 0)), pl.BlockSpec((1, D_MODEL), fixed),
                  pl.BlockSpec(wt.shape, fixed), pl.BlockSpec(wf.shape, fixed),
                  pl.BlockSpec(bcol.shape, fixed), pl.BlockSpec(brow.shape, fixed)],
        out_specs=[pl.BlockSpec((1, FOX_WIDTH, tm), feat), pl.BlockSpec((1, FOX_WIDTH, tm), feat),
                   pl.BlockSpec((1, FOX_WIDTH, tm), feat), pl.BlockSpec((1, N_FOX_HEADS, tm), feat),
                   pl.BlockSpec((1, tm, LANES), lambda i, j: (i, j, 0)),
                   pl.BlockSpec((1, 1, N_FOX_HEADS, LANES), lambda i, j: (i, j, 0, 0)),
                   pl.BlockSpec((1, 1, N_FOX_HEADS, LANES), lambda i, j: (i, j, 0, 0))],
        out_shape=[jax.ShapeDtypeStruct((b, FOX_WIDTH, t), BF16), jax.ShapeDtypeStruct((b, FOX_WIDTH, t), F32),
                   jax.ShapeDtypeStruct((b, FOX_WIDTH, t), F32), jax.ShapeDtypeStruct((b, N_FOX_HEADS, t), F32),
                   jax.ShapeDtypeStruct((b, t, LANES), F32),
                   jax.ShapeDtypeStruct((b, t // tm, N_FOX_HEADS, LANES), F32),
                   jax.ShapeDtypeStruct((b, t // tm, N_FOX_HEADS, LANES), F32)],
        compiler_params=_cparams(2), name="proj_fox_t",
    )(x, g, wt, wf, bcol, brow)


def _proj_fox_kernel(x_ref, g_ref, w_ref, b_ref, q_ref, k_ref, v_ref, lf_ref):
    h = _rmsnorm(x_ref[...], g_ref[...]).astype(BF16)
    w = FOX_WIDTH
    q_ref[...] = (_dot(h, w_ref[:, 0:w]) * (FOX_HEAD_DIM ** -0.5)).astype(BF16)
    k_ref[...] = _dot(h, w_ref[:, w:2 * w])
    v_ref[...] = _dot(h, w_ref[:, 2 * w:3 * w])
    lf = _log_sigmoid(_dot(h, w_ref[:, 3 * w:3 * w + LANES]) + b_ref[...])
    lf_ref[...] = lf[:, 0:N_FOX_HEADS]


def _proj_fox(x, g, w, b):
    n = x.shape[0]
    tm = TOKEN_TILE
    tok = lambda i: (i, 0)
    fixed = lambda i: (0, 0)
    return pl.pallas_call(
        _proj_fox_kernel,
        grid=(n // tm,),
        in_specs=[pl.BlockSpec((tm, D_MODEL), tok), pl.BlockSpec((1, D_MODEL), fixed),
                  pl.BlockSpec(w.shape, fixed), pl.BlockSpec((1, LANES), fixed)],
        out_specs=[pl.BlockSpec((tm, FOX_WIDTH), tok), pl.BlockSpec((tm, FOX_WIDTH), tok),
                   pl.BlockSpec((tm, FOX_WIDTH), tok), pl.BlockSpec((tm, N_FOX_HEADS), tok)],
        out_shape=[jax.ShapeDtypeStruct((n, FOX_WIDTH), BF16), jax.ShapeDtypeStruct((n, FOX_WIDTH), F32),
                   jax.ShapeDtypeStruct((n, FOX_WIDTH), F32), jax.ShapeDtypeStruct((n, N_FOX_HEADS), F32)],
        compiler_params=_cparams(1), name="proj_fox",
    )(x, g, w, b)


def _proj_ret_qk_kernel(x_ref, g_ref, w_ref, cos_ref, sin_ref, q_ref, k_ref):
    h = _rmsnorm(x_ref[...], g_ref[...]).astype(BF16)
    cos = cos_ref[...]
    sin = sin_ref[...]
    half = RET_KEY_DIM // 2
    for out_ref, base, scale in ((q_ref, 0, 1.0), (k_ref, RET_QK_WIDTH, RET_KEY_DIM ** -0.5)):
        for hd in range(N_RET_HEADS):
            lo = hd * RET_KEY_DIM
            z = _dot(h, w_ref[:, base + lo:base + lo + RET_KEY_DIM])
            x1 = z[:, :half]
            x2 = z[:, half:]
            out_ref[:, lo:lo + half] = ((x1 * cos - x2 * sin) * scale).astype(BF16)
            out_ref[:, lo + half:lo + RET_KEY_DIM] = ((x1 * sin + x2 * cos) * scale).astype(BF16)


def _proj_ret_qk(x, g, w, cos, sin):
    n = x.shape[0]
    tm = TOKEN_TILE
    period = cos.shape[0] // tm
    tok = lambda i: (i, 0)
    fixed = lambda i: (0, 0)
    pos = lambda i: (i % period, 0)
    half = RET_KEY_DIM // 2
    return pl.pallas_call(
        _proj_ret_qk_kernel,
        grid=(n // tm,),
        in_specs=[pl.BlockSpec((tm, D_MODEL), tok), pl.BlockSpec((1, D_MODEL), fixed),
                  pl.BlockSpec((D_MODEL, 2 * RET_QK_WIDTH), fixed),
                  pl.BlockSpec((tm, half), pos), pl.BlockSpec((tm, half), pos)],
        out_specs=[pl.BlockSpec((tm, RET_QK_WIDTH), tok), pl.BlockSpec((tm, RET_QK_WIDTH), tok)],
        out_shape=[jax.ShapeDtypeStruct((n, RET_QK_WIDTH), BF16)] * 2,
        compiler_params=_cparams(1), name="proj_ret_qk",
    )(x, g, w, cos, sin)


def _proj_ret_vg_kernel(x_ref, g_ref, w_ref, v_ref, gr_ref):
    h = _rmsnorm(x_ref[...], g_ref[...]).astype(BF16)
    v_ref[...] = _dot(h, w_ref[:, 0:RET_V_WIDTH]).astype(BF16)
    gr_ref[...] = _dot(h, w_ref[:, RET_V_WIDTH:2 * RET_V_WIDTH])


def _proj_ret_vg(x, g, w):
    n = x.shape[0]
    tm = TOKEN_TILE
    tok = lambda i: (i, 0)
    fixed = lambda i: (0, 0)
    return pl.pallas_call(
        _proj_ret_vg_kernel,
        grid=(n // tm,),
        in_specs=[pl.BlockSpec((tm, D_MODEL), tok), pl.BlockSpec((1, D_MODEL), fixed),
                  pl.BlockSpec((D_MODEL, 2 * RET_V_WIDTH), fixed)],
        out_specs=[pl.BlockSpec((tm, RET_V_WIDTH), tok), pl.BlockSpec((tm, RET_V_WIDTH), tok)],
        out_shape=[jax.ShapeDtypeStruct((n, RET_V_WIDTH), BF16), jax.ShapeDtypeStruct((n, RET_V_WIDTH), F32)],
        compiler_params=_cparams(1), name="proj_ret_vg",
    )(x, g, w)


def _cumsum_kernel(lf_ref, c_ref, cend_ref, carry_ref, *, tb, scale):
    @pl.when(pl.program_id(1) == 0)
    def _():
        carry_ref[...] = jnp.zeros_like(carry_ref)

    row = lax.broadcasted_iota(jnp.int32, (tb, tb), 0)
    col = lax.broadcasted_iota(jnp.int32, (tb, tb), 1)
    tri = jnp.where(col <= row, 1.0, 0.0).astype(BF16)
    c = carry_ref[0:1, :]
    for piece in _split3(lf_ref[0]):
        c = c + _dot(tri, piece)
    carry_ref[...] = jnp.broadcast_to(c[tb - 1:tb, :], carry_ref.shape)
    erow = lax.broadcasted_iota(jnp.int32, (8, LANES), 0)
    ends = jnp.zeros((8, LANES), F32)
    for e in range(tb // FOX_BLOCK):
        ends = jnp.where(erow == e, c[(e + 1) * FOX_BLOCK - 1:(e + 1) * FOX_BLOCK, :] * scale, ends)
    cend_ref[0, 0] = ends

    prow = lax.broadcasted_iota(jnp.int32, (LANES, LANES), 0)
    pcol = lax.broadcasted_iota(jnp.int32, (LANES, LANES), 1)
    out = jnp.zeros((tb, LANES), F32)
    for p, piece in enumerate(_split3(-c * scale)):
        place = jnp.where(pcol == prow + p * N_FOX_HEADS, jnp.where(prow < N_FOX_HEADS, 1.0, 0.0), 0.0)
        out = out + _dot(piece, place.astype(BF16))
    c_ref[0] = out.astype(BF16)


def _cumsum_pieces(lf, tb, scale):
    b, t, _ = lf.shape
    blk = lambda i, j: (i, j, 0)
    return pl.pallas_call(
        functools.partial(_cumsum_kernel, tb=tb, scale=scale),
        grid=(b, t // tb),
        in_specs=[pl.BlockSpec((1, tb, LANES), blk)],
        out_specs=[pl.BlockSpec((1, tb, LANES), blk), pl.BlockSpec((1, 1, 8, LANES), lambda i, j: (i, j, 0, 0))],
        out_shape=[jax.ShapeDtypeStruct((b, t, LANES), BF16), jax.ShapeDtypeStruct((b, t // tb, 8, LANES), F32)],
        scratch_shapes=[pltpu.VMEM((8, LANES), F32)],
        compiler_params=_cparams(2), name="cumsum",
    )(lf)


def _cumsum_lanes_kernel(lf_ref, c_ref):
    heads, t = lf_ref.shape[1], lf_ref.shape[2]
    row = lax.broadcasted_iota(jnp.int32, (LANES, LANES), 0)
    col = lax.broadcasted_iota(jnp.int32, (LANES, LANES), 1)
    tri = jnp.where(row <= col, 1.0, 0.0).astype(BF16)
    ones = jnp.ones((LANES, LANES), BF16)
    carry = jnp.zeros((heads, LANES), F32)
    for seg in range(t // LANES):
        lanes = slice(seg * LANES, (seg + 1) * LANES)
        stack = jnp.concatenate(_split3(lf_ref[0, :, lanes]), axis=0)
        within = _dot(stack, tri)
        total = _dot(stack, ones)
        fold = lambda a: a[0:heads] + a[heads:2 * heads] + a[2 * heads:3 * heads]
        c_ref[0, :, lanes] = fold(within) + carry
        carry = carry + fold(total)


def _cumsum_lanes(lf):
    b, heads, t = lf.shape
    blk = lambda i: (i, 0, 0)
    return pl.pallas_call(
        _cumsum_lanes_kernel,
        grid=(b,),
        in_specs=[pl.BlockSpec((1, heads, t), blk)],
        out_specs=pl.BlockSpec((1, heads, t), blk),
        out_shape=jax.ShapeDtypeStruct((b, heads, t), F32),
        compiler_params=_cparams(1), name="cumsum_lanes",
    )(lf)


def _piece_selector(shape, axis, head):
    idx = lax.broadcasted_iota(jnp.int32, shape, axis)
    hit = jnp.where(idx < N_C_PIECES * N_FOX_HEADS, jnp.where((idx & (N_FOX_HEADS - 1)) == head, 1.0, 0.0), 0.0)
    return hit


_ROW_M, _ROW_MX, _ROW_ALPHA = 0, 1, 2


def _fox_prompt_kernel(first_ref, qt_ref, kt_ref, vt_ref, c_ref, o_ref, kaug_ref, vta_ref, qa_ref, s_ref, p_ref, acc_ref,
                       stat_ref, *, t, tq, blk):
    pair = pl.program_id(1)
    qi = pl.program_id(2)
    nblk = t // blk
    diag_blocks = tq // blk
    d = FOX_HEAD_DIM
    heads = range(HEADS_PER_LANE_BLOCK)

    @pl.when(qi == 0)
    def _build():
        ones = jnp.ones((ONES_ROWS, blk), F32)

        def body(j, carry):
            off = pl.multiple_of(j * blk, blk)
            kaug_ref[pl.ds(off, blk), 0:LANES] = kt_ref[0, :, pl.ds(off, blk)].T.astype(BF16)
            kaug_ref[pl.ds(off, blk), LANES:2 * LANES] = c_ref[0, pl.ds(off, blk), :]
            for h in heads:
                vh = vt_ref[0, h * d:(h + 1) * d, pl.ds(off, blk)]
                vta_ref[j, h] = jnp.concatenate([vh, ones], axis=0).astype(BF16)
            return carry

        lax.fori_loop(0, nblk, body, 0)

    qt = qt_ref[0]
    row = lax.broadcasted_iota(jnp.int32, (LANES, tq), 0)
    for h in heads:
        in_head = _indicator(row, h * d, (h + 1) * d).astype(BF16)
        sel = _piece_selector((LANES, tq), 0, pair * HEADS_PER_LANE_BLOCK + h).astype(BF16)
        qa_ref[h] = jnp.concatenate([qt * in_head, sel], axis=0)

    acc_ref[...] = jnp.zeros_like(acc_ref)
    stat_ref[...] = jnp.full(stat_ref.shape, MASK_VALUE, F32)
    krow = lax.broadcasted_iota(jnp.int32, (blk, tq), 0)
    qcol = lax.broadcasted_iota(jnp.int32, (blk, tq), 1)

    def stat(h, slot, kind):
        r = (h * 2 + slot) * 4 + kind
        return slice(r, r + 1)

    def stage_a(j, slot, diag):
        off = pl.multiple_of(j * blk, blk)
        kb = kaug_ref[pl.ds(off, blk), :]
        for h in heads:
            s = _dot(kb, qa_ref[h])
            if diag is not None:
                s = jnp.where(krow + diag * blk <= qcol, s, MASK_VALUE)
            s_ref[slot, h] = s
            stat_ref[stat(h, slot, _ROW_MX), :] = jnp.max(s, axis=0, keepdims=True)

    def stage_b(slot):
        for h in heads:
            m_old = stat_ref[stat(h, 0, _ROW_M), :]
            m_new = jnp.maximum(m_old, stat_ref[stat(h, slot, _ROW_MX), :])
            stat_ref[stat(h, slot, _ROW_ALPHA), :] = jnp.exp2(m_old - m_new)
            stat_ref[stat(h, 0, _ROW_M), :] = m_new
            p_ref[slot, h] = jnp.exp2(s_ref[slot, h] - m_new).astype(BF16)

    def stage_c(j, slot):
        for h in heads:
            alpha = stat_ref[stat(h, slot, _ROW_ALPHA), :]
            acc_ref[h] = acc_ref[h] * alpha + _dot(vta_ref[j, h], p_ref[slot, h])

    nfull = qi * diag_blocks
    j0 = first_ref[(pl.program_id(0) * N_HEAD_PAIRS + pair) * (t // tq) + qi]

    @pl.when(qi > 0)
    def _pipelined():
        stage_a(j0, 0, None)
        stage_b(0)
        stage_a(j0 + 1, 1, None)

        def two_steps(j):
            stage_c(j - 2, 0)
            stage_b(1)
            stage_a(j, 0, None)
            stage_c(j - 1, 1)
            stage_b(0)
            stage_a(j + 1, 1, None)

        def body(u, carry):
            two_steps(j0 + 2 + 4 * u)
            two_steps(j0 + 4 + 4 * u)
            return carry

        pairs_left = (nfull - j0) // 2 - 1
        lax.fori_loop(0, pairs_left // 2, body, 0)

        @pl.when(pairs_left % 2 == 1)
        def _odd_group():
            two_steps(j0 + 2 * pairs_left)
        for dg in range(diag_blocks):
            stage_c(nfull + dg - 2, dg % 2)
            stage_b((dg + 1) % 2)
            stage_a(nfull + dg, dg % 2, dg)
        stage_c(nfull + diag_blocks - 2, 0)
        stage_b(1)
        stage_c(nfull + diag_blocks - 1, 1)

    @pl.when(qi == 0)
    def _first_block():
        for dg in range(diag_blocks):
            stage_a(dg, dg % 2, dg)
            stage_b(dg % 2)
            stage_c(dg, dg % 2)

    for h in heads:
        a = acc_ref[h]
        o_ref[0, h * d:(h + 1) * d, :] = (a[0:d] / a[d:d + 1]).astype(BF16)


def _first_key_block(c_end, q_norm, k_norm, t, tq, blk):
    b = c_end.shape[0]
    nblk, nq, per = t // blk, t // tq, tq // blk
    ce = c_end[:, :, :CUMSUM_TILE // blk, :N_FOX_HEADS].reshape(b, nblk, N_FOX_HEADS)
    qn = q_norm[:, :, :, 0]
    kmax = jnp.max(k_norm[:, :, :, 0], axis=1)
    before = jnp.concatenate([jnp.zeros((b, 1, N_FOX_HEADS), F32), ce[:, per - 1:-1:per]], axis=1)
    decay = before[:, :, None, :] - ce[:, None, :, :]
    bound = (2.0 * NORM_SLACK * NORM_SLACK) * (qn * kmax[:, None, :])[:, :, None, :] + decay
    below = (jnp.arange(nblk)[None, :] < (jnp.arange(nq) * per)[:, None])[None, :, :, None]
    skippable = jnp.logical_and(bound <= -SKIP_EXPONENT, below)
    block = jnp.arange(nblk, dtype=jnp.int32)[None, None, :, None]
    n_skip = jnp.min(jnp.where(skippable, nblk, block), axis=2)
    n_skip = jnp.min(n_skip.reshape(b, nq, N_HEAD_PAIRS, HEADS_PER_LANE_BLOCK), axis=3)
    first = jnp.clip((n_skip // 2) * 2, 0, jnp.maximum(jnp.arange(nq) * per - 2, 0)[None, :, None])
    return jnp.transpose(first, (0, 2, 1)).reshape(-1).astype(jnp.int32)


def _fox_prompt(first, qt, kt, vt, c):
    b, _, t = qt.shape
    blk = FOX_BLOCK
    tq = FOX_Q_BLOCK
    assert tq == 2 * blk, "the pipeline keeps two key blocks in flight and slot = block parity"
    nblk = t // blk
    hp = HEADS_PER_LANE_BLOCK
    acc_rows = FOX_HEAD_DIM + ONES_ROWS
    per_pair_all_time = lambda i, p, j, first: (i, p, 0)
    grid_spec = pltpu.PrefetchScalarGridSpec(
        num_scalar_prefetch=1,
        grid=(b, N_HEAD_PAIRS, t // tq),
        in_specs=[pl.BlockSpec((1, LANES, tq), lambda i, p, j, first: (i, p, j)),
                  pl.BlockSpec((1, LANES, t), per_pair_all_time),
                  pl.BlockSpec((1, LANES, t), per_pair_all_time),
                  pl.BlockSpec((1, t, LANES), lambda i, p, j, first: (i, 0, 0))],
        out_specs=pl.BlockSpec((1, LANES, tq), lambda i, p, j, first: (i, p, j)),
        scratch_shapes=[pltpu.VMEM((t, 2 * LANES), BF16),
                        pltpu.VMEM((nblk, hp, acc_rows, blk), BF16),
                        pltpu.VMEM((hp, 2 * LANES, tq), BF16),
                        pltpu.VMEM((2, hp, blk, tq), F32),
                        pltpu.VMEM((2, hp, blk, tq), BF16),
                        pltpu.VMEM((hp, acc_rows, tq), F32),
                        pltpu.VMEM((hp * 2 * 4, tq), F32)])
    return pl.pallas_call(
        functools.partial(_fox_prompt_kernel, t=t, tq=tq, blk=blk),
        grid_spec=grid_spec,
        out_shape=jax.ShapeDtypeStruct((b, FOX_WIDTH, t), BF16),
        compiler_params=_cparams(3), name="fox_prompt",
    )(first, qt, kt, vt, c)


def _fox_sample_kernel(q_ref, kn_ref, vn_ref, kc_ref, vc_ref, c_ref, o_ref, *, past, new, pairs):
    group = pl.program_id(1)
    d = FOX_HEAD_DIM
    rows = HEADS_PER_LANE_BLOCK * new
    lane = lax.broadcasted_iota(jnp.int32, (rows, LANES), 1)
    qrow = lax.broadcasted_iota(jnp.int32, (rows, LANES), 0)
    head_lanes = jnp.where(qrow < new, _indicator(lane, 0, d), _indicator(lane, d, 2 * d)).astype(BF16)
    out_lane = lax.broadcasted_iota(jnp.int32, (new, LANES), 1)
    nrow = lax.broadcasted_iota(jnp.int32, (rows, new), 0)
    ncol = lax.broadcasted_iota(jnp.int32, (rows, new), 1)
    causal = ncol <= (nrow & (new - 1))
    for pp in range(pairs):
        lanes = slice(pp * LANES, (pp + 1) * LANES)
        head0 = (group * pairs + pp) * HEADS_PER_LANE_BLOCK
        tpad = c_ref.shape[2]
        bias = jnp.concatenate(
            [jnp.broadcast_to(c_ref[0, pl.ds(head0 + h, 1), :], (new, tpad)) for h in range(HEADS_PER_LANE_BLOCK)],
            axis=0)
        q = q_ref[0, :, lanes]
        q2 = jnp.concatenate([q, q], axis=0) * head_lanes
        s_c = _dot(q2, kc_ref[0, lanes, :].astype(BF16)) - bias[:, 0:past]
        s_n = _dot_nt(q2, kn_ref[0, :, lanes].astype(BF16)) - bias[:, past:past + new]
        s_n = jnp.where(causal, s_n, MASK_VALUE)
        m = jnp.maximum(jnp.max(s_c, axis=-1, keepdims=True), jnp.max(s_n, axis=-1, keepdims=True))
        e_c = jnp.exp(s_c - m)
        e_n = jnp.exp(s_n - m)
        l = jnp.sum(e_c, axis=-1, keepdims=True) + jnp.sum(e_n, axis=-1, keepdims=True)
        o2 = _dot_nt(e_c.astype(BF16), vc_ref[0, lanes, :].astype(BF16))
        o2 = (o2 + _dot(e_n.astype(BF16), vn_ref[0, :, lanes].astype(BF16))) / l
        o = jnp.where(out_lane < d, o2[0:new], o2[new:rows])
        o_ref[0, :, lanes] = o.astype(BF16)


def _fox_sample(q, kn, vn, kct, vct, c):
    b, new, _ = q.shape
    past = kct.shape[2]
    pairs = SAMPLE_PAIRS_PER_STEP
    width = pairs * LANES
    tok = lambda i, g: (i, 0, g)
    feat = lambda i, g: (i, g, 0)
    return pl.pallas_call(
        functools.partial(_fox_sample_kernel, past=past, new=new, pairs=pairs),
        grid=(b, N_HEAD_PAIRS // pairs),
        in_specs=[pl.BlockSpec((1, new, width), tok), pl.BlockSpec((1, new, width), tok),
                  pl.BlockSpec((1, new, width), tok), pl.BlockSpec((1, width, past), feat),
                  pl.BlockSpec((1, width, past), feat),
                  pl.BlockSpec((1,) + c.shape[1:], lambda i, g: (i, 0, 0))],
        out_specs=pl.BlockSpec((1, new, width), tok),
        out_shape=jax.ShapeDtypeStruct((b, new, FOX_WIDTH), BF16),
        compiler_params=_cparams(2), name="fox_sample",
    )(q, kn, vn, kct, vct, c)


def _retention_kernel(lg_ref, q_ref, k_ref, v_ref, g_ref, gn_ref, s0_ref, o_ref, st_ref,
                      decay_ref, qdec_ref, kdec_ref, *, blk, heads):
    @pl.when(pl.program_id(2) == 0)
    def _():
        n = lax.broadcasted_iota(jnp.int32, (blk, blk), 0)
        m = lax.broadcasted_iota(jnp.int32, (blk, blk), 1)
        shift = CHUNK.bit_length() - 1
        dist = jnp.abs(n - m).astype(F32)
        same_or_earlier_chunk = (m >> shift) <= (n >> shift)
        pos = lax.broadcasted_iota(jnp.int32, (blk, RET_KEY_DIM), 0).astype(F32)
        for h in range(heads):
            lg = lg_ref[h, 0:1, 0:1]
            st_ref[0, h] = s0_ref[0, h]
            decay_ref[h] = jnp.where(same_or_earlier_chunk, jnp.exp(lg * dist), 0.0)
            qdec_ref[h] = jnp.exp(lg * (pos + 1.0))
            kdec_ref[h] = jnp.exp(lg * (blk - 1.0 - pos))

    for h in range(heads):
        lg = lg_ref[h, 0:1, 0:1]
        keys = slice(h * RET_KEY_DIM, (h + 1) * RET_KEY_DIM)
        vals = slice(h * RET_VAL_DIM, (h + 1) * RET_VAL_DIM)
        q = q_ref[0, :, keys]
        k = k_ref[0, :, keys]
        v = v_ref[0, :, vals]
        state = st_ref[0, h]
        s = _dot_nt(q, k) * decay_ref[h]
        qd = (q.astype(F32) * qdec_ref[h]).astype(BF16)
        y = _dot(s.astype(BF16), v) + _dot(qd, state.astype(BF16))
        kd = k.astype(F32) * kdec_ref[h]
        st_ref[0, h] = jnp.exp(lg * blk) * state + _dot(kd.T.astype(BF16), v)

        yn = y * lax.rsqrt(jnp.mean(y * y, axis=-1, keepdims=True) + EPS) * gn_ref[h]
        g = g_ref[0, :, vals]
        o_ref[0, :, vals] = (g * jax.nn.sigmoid(g) * yn).astype(BF16)


def _retention(lg, q, k, v, g, gn, s0, blk):
    b, t, _ = q.shape
    heads = RET_HEADS_PER_STEP
    qk = lambda i, h, j: (i, j, h)
    per_group = lambda i, h, j: (h, 0, 0)
    group_state = lambda i, h, j: (i, h, 0, 0)
    return pl.pallas_call(
        functools.partial(_retention_kernel, blk=blk, heads=heads),
        grid=(b, N_RET_HEADS // heads, t // blk),
        in_specs=[pl.BlockSpec((heads, 8, LANES), per_group),
                  pl.BlockSpec((1, blk, heads * RET_KEY_DIM), qk), pl.BlockSpec((1, blk, heads * RET_KEY_DIM), qk),
                  pl.BlockSpec((1, blk, heads * RET_VAL_DIM), qk), pl.BlockSpec((1, blk, heads * RET_VAL_DIM), qk),
                  pl.BlockSpec((heads, 1, RET_VAL_DIM), per_group),
                  pl.BlockSpec((1, heads, RET_KEY_DIM, RET_VAL_DIM), group_state)],
        out_specs=[pl.BlockSpec((1, blk, heads * RET_VAL_DIM), qk),
                   pl.BlockSpec((1, heads, RET_KEY_DIM, RET_VAL_DIM), group_state)],
        out_shape=[jax.ShapeDtypeStruct((b, t, RET_V_WIDTH), BF16),
                   jax.ShapeDtypeStruct((b, N_RET_HEADS, RET_KEY_DIM, RET_VAL_DIM), F32)],
        scratch_shapes=[pltpu.VMEM((heads, blk, blk), F32), pltpu.VMEM((heads, blk, RET_KEY_DIM), F32),
                        pltpu.VMEM((heads, blk, RET_KEY_DIM), F32)],
        compiler_params=_cparams(3), name="retention",
    )(lg, q, k, v, g, gn, s0)


def _mixer_kernel(x_ref, oa_ref, ob_ref, g_ref, wg_ref, wa_ref, wb_ref, wo_ref, x1_ref, *, oa_feature_major):
    x = x_ref[...]
    h = _rmsnorm(x, g_ref[...]).astype(BF16)
    gate_a = jax.nn.sigmoid(_dot(h, wg_ref[:, 0:D_MODEL]))
    gate_b = jax.nn.sigmoid(_dot(h, wg_ref[:, D_MODEL:2 * D_MODEL]))
    if oa_feature_major:
        ya = lax.dot_general(oa_ref[0], wa_ref[...], (((0,), (0,)), ((), ())), preferred_element_type=F32)
    else:
        ya = _dot(oa_ref[...], wa_ref[...])
    merged = gate_a * ya + gate_b * _dot(ob_ref[...], wb_ref[...])
    x1_ref[...] = x + _dot(merged.astype(BF16), wo_ref[...])


def _mixer(x, oa, ob, g, wg, wa, wb, wo):
    n = x.shape[0]
    tm = MIX_TOKEN_TILE
    tok = lambda i: (i, 0)
    fixed = lambda i: (0, 0)
    feature_major = oa.ndim == 3
    if feature_major:
        per_batch = oa.shape[2] // tm
        oa_spec = pl.BlockSpec((1, FOX_WIDTH, tm), lambda i: (i // per_batch, 0, i % per_batch))
    else:
        oa_spec = pl.BlockSpec((tm, FOX_WIDTH), tok)
    return pl.pallas_call(
        functools.partial(_mixer_kernel, oa_feature_major=feature_major),
        grid=(n // tm,),
        in_specs=[pl.BlockSpec((tm, D_MODEL), tok), oa_spec,
                  pl.BlockSpec((tm, RET_V_WIDTH), tok), pl.BlockSpec((1, D_MODEL), fixed),
                  _resident((D_MODEL, 2 * D_MODEL), fixed), _resident((FOX_WIDTH, D_MODEL), fixed),
                  _resident((RET_V_WIDTH, D_MODEL), fixed), _resident((D_MODEL, D_MODEL), fixed)],
        out_specs=pl.BlockSpec((tm, D_MODEL), tok),
        out_shape=jax.ShapeDtypeStruct((n, D_MODEL), F32),
        compiler_params=_cparams(1), name="mixer",
    )(x, oa, ob, g, wg, wa, wb, wo)


def _mlp_kernel(x_ref, g_ref, wu_ref, wd_ref, gf_ref, y_ref):
    x = x_ref[...]
    h = _rmsnorm(x, g_ref[...]).astype(BF16)
    acc = x
    for c in range(D_FF // D_MODEL):
        cols = slice(c * D_MODEL, (c + 1) * D_MODEL)
        u = jnp.square(jnp.maximum(_dot(h, wu_ref[:, cols]), 0.0)).astype(BF16)
        acc = acc + _dot(u, wd_ref[cols, :])
    y_ref[...] = _rmsnorm(acc, gf_ref[...])


def _mlp(x, g, wu, wd, gf):
    n = x.shape[0]
    tm = MIX_TOKEN_TILE
    tok = lambda i: (i, 0)
    fixed = lambda i: (0, 0)
    return pl.pallas_call(
        _mlp_kernel,
        grid=(n // tm,),
        in_specs=[pl.BlockSpec((tm, D_MODEL), tok), pl.BlockSpec((1, D_MODEL), fixed),
                  _resident((D_MODEL, D_FF), fixed), _resident((D_FF, D_MODEL), fixed),
                  pl.BlockSpec((1, D_MODEL), fixed)],
        out_specs=pl.BlockSpec((tm, D_MODEL), tok),
        out_shape=jax.ShapeDtypeStruct((n, D_MODEL), F32),
        compiler_params=_cparams(1), name="mlp",
    )(x, g, wu, wd, gf)


def _rope_tables(pos):
    inv_freq = ROPE_BASE ** (-jnp.arange(0, RET_KEY_DIM, 2, dtype=F32) / RET_KEY_DIM)
    ang = pos.astype(F32)[:, None] * inv_freq[None, :]
    return jnp.cos(ang), jnp.sin(ang)


def _after_attention(x, pos, o_a, params, ret_blk, state0):
    b, t, _ = x.shape
    n = b * t
    xf = x.reshape(n, D_MODEL)
    cos, sin = _rope_tables(pos)
    reps = max(1, TOKEN_TILE // t)
    q_r, k_r = _proj_ret_qk(xf, params["g_attn"], params["w_ret_qk"], jnp.tile(cos, (reps, 1)),
                            jnp.tile(sin, (reps, 1)))
    v_r, g_r = _proj_ret_vg(xf, params["g_attn"], params["w_ret_vg"])
    o_b, state = _retention(params["lg"], q_r.reshape(b, t, RET_QK_WIDTH), k_r.reshape(b, t, RET_QK_WIDTH),
                            v_r.reshape(b, t, RET_V_WIDTH), g_r.reshape(b, t, RET_V_WIDTH),
                            params["g_ret_norm"], state0, ret_blk)
    x1 = _mixer(xf, o_a, o_b.reshape(n, RET_V_WIDTH), params["g_attn"],
                params["w_gates"], params["w_branch_a"], params["w_branch_b"], params["w_out"])
    y = _mlp(x1, params["g_mlp"], params["w_up"], params["w_down"], params["g_final"])
    return y.reshape(b, t, D_MODEL), state[None]


def kernel(x_prompt, x_sample, cache_fox_k, cache_fox_v, cache_fox_logf, state_ret, g_attn, w_in, b_forget,
           g_ret_norm, w_branch, w_out, g_mlp, w_up, w_down, g_final):
    assert w_in.shape[0] == 1, "single-layer trunk"
    wi = w_in[0]
    fox_cols = 3 * FOX_WIDTH + N_FOX_HEADS
    qk_end = fox_cols + 2 * RET_QK_WIDTH
    vg_end = qk_end + 2 * RET_V_WIDTH
    lane_pad = LANES - N_FOX_HEADS
    lg = jnp.log(1.0 - 2.0 ** (-5.0 - jnp.arange(N_RET_HEADS, dtype=F32)))
    w_fox = jnp.pad(wi[:, :fox_cols], ((0, 0), (0, lane_pad))).astype(BF16)
    b_row = jnp.pad(b_forget[0], (0, lane_pad)).reshape(1, LANES)
    params = {
        "g_attn": g_attn[0].reshape(1, D_MODEL),
        "w_ret_qk": wi[:, fox_cols:qk_end].astype(BF16),
        "w_ret_vg": wi[:, qk_end:vg_end].astype(BF16),
        "w_gates": wi[:, vg_end:].astype(BF16),
        "lg": jnp.broadcast_to(lg[:, None, None], (N_RET_HEADS, 8, LANES)),
        "g_ret_norm": g_ret_norm[0].reshape(N_RET_HEADS, 1, RET_VAL_DIM),
        "w_branch_a": w_branch[0, :FOX_WIDTH].astype(BF16),
        "w_branch_b": w_branch[0, FOX_WIDTH:].astype(BF16),
        "w_out": w_out[0].astype(BF16),
        "g_mlp": g_mlp[0].reshape(1, D_MODEL),
        "w_up": w_up[0].astype(BF16),
        "w_down": w_down[0].astype(BF16),
        "g_final": g_final.reshape(1, D_MODEL),
    }
    bp, tp, _ = x_prompt.shape
    bs, ts, _ = x_sample.shape
    past = cache_fox_k.shape[2]

    qt, kt, vt, lft, lf_pad, q_norm, k_norm = _proj_fox_t(
        x_prompt, params["g_attn"], wi[:, :fox_cols].T.astype(BF16), w_fox[:, 3 * FOX_WIDTH:],
        jnp.broadcast_to(b_forget[0][:, None], (N_FOX_HEADS, LANES)), b_row)
    c_pieces, c_end = _cumsum_pieces(lf_pad, CUMSUM_TILE, LOG2E)
    first = _first_key_block(c_end, q_norm, k_norm, tp, FOX_Q_BLOCK, FOX_BLOCK)
    o_a = _fox_prompt(first, qt, kt, vt, c_pieces)
    zero_state = jnp.zeros((bp, N_RET_HEADS, RET_KEY_DIM, RET_VAL_DIM), F32)
    yp, sp = _after_attention(x_prompt, jnp.arange(tp), o_a, params, RET_BLOCK, zero_state)
    to_heads = lambda a: jnp.transpose(a.reshape(bp, N_FOX_HEADS, FOX_HEAD_DIM, tp), (0, 3, 1, 2))[None]
    kp, vp = to_heads(kt), to_heads(vt)
    fp = jnp.transpose(lft, (0, 2, 1))[None]

    q, k, v, lf = _proj_fox(x_sample.reshape(bs * ts, D_MODEL), params["g_attn"], w_fox, b_row)
    feature_major = lambda a: jnp.transpose(a[0], (0, 2, 3, 1)).reshape(bs, FOX_WIDTH, past)
    lf_time = jnp.concatenate([jnp.transpose(cache_fox_logf[0], (0, 2, 1)),
                               jnp.transpose(lf.reshape(bs, ts, N_FOX_HEADS), (0, 2, 1))], axis=2)
    lf_time = jnp.pad(lf_time, ((0, 0), (0, 0), (0, -(past + ts) % LANES)))
    o_a = _fox_sample(q.reshape(bs, ts, FOX_WIDTH), k.reshape(bs, ts, FOX_WIDTH), v.reshape(bs, ts, FOX_WIDTH),
                      feature_major(cache_fox_k), feature_major(cache_fox_v), _cumsum_lanes(lf_time))
    ys, ss = _after_attention(x_sample, past + jnp.arange(ts), o_a.reshape(bs * ts, FOX_WIDTH), params, CHUNK,
                              state_ret[0])
    shape5 = (1, bs, ts, N_FOX_HEADS, FOX_HEAD_DIM)
    return (yp, ys, kp, vp, fp, sp, k.reshape(shape5), v.reshape(shape5),
            lf.reshape(1, bs, ts, N_FOX_HEADS), ss)
```

```python
import functools

import jax
import jax.numpy as jnp
from jax import lax
from jax.experimental import pallas as pl
from jax.experimental.pallas import tpu as pltpu

D_MODEL = 1024
N_FOX_HEADS = 16
FOX_HEAD_DIM = 64
FOX_WIDTH = N_FOX_HEADS * FOX_HEAD_DIM
N_RET_HEADS = 4
RET_KEY_DIM = 256
RET_VAL_DIM = 512
RET_QK_WIDTH = N_RET_HEADS * RET_KEY_DIM
RET_V_WIDTH = N_RET_HEADS * RET_VAL_DIM
D_FF = 4 * D_MODEL
CHUNK = 64
ROPE_BASE = 10000.0
EPS = 1e-6

LANES = 128
HEADS_PER_LANE_BLOCK = LANES // FOX_HEAD_DIM
N_HEAD_PAIRS = N_FOX_HEADS // HEADS_PER_LANE_BLOCK
N_C_PIECES = 3
ONES_ROWS = 16
MASK_VALUE = -1e30
LOG2E = 1.4426950408889634
VMEM_LIMIT = 56 * 1024 * 1024

TOKEN_TILE = 512
MIX_TOKEN_TILE = 512
FOX_BLOCK = 256
FOX_Q_BLOCK = 512
SAMPLE_PAIRS_PER_STEP = 4
RET_BLOCK = 256
RET_HEADS_PER_STEP = 4
CUMSUM_TILE = 512
SKIP_EXPONENT = 160.0
NORM_SLACK = 1.01

BF16 = jnp.bfloat16
F32 = jnp.float32


def _cparams(n_axes):
    return pltpu.CompilerParams(dimension_semantics=("arbitrary",) * n_axes,
                                vmem_limit_bytes=VMEM_LIMIT)


def _resident(shape, index_map):
    return pl.BlockSpec(shape, index_map, pipeline_mode=pl.Buffered(1))


def _dot(a, b):
    return jnp.dot(a, b, preferred_element_type=F32)


def _dot_nt(a, b):
    return lax.dot_general(a, b, (((1,), (1,)), ((), ())), preferred_element_type=F32)


def _rmsnorm(x, g):
    return x * lax.rsqrt(jnp.mean(x * x, axis=-1, keepdims=True) + EPS) * g


def _log_sigmoid(z):
    return -(jnp.maximum(-z, 0.0) + jnp.log(1.0 + jnp.exp(-jnp.abs(z))))


def _split3(x):
    a = x.astype(BF16)
    r = x - a.astype(F32)
    b = r.astype(BF16)
    c = (r - b.astype(F32)).astype(BF16)
    return a, b, c


def _indicator(idx, lo, hi):
    return jnp.where(idx >= lo, jnp.where(idx < hi, 1.0, 0.0), 0.0)


def _proj_fox_t_kernel(x_ref, g_ref, wt_ref, wf_ref, bcol_ref, brow_ref, qt_ref, kt_ref, vt_ref, lft_ref, lfp_ref,
                       qn_ref, kn_ref, kb_ref):
    h = _rmsnorm(x_ref[0], g_ref[...])
    ht = h.T.astype(BF16)
    w = FOX_WIDTH
    qs = _dot(wt_ref[0:w, :], ht) * (FOX_HEAD_DIM ** -0.5 * LOG2E)
    kf = _dot(wt_ref[w:2 * w, :], ht)
    qt_ref[0] = qs.astype(BF16)
    kt_ref[0] = kf
    kb_ref[0] = kf.T.astype(BF16)
    for src, dst in ((qs, qn_ref), (kf, kn_ref)):
        sq = jnp.sum((src * src).reshape(N_FOX_HEADS, FOX_HEAD_DIM, src.shape[1]), axis=1)
        dst[0, 0] = jnp.broadcast_to(jnp.sqrt(jnp.max(sq, axis=1, keepdims=True)), (N_FOX_HEADS, LANES))
    vt_ref[0] = _dot(wt_ref[2 * w:3 * w, :], ht)
    lft_ref[0] = _log_sigmoid(_dot(wt_ref[3 * w:3 * w + N_FOX_HEADS, :], ht) + bcol_ref[:, 0:1])
    lfp_ref[0] = _log_sigmoid(_dot(h.astype(BF16), wf_ref[...]) + brow_ref[...])


def _proj_fox_t(x, g, wt, wf, bcol, brow):
    b, t, _ = x.shape
    tm = TOKEN_TILE
    fixed = lambda i, j: (0, 0)
    feat = lambda i, j: (i, 0, j)
    return pl.pallas_call(
        _proj_fox_t_kernel,
        grid=(b, t // tm),
        in_specs=[pl.BlockSpec((1, tm, D_MODEL), lambda i, j: (i, j, 0)), pl.BlockSpec((1, D_MODEL), fixed),
                  pl.BlockSpec(wt.shape, fixed), pl.BlockSpec(wf.shape, fixed),
                  pl.BlockSpec(bcol.shape, fixed), pl.BlockSpec(brow.shape, fixed)],
        out_specs=[pl.BlockSpec((1, FOX_WIDTH, tm), feat), pl.BlockSpec((1, FOX_WIDTH, tm), feat),
                   pl.BlockSpec((1, FOX_WIDTH, tm), feat), pl.BlockSpec((1, N_FOX_HEADS, tm), feat),
                   pl.BlockSpec((1, tm, LANES), lambda i, j: (i, j, 0)),
                   pl.BlockSpec((1, 1, N_FOX_HEADS, LANES), lambda i, j: (i, j, 0, 0)),
                   pl.BlockSpec((1, 1, N_FOX_HEADS, LANES), lambda i, j: (i, j, 0, 0)),
                   pl.BlockSpec((1, tm, FOX_WIDTH), lambda i, j: (i, j, 0))],
        out_shape=[jax.ShapeDtypeStruct((b, FOX_WIDTH, t), BF16), jax.ShapeDtypeStruct((b, FOX_WIDTH, t), F32),
                   jax.ShapeDtypeStruct((b, FOX_WIDTH, t), F32), jax.ShapeDtypeStruct((b, N_FOX_HEADS, t), F32),
                   jax.ShapeDtypeStruct((b, t, LANES), F32),
                   jax.ShapeDtypeStruct((b, t // tm, N_FOX_HEADS, LANES), F32),
                   jax.ShapeDtypeStruct((b, t // tm, N_FOX_HEADS, LANES), F32),
                   jax.ShapeDtypeStruct((b, t, FOX_WIDTH), BF16)],
        compiler_params=_cparams(2), name="proj_fox_t",
    )(x, g, wt, wf, bcol, brow)


def _proj_fox_kernel(x_ref, g_ref, w_ref, b_ref, q_ref, k_ref, v_ref, lf_ref):
    h = _rmsnorm(x_ref[...], g_ref[...]).astype(BF16)
    w = FOX_WIDTH
    q_ref[...] = (_dot(h, w_ref[:, 0:w]) * (FOX_HEAD_DIM ** -0.5)).astype(BF16)
    k_ref[...] = _dot(h, w_ref[:, w:2 * w])
    v_ref[...] = _dot(h, w_ref[:, 2 * w:3 * w])
    lf = _log_sigmoid(_dot(h, w_ref[:, 3 * w:3 * w + LANES]) + b_ref[...])
    lf_ref[...] = lf[:, 0:N_FOX_HEADS]


def _proj_fox(x, g, w, b):
    n = x.shape[0]
    tm = TOKEN_TILE
    tok = lambda i: (i, 0)
    fixed = lambda i: (0, 0)
    return pl.pallas_call(
        _proj_fox_kernel,
        grid=(n // tm,),
        in_specs=[pl.BlockSpec((tm, D_MODEL), tok), pl.BlockSpec((1, D_MODEL), fixed),
                  pl.BlockSpec(w.shape, fixed), pl.BlockSpec((1, LANES), fixed)],
        out_specs=[pl.BlockSpec((tm, FOX_WIDTH), tok), pl.BlockSpec((tm, FOX_WIDTH), tok),
                   pl.BlockSpec((tm, FOX_WIDTH), tok), pl.BlockSpec((tm, N_FOX_HEADS), tok)],
        out_shape=[jax.ShapeDtypeStruct((n, FOX_WIDTH), BF16), jax.ShapeDtypeStruct((n, FOX_WIDTH), F32),
                   jax.ShapeDtypeStruct((n, FOX_WIDTH), F32), jax.ShapeDtypeStruct((n, N_FOX_HEADS), F32)],
        compiler_params=_cparams(1), name="proj_fox",
    )(x, g, w, b)


def _proj_ret_qk_kernel(x_ref, g_ref, w_ref, cos_ref, sin_ref, q_ref, k_ref):
    h = _rmsnorm(x_ref[...], g_ref[...]).astype(BF16)
    cos = cos_ref[...]
    sin = sin_ref[...]
    half = RET_KEY_DIM // 2
    for out_ref, base, scale in ((q_ref, 0, 1.0), (k_ref, RET_QK_WIDTH, RET_KEY_DIM ** -0.5)):
        for hd in range(N_RET_HEADS):
            lo = hd * RET_KEY_DIM
            z = _dot(h, w_ref[:, base + lo:base + lo + RET_KEY_DIM])
            x1 = z[:, :half]
            x2 = z[:, half:]
            out_ref[:, lo:lo + half] = ((x1 * cos - x2 * sin) * scale).astype(BF16)
            out_ref[:, lo + half:lo + RET_KEY_DIM] = ((x1 * sin + x2 * cos) * scale).astype(BF16)


def _proj_ret_qk(x, g, w, cos, sin):
    n = x.shape[0]
    tm = TOKEN_TILE
    period = cos.shape[0] // tm
    tok = lambda i: (i, 0)
    fixed = lambda i: (0, 0)
    pos = lambda i: (i % period, 0)
    half = RET_KEY_DIM // 2
    return pl.pallas_call(
        _proj_ret_qk_kernel,
        grid=(n // tm,),
        in_specs=[pl.BlockSpec((tm, D_MODEL), tok), pl.BlockSpec((1, D_MODEL), fixed),
                  pl.BlockSpec((D_MODEL, 2 * RET_QK_WIDTH), fixed),
                  pl.BlockSpec((tm, half), pos), pl.BlockSpec((tm, half), pos)],
        out_specs=[pl.BlockSpec((tm, RET_QK_WIDTH), tok), pl.BlockSpec((tm, RET_QK_WIDTH), tok)],
        out_shape=[jax.ShapeDtypeStruct((n, RET_QK_WIDTH), BF16)] * 2,
        compiler_params=_cparams(1), name="proj_ret_qk",
    )(x, g, w, cos, sin)


def _proj_ret_vg_kernel(x_ref, g_ref, w_ref, v_ref, gr_ref):
    h = _rmsnorm(x_ref[...], g_ref[...]).astype(BF16)
    v_ref[...] = _dot(h, w_ref[:, 0:RET_V_WIDTH]).astype(BF16)
    gr_ref[...] = _dot(h, w_ref[:, RET_V_WIDTH:2 * RET_V_WIDTH])


def _proj_ret_vg(x, g, w):
    n = x.shape[0]
    tm = TOKEN_TILE
    tok = lambda i: (i, 0)
    fixed = lambda i: (0, 0)
    return pl.pallas_call(
        _proj_ret_vg_kernel,
        grid=(n // tm,),
        in_specs=[pl.BlockSpec((tm, D_MODEL), tok), pl.BlockSpec((1, D_MODEL), fixed),
                  pl.BlockSpec((D_MODEL, 2 * RET_V_WIDTH), fixed)],
        out_specs=[pl.BlockSpec((tm, RET_V_WIDTH), tok), pl.BlockSpec((tm, RET_V_WIDTH), tok)],
        out_shape=[jax.ShapeDtypeStruct((n, RET_V_WIDTH), BF16), jax.ShapeDtypeStruct((n, RET_V_WIDTH), F32)],
        compiler_params=_cparams(1), name="proj_ret_vg",
    )(x, g, w)


def _cumsum_kernel(lf_ref, c_ref, cend_ref, carry_ref, *, tb, scale):
    @pl.when(pl.program_id(1) == 0)
    def _():
        carry_ref[...] = jnp.zeros_like(carry_ref)

    row = lax.broadcasted_iota(jnp.int32, (tb, tb), 0)
    col = lax.broadcasted_iota(jnp.int32, (tb, tb), 1)
    tri = jnp.where(col <= row, 1.0, 0.0).astype(BF16)
    c = carry_ref[0:1, :]
    for piece in _split3(lf_ref[0]):
        c = c + _dot(tri, piece)
    carry_ref[...] = jnp.broadcast_to(c[tb - 1:tb, :], carry_ref.shape)
    erow = lax.broadcasted_iota(jnp.int32, (8, LANES), 0)
    ends = jnp.zeros((8, LANES), F32)
    for e in range(tb // FOX_BLOCK):
        ends = jnp.where(erow == e, c[(e + 1) * FOX_BLOCK - 1:(e + 1) * FOX_BLOCK, :] * scale, ends)
    cend_ref[0, 0] = ends

    prow = lax.broadcasted_iota(jnp.int32, (LANES, LANES), 0)
    pcol = lax.broadcasted_iota(jnp.int32, (LANES, LANES), 1)
    out = jnp.zeros((tb, LANES), F32)
    for p, piece in enumerate(_split3(-c * scale)):
        place = jnp.where(pcol == prow + p * N_FOX_HEADS, jnp.where(prow < N_FOX_HEADS, 1.0, 0.0), 0.0)
        out = out + _dot(piece, place.astype(BF16))
    c_ref[0] = out.astype(BF16)


def _cumsum_pieces(lf, tb, scale):
    b, t, _ = lf.shape
    blk = lambda i, j: (i, j, 0)
    return pl.pallas_call(
        functools.partial(_cumsum_kernel, tb=tb, scale=scale),
        grid=(b, t // tb),
        in_specs=[pl.BlockSpec((1, tb, LANES), blk)],
        out_specs=[pl.BlockSpec((1, tb, LANES), blk), pl.BlockSpec((1, 1, 8, LANES), lambda i, j: (i, j, 0, 0))],
        out_shape=[jax.ShapeDtypeStruct((b, t, LANES), BF16), jax.ShapeDtypeStruct((b, t // tb, 8, LANES), F32)],
        scratch_shapes=[pltpu.VMEM((8, LANES), F32)],
        compiler_params=_cparams(2), name="cumsum",
    )(lf)


def _cumsum_lanes_kernel(lf_ref, c_ref):
    heads, t = lf_ref.shape[1], lf_ref.shape[2]
    row = lax.broadcasted_iota(jnp.int32, (LANES, LANES), 0)
    col = lax.broadcasted_iota(jnp.int32, (LANES, LANES), 1)
    tri = jnp.where(row <= col, 1.0, 0.0).astype(BF16)
    ones = jnp.ones((LANES, LANES), BF16)
    carry = jnp.zeros((heads, LANES), F32)
    for seg in range(t // LANES):
        lanes = slice(seg * LANES, (seg + 1) * LANES)
        stack = jnp.concatenate(_split3(lf_ref[0, :, lanes]), axis=0)
        within = _dot(stack, tri)
        total = _dot(stack, ones)
        fold = lambda a: a[0:heads] + a[heads:2 * heads] + a[2 * heads:3 * heads]
        c_ref[0, :, lanes] = fold(within) + carry
        carry = carry + fold(total)


def _cumsum_lanes(lf):
    b, heads, t = lf.shape
    blk = lambda i: (i, 0, 0)
    return pl.pallas_call(
        _cumsum_lanes_kernel,
        grid=(b,),
        in_specs=[pl.BlockSpec((1, heads, t), blk)],
        out_specs=pl.BlockSpec((1, heads, t), blk),
        out_shape=jax.ShapeDtypeStruct((b, heads, t), F32),
        compiler_params=_cparams(1), name="cumsum_lanes",
    )(lf)


def _piece_selector(shape, axis, head):
    idx = lax.broadcasted_iota(jnp.int32, shape, axis)
    hit = jnp.where(idx < N_C_PIECES * N_FOX_HEADS, jnp.where((idx & (N_FOX_HEADS - 1)) == head, 1.0, 0.0), 0.0)
    return hit


_ROW_M, _ROW_MX, _ROW_ALPHA = 0, 1, 2


def _fox_prompt_kernel(first_ref, qt_ref, kb_ref, vt_ref, c_ref, o_ref, qa_ref, s_ref, p_ref, acc_ref,
                       stat_ref, *, t, tq, blk):
    pair = pl.program_id(1)
    qi = pl.program_id(2)
    diag_blocks = tq // blk
    d = FOX_HEAD_DIM
    heads = range(HEADS_PER_LANE_BLOCK)

    qt = qt_ref[0]
    row = lax.broadcasted_iota(jnp.int32, (LANES, tq), 0)
    for h in heads:
        in_head = _indicator(row, h * d, (h + 1) * d).astype(BF16)
        sel = _piece_selector((LANES, tq), 0, pair * HEADS_PER_LANE_BLOCK + h).astype(BF16)
        qa_ref[h] = jnp.concatenate([qt * in_head, sel], axis=0)

    acc_ref[...] = jnp.zeros_like(acc_ref)
    stat_ref[...] = jnp.full(stat_ref.shape, MASK_VALUE, F32)
    krow = lax.broadcasted_iota(jnp.int32, (blk, tq), 0)
    qcol = lax.broadcasted_iota(jnp.int32, (blk, tq), 1)

    def stat(h, slot, kind):
        r = (h * 2 + slot) * 4 + kind
        return slice(r, r + 1)

    def stage_a(j, slot, diag):
        off = pl.multiple_of(j * blk, blk)
        kb = jnp.concatenate([kb_ref[0, pl.ds(off, blk), :], c_ref[0, pl.ds(off, blk), :]], axis=1)
        for h in heads:
            s = _dot(kb, qa_ref[h])
            if diag is not None:
                s = jnp.where(krow + diag * blk <= qcol, s, MASK_VALUE)
            s_ref[slot, h] = s
            stat_ref[stat(h, slot, _ROW_MX), :] = jnp.max(s, axis=0, keepdims=True)

    def stage_b(slot):
        for h in heads:
            m_old = stat_ref[stat(h, 0, _ROW_M), :]
            m_new = jnp.maximum(m_old, stat_ref[stat(h, slot, _ROW_MX), :])
            stat_ref[stat(h, slot, _ROW_ALPHA), :] = jnp.exp2(m_old - m_new)
            stat_ref[stat(h, 0, _ROW_M), :] = m_new
            p_ref[slot, h] = jnp.exp2(s_ref[slot, h] - m_new).astype(BF16)

    ones = jnp.ones((ONES_ROWS, blk), BF16)

    def stage_c(j, slot):
        off = pl.multiple_of(j * blk, blk)
        for h in heads:
            alpha = stat_ref[stat(h, slot, _ROW_ALPHA), :]
            va = jnp.concatenate([vt_ref[0, h * d:(h + 1) * d, pl.ds(off, blk)].astype(BF16), ones], axis=0)
            acc_ref[h] = acc_ref[h] * alpha + _dot(va, p_ref[slot, h])

    nfull = qi * diag_blocks
    j0 = first_ref[(pl.program_id(0) * N_HEAD_PAIRS + pair) * (t // tq) + qi]

    @pl.when(qi > 0)
    def _pipelined():
        stage_a(j0, 0, None)
        stage_b(0)
        stage_a(j0 + 1, 1, None)

        def two_steps(j):
            stage_c(j - 2, 0)
            stage_b(1)
            stage_a(j, 0, None)
            stage_c(j - 1, 1)
            stage_b(0)
            stage_a(j + 1, 1, None)

        def body(u, carry):
            two_steps(j0 + 2 + 4 * u)
            two_steps(j0 + 4 + 4 * u)
            return carry

        pairs_left = (nfull - j0) // 2 - 1
        lax.fori_loop(0, pairs_left // 2, body, 0)

        @pl.when(pairs_left % 2 == 1)
        def _odd_group():
            two_steps(j0 + 2 * pairs_left)
        for dg in range(diag_blocks):
            stage_c(nfull + dg - 2, dg % 2)
            stage_b((dg + 1) % 2)
            stage_a(nfull + dg, dg % 2, dg)
        stage_c(nfull + diag_blocks - 2, 0)
        stage_b(1)
        stage_c(nfull + diag_blocks - 1, 1)

    @pl.when(qi == 0)
    def _first_block():
        for dg in range(diag_blocks):
            stage_a(dg, dg % 2, dg)
            stage_b(dg % 2)
            stage_c(dg, dg % 2)

    for h in heads:
        a = acc_ref[h]
        o_ref[0, h * d:(h + 1) * d, :] = (a[0:d] / a[d:d + 1]).astype(BF16)


def _first_key_block(c_end, q_norm, k_norm, t, tq, blk):
    b = c_end.shape[0]
    nblk, nq, per = t // blk, t // tq, tq // blk
    ce = c_end[:, :, :CUMSUM_TILE // blk, :N_FOX_HEADS].reshape(b, nblk, N_FOX_HEADS)
    qn = q_norm[:, :, :, 0]
    kmax = jnp.max(k_norm[:, :, :, 0], axis=1)
    before = jnp.concatenate([jnp.zeros((b, 1, N_FOX_HEADS), F32), ce[:, per - 1:-1:per]], axis=1)
    decay = before[:, :, None, :] - ce[:, None, :, :]
    bound = (2.0 * NORM_SLACK * NORM_SLACK) * (qn * kmax[:, None, :])[:, :, None, :] + decay
    below = (jnp.arange(nblk)[None, :] < (jnp.arange(nq) * per)[:, None])[None, :, :, None]
    skippable = jnp.logical_and(bound <= -SKIP_EXPONENT, below)
    block = jnp.arange(nblk, dtype=jnp.int32)[None, None, :, None]
    n_skip = jnp.min(jnp.where(skippable, nblk, block), axis=2)
    n_skip = jnp.min(n_skip.reshape(b, nq, N_HEAD_PAIRS, HEADS_PER_LANE_BLOCK), axis=3)
    first = jnp.clip((n_skip // 2) * 2, 0, jnp.maximum(jnp.arange(nq) * per - 2, 0)[None, :, None])
    return jnp.transpose(first, (0, 2, 1)).reshape(-1).astype(jnp.int32)


def _fox_prompt(first, qt, kb, vt, c):
    b, _, t = qt.shape
    blk = FOX_BLOCK
    tq = FOX_Q_BLOCK
    assert tq == 2 * blk, "the pipeline keeps two key blocks in flight and slot = block parity"
    hp = HEADS_PER_LANE_BLOCK
    acc_rows = FOX_HEAD_DIM + ONES_ROWS
    per_pair_all_time = lambda i, p, j, first: (i, p, 0)
    grid_spec = pltpu.PrefetchScalarGridSpec(
        num_scalar_prefetch=1,
        grid=(b, N_HEAD_PAIRS, t // tq),
        in_specs=[pl.BlockSpec((1, LANES, tq), lambda i, p, j, first: (i, p, j)),
                  pl.BlockSpec((1, t, LANES), lambda i, p, j, first: (i, 0, p)),
                  pl.BlockSpec((1, LANES, t), per_pair_all_time),
                  pl.BlockSpec((1, t, LANES), lambda i, p, j, first: (i, 0, 0))],
        out_specs=pl.BlockSpec((1, LANES, tq), lambda i, p, j, first: (i, p, j)),
        scratch_shapes=[pltpu.VMEM((hp, 2 * LANES, tq), BF16),
                        pltpu.VMEM((2, hp, blk, tq), F32),
                        pltpu.VMEM((2, hp, blk, tq), BF16),
                        pltpu.VMEM((hp, acc_rows, tq), F32),
                        pltpu.VMEM((hp * 2 * 4, tq), F32)])
    return pl.pallas_call(
        functools.partial(_fox_prompt_kernel, t=t, tq=tq, blk=blk),
        grid_spec=grid_spec,
        out_shape=jax.ShapeDtypeStruct((b, FOX_WIDTH, t), BF16),
        compiler_params=_cparams(3), name="fox_prompt",
    )(first, qt, kb, vt, c)


def _fox_sample_kernel(q_ref, kn_ref, vn_ref, kc_ref, vc_ref, c_ref, o_ref, *, past, new, pairs):
    group = pl.program_id(1)
    d = FOX_HEAD_DIM
    rows = HEADS_PER_LANE_BLOCK * new
    lane = lax.broadcasted_iota(jnp.int32, (rows, LANES), 1)
    qrow = lax.broadcasted_iota(jnp.int32, (rows, LANES), 0)
    head_lanes = jnp.where(qrow < new, _indicator(lane, 0, d), _indicator(lane, d, 2 * d)).astype(BF16)
    out_lane = lax.broadcasted_iota(jnp.int32, (new, LANES), 1)
    nrow = lax.broadcasted_iota(jnp.int32, (rows, new), 0)
    ncol = lax.broadcasted_iota(jnp.int32, (rows, new), 1)
    causal = ncol <= (nrow & (new - 1))
    for pp in range(pairs):
        lanes = slice(pp * LANES, (pp + 1) * LANES)
        head0 = (group * pairs + pp) * HEADS_PER_LANE_BLOCK
        tpad = c_ref.shape[2]
        bias = jnp.concatenate(
            [jnp.broadcast_to(c_ref[0, pl.ds(head0 + h, 1), :], (new, tpad)) for h in range(HEADS_PER_LANE_BLOCK)],
            axis=0)
        q = q_ref[0, :, lanes]
        q2 = jnp.concatenate([q, q], axis=0) * head_lanes
        s_c = _dot(q2, kc_ref[0, lanes, :].astype(BF16)) - bias[:, 0:past]
        s_n = _dot_nt(q2, kn_ref[0, :, lanes].astype(BF16)) - bias[:, past:past + new]
        s_n = jnp.where(causal, s_n, MASK_VALUE)
        m = jnp.maximum(jnp.max(s_c, axis=-1, keepdims=True), jnp.max(s_n, axis=-1, keepdims=True))
        e_c = jnp.exp(s_c - m)
        e_n = jnp.exp(s_n - m)
        l = jnp.sum(e_c, axis=-1, keepdims=True) + jnp.sum(e_n, axis=-1, keepdims=True)
        o2 = _dot_nt(e_c.astype(BF16), vc_ref[0, lanes, :].astype(BF16))
        o2 = (o2 + _dot(e_n.astype(BF16), vn_ref[0, :, lanes].astype(BF16))) / l
        o = jnp.where(out_lane < d, o2[0:new], o2[new:rows])
        o_ref[0, :, lanes] = o.astype(BF16)


def _fox_sample(q, kn, vn, kct, vct, c):
    b, new, _ = q.shape
    past = kct.shape[2]
    pairs = SAMPLE_PAIRS_PER_STEP
    width = pairs * LANES
    tok = lambda i, g: (i, 0, g)
    feat = lambda i, g: (i, g, 0)
    return pl.pallas_call(
        functools.partial(_fox_sample_kernel, past=past, new=new, pairs=pairs),
        grid=(b, N_HEAD_PAIRS // pairs),
        in_specs=[pl.BlockSpec((1, new, width), tok), pl.BlockSpec((1, new, width), tok),
                  pl.BlockSpec((1, new, width), tok), pl.BlockSpec((1, width, past), feat),
                  pl.BlockSpec((1, width, past), feat),
                  pl.BlockSpec((1,) + c.shape[1:], lambda i, g: (i, 0, 0))],
        out_specs=pl.BlockSpec((1, new, width), tok),
        out_shape=jax.ShapeDtypeStruct((b, new, FOX_WIDTH), BF16),
        compiler_params=_cparams(2), name="fox_sample",
    )(q, kn, vn, kct, vct, c)


def _retention_kernel(lg_ref, q_ref, k_ref, v_ref, g_ref, gn_ref, s0_ref, o_ref, st_ref,
                      decay_ref, qdec_ref, kdec_ref, *, blk, heads):
    @pl.when(pl.program_id(2) == 0)
    def _():
        n = lax.broadcasted_iota(jnp.int32, (blk, blk), 0)
        m = lax.broadcasted_iota(jnp.int32, (blk, blk), 1)
        shift = CHUNK.bit_length() - 1
        dist = jnp.abs(n - m).astype(F32)
        same_or_earlier_chunk = (m >> shift) <= (n >> shift)
        pos = lax.broadcasted_iota(jnp.int32, (blk, RET_KEY_DIM), 0).astype(F32)
        for h in range(heads):
            lg = lg_ref[h, 0:1, 0:1]
            st_ref[0, h] = s0_ref[0, h]
            decay_ref[h] = jnp.where(same_or_earlier_chunk, jnp.exp(lg * dist), 0.0)
            qdec_ref[h] = jnp.exp(lg * (pos + 1.0))
            kdec_ref[h] = jnp.exp(lg * (blk - 1.0 - pos))

    for h in range(heads):
        lg = lg_ref[h, 0:1, 0:1]
        keys = slice(h * RET_KEY_DIM, (h + 1) * RET_KEY_DIM)
        vals = slice(h * RET_VAL_DIM, (h + 1) * RET_VAL_DIM)
        q = q_ref[0, :, keys]
        k = k_ref[0, :, keys]
        v = v_ref[0, :, vals]
        state = st_ref[0, h]
        s = _dot_nt(q, k) * decay_ref[h]
        qd = (q.astype(F32) * qdec_ref[h]).astype(BF16)
        y = _dot(s.astype(BF16), v) + _dot(qd, state.astype(BF16))
        kd = k.astype(F32) * kdec_ref[h]
        st_ref[0, h] = jnp.exp(lg * blk) * state + _dot(kd.T.astype(BF16), v)

        yn = y * lax.rsqrt(jnp.mean(y * y, axis=-1, keepdims=True) + EPS) * gn_ref[h]
        g = g_ref[0, :, vals]
        o_ref[0, :, vals] = (g * jax.nn.sigmoid(g) * yn).astype(BF16)


def _retention(lg, q, k, v, g, gn, s0, blk):
    b, t, _ = q.shape
    heads = RET_HEADS_PER_STEP
    qk = lambda i, h, j: (i, j, h)
    per_group = lambda i, h, j: (h, 0, 0)
    group_state = lambda i, h, j: (i, h, 0, 0)
    return pl.pallas_call(
        functools.partial(_retention_kernel, blk=blk, heads=heads),
        grid=(b, N_RET_HEADS // heads, t // blk),
        in_specs=[pl.BlockSpec((heads, 8, LANES), per_group),
                  pl.BlockSpec((1, blk, heads * RET_KEY_DIM), qk), pl.BlockSpec((1, blk, heads * RET_KEY_DIM), qk),
                  pl.BlockSpec((1, blk, heads * RET_VAL_DIM), qk), pl.BlockSpec((1, blk, heads * RET_VAL_DIM), qk),
                  pl.BlockSpec((heads, 1, RET_VAL_DIM), per_group),
                  pl.BlockSpec((1, heads, RET_KEY_DIM, RET_VAL_DIM), group_state)],
        out_specs=[pl.BlockSpec((1, blk, heads * RET_VAL_DIM), qk),
                   pl.BlockSpec((1, heads, RET_KEY_DIM, RET_VAL_DIM), group_state)],
        out_shape=[jax.ShapeDtypeStruct((b, t, RET_V_WIDTH), BF16),
                   jax.ShapeDtypeStruct((b, N_RET_HEADS, RET_KEY_DIM, RET_VAL_DIM), F32)],
        scratch_shapes=[pltpu.VMEM((heads, blk, blk), F32), pltpu.VMEM((heads, blk, RET_KEY_DIM), F32),
                        pltpu.VMEM((heads, blk, RET_KEY_DIM), F32)],
        compiler_params=_cparams(3), name="retention",
    )(lg, q, k, v, g, gn, s0)


def _mixer_kernel(x_ref, oa_ref, ob_ref, g_ref, wg_ref, wa_ref, wb_ref, wo_ref, x1_ref, *, oa_feature_major):
    x = x_ref[...]
    h = _rmsnorm(x, g_ref[...]).astype(BF16)
    gate_a = jax.nn.sigmoid(_dot(h, wg_ref[:, 0:D_MODEL]))
    gate_b = jax.nn.sigmoid(_dot(h, wg_ref[:, D_MODEL:2 * D_MODEL]))
    if oa_feature_major:
        ya = lax.dot_general(oa_ref[0], wa_ref[...], (((0,), (0,)), ((), ())), preferred_element_type=F32)
    else:
        ya = _dot(oa_ref[...], wa_ref[...])
    merged = gate_a * ya + gate_b * _dot(ob_ref[...], wb_ref[...])
    x1_ref[...] = x + _dot(merged.astype(BF16), wo_ref[...])


def _mixer(x, oa, ob, g, wg, wa, wb, wo):
    n = x.shape[0]
    tm = MIX_TOKEN_TILE
    tok = lambda i: (i, 0)
    fixed = lambda i: (0, 0)
    feature_major = oa.ndim == 3
    if feature_major:
        per_batch = oa.shape[2] // tm
        oa_spec = pl.BlockSpec((1, FOX_WIDTH, tm), lambda i: (i // per_batch, 0, i % per_batch))
    else:
        oa_spec = pl.BlockSpec((tm, FOX_WIDTH), tok)
    return pl.pallas_call(
        functools.partial(_mixer_kernel, oa_feature_major=feature_major),
        grid=(n // tm,),
        in_specs=[pl.BlockSpec((tm, D_MODEL), tok), oa_spec,
                  pl.BlockSpec((tm, RET_V_WIDTH), tok), pl.BlockSpec((1, D_MODEL), fixed),
                  _resident((D_MODEL, 2 * D_MODEL), fixed), _resident((FOX_WIDTH, D_MODEL), fixed),
                  _resident((RET_V_WIDTH, D_MODEL), fixed), _resident((D_MODEL, D_MODEL), fixed)],
        out_specs=pl.BlockSpec((tm, D_MODEL), tok),
        out_shape=jax.ShapeDtypeStruct((n, D_MODEL), F32),
        compiler_params=_cparams(1), name="mixer",
    )(x, oa, ob, g, wg, wa, wb, wo)


def _mlp_kernel(x_ref, g_ref, wu_ref, wd_ref, gf_ref, y_ref):
    x = x_ref[...]
    h = _rmsnorm(x, g_ref[...]).astype(BF16)
    acc = x
    for c in range(D_FF // D_MODEL):
        cols = slice(c * D_MODEL, (c + 1) * D_MODEL)
        u = jnp.square(jnp.maximum(_dot(h, wu_ref[:, cols]), 0.0)).astype(BF16)
        acc = acc + _dot(u, wd_ref[cols, :])
    y_ref[...] = _rmsnorm(acc, gf_ref[...])


def _mlp(x, g, wu, wd, gf):
    n = x.shape[0]
    tm = MIX_TOKEN_TILE
    tok = lambda i: (i, 0)
    fixed = lambda i: (0, 0)
    return pl.pallas_call(
        _mlp_kernel,
        grid=(n // tm,),
        in_specs=[pl.BlockSpec((tm, D_MODEL), tok), pl.BlockSpec((1, D_MODEL), fixed),
                  _resident((D_MODEL, D_FF), fixed), _resident((D_FF, D_MODEL), fixed),
                  pl.BlockSpec((1, D_MODEL), fixed)],
        out_specs=pl.BlockSpec((tm, D_MODEL), tok),
        out_shape=jax.ShapeDtypeStruct((n, D_MODEL), F32),
        compiler_params=_cparams(1), name="mlp",
    )(x, g, wu, wd, gf)


def _rope_tables(pos):
    inv_freq = ROPE_BASE ** (-jnp.arange(0, RET_KEY_DIM, 2, dtype=F32) / RET_KEY_DIM)
    ang = pos.astype(F32)[:, None] * inv_freq[None, :]
    return jnp.cos(ang), jnp.sin(ang)


def _after_attention(x, pos, o_a, params, ret_blk, state0):
    b, t, _ = x.shape
    n = b * t
    xf = x.reshape(n, D_MODEL)
    cos, sin = _rope_tables(pos)
    reps = max(1, TOKEN_TILE // t)
    q_r, k_r = _proj_ret_qk(xf, params["g_attn"], params["w_ret_qk"], jnp.tile(cos, (reps, 1)),
                            jnp.tile(sin, (reps, 1)))
    v_r, g_r = _proj_ret_vg(xf, params["g_attn"], params["w_ret_vg"])
    o_b, state = _retention(params["lg"], q_r.reshape(b, t, RET_QK_WIDTH), k_r.reshape(b, t, RET_QK_WIDTH),
                            v_r.reshape(b, t, RET_V_WIDTH), g_r.reshape(b, t, RET_V_WIDTH),
                            params["g_ret_norm"], state0, ret_blk)
    x1 = _mixer(xf, o_a, o_b.reshape(n, RET_V_WIDTH), params["g_attn"],
                params["w_gates"], params["w_branch_a"], params["w_branch_b"], params["w_out"])
    y = _mlp(x1, params["g_mlp"], params["w_up"], params["w_down"], params["g_final"])
    return y.reshape(b, t, D_MODEL), state[None]


def kernel(x_prompt, x_sample, cache_fox_k, cache_fox_v, cache_fox_logf, state_ret, g_attn, w_in, b_forget,
           g_ret_norm, w_branch, w_out, g_mlp, w_up, w_down, g_final):
    assert w_in.shape[0] == 1, "single-layer trunk"
    wi = w_in[0]
    fox_cols = 3 * FOX_WIDTH + N_FOX_HEADS
    qk_end = fox_cols + 2 * RET_QK_WIDTH
    vg_end = qk_end + 2 * RET_V_WIDTH
    lane_pad = LANES - N_FOX_HEADS
    lg = jnp.log(1.0 - 2.0 ** (-5.0 - jnp.arange(N_RET_HEADS, dtype=F32)))
    w_fox = jnp.pad(wi[:, :fox_cols], ((0, 0), (0, lane_pad))).astype(BF16)
    b_row = jnp.pad(b_forget[0], (0, lane_pad)).reshape(1, LANES)
    params = {
        "g_attn": g_attn[0].reshape(1, D_MODEL),
        "w_ret_qk": wi[:, fox_cols:qk_end].astype(BF16),
        "w_ret_vg": wi[:, qk_end:vg_end].astype(BF16),
        "w_gates": wi[:, vg_end:].astype(BF16),
        "lg": jnp.broadcast_to(lg[:, None, None], (N_RET_HEADS, 8, LANES)),
        "g_ret_norm": g_ret_norm[0].reshape(N_RET_HEADS, 1, RET_VAL_DIM),
        "w_branch_a": w_branch[0, :FOX_WIDTH].astype(BF16),
        "w_branch_b": w_branch[0, FOX_WIDTH:].astype(BF16),
        "w_out": w_out[0].astype(BF16),
        "g_mlp": g_mlp[0].reshape(1, D_MODEL),
        "w_up": w_up[0].astype(BF16),
        "w_down": w_down[0].astype(BF16),
        "g_final": g_final.reshape(1, D_MODEL),
    }
    bp, tp, _ = x_prompt.shape
    bs, ts, _ = x_sample.shape
    past = cache_fox_k.shape[2]

    qt, kt, vt, lft, lf_pad, q_norm, k_norm, kb = _proj_fox_t(
        x_prompt, params["g_attn"], wi[:, :fox_cols].T.astype(BF16), w_fox[:, 3 * FOX_WIDTH:],
        jnp.broadcast_to(b_forget[0][:, None], (N_FOX_HEADS, LANES)), b_row)
    c_pieces, c_end = _cumsum_pieces(lf_pad, CUMSUM_TILE, LOG2E)
    first = _first_key_block(c_end, q_norm, k_norm, tp, FOX_Q_BLOCK, FOX_BLOCK)
    o_a = _fox_prompt(first, qt, kb, vt, c_pieces)
    zero_state = jnp.zeros((bp, N_RET_HEADS, RET_KEY_DIM, RET_VAL_DIM), F32)
    yp, sp = _after_attention(x_prompt, jnp.arange(tp), o_a, params, RET_BLOCK, zero_state)
    to_heads = lambda a: jnp.transpose(a.reshape(bp, N_FOX_HEADS, FOX_HEAD_DIM, tp), (0, 3, 1, 2))[None]
    kp, vp = to_heads(kt), to_heads(vt)
    fp = jnp.transpose(lft, (0, 2, 1))[None]

    q, k, v, lf = _proj_fox(x_sample.reshape(bs * ts, D_MODEL), params["g_attn"], w_fox, b_row)
    feature_major = lambda a: jnp.transpose(a[0], (0, 2, 3, 1)).reshape(bs, FOX_WIDTH, past)
    lf_time = jnp.concatenate([jnp.transpose(cache_fox_logf[0], (0, 2, 1)),
                               jnp.transpose(lf.reshape(bs, ts, N_FOX_HEADS), (0, 2, 1))], axis=2)
    lf_time = jnp.pad(lf_time, ((0, 0), (0, 0), (0, -(past + ts) % LANES)))
    o_a = _fox_sample(q.reshape(bs, ts, FOX_WIDTH), k.reshape(bs, ts, FOX_WIDTH), v.reshape(bs, ts, FOX_WIDTH),
                      feature_major(cache_fox_k), feature_major(cache_fox_v), _cumsum_lanes(lf_time))
    ys, ss = _after_attention(x_sample, past + jnp.arange(ts), o_a.reshape(bs * ts, FOX_WIDTH), params, CHUNK,
                              state_ret[0])
    shape5 = (1, bs, ts, N_FOX_HEADS, FOX_HEAD_DIM)
    return (yp, ys, kp, vp, fp, sp, k.reshape(shape5), v.reshape(shape5),
            lf.reshape(1, bs, ts, N_FOX_HEADS), ss)
```

```python
import functools

import jax
import jax.numpy as jnp
from jax import lax
from jax.experimental import pallas as pl
from jax.experimental.pallas import tpu as pltpu

D_MODEL = 1024
N_FOX_HEADS = 16
FOX_HEAD_DIM = 64
FOX_WIDTH = N_FOX_HEADS * FOX_HEAD_DIM
N_RET_HEADS = 4
RET_KEY_DIM = 256
RET_VAL_DIM = 512
RET_QK_WIDTH = N_RET_HEADS * RET_KEY_DIM
RET_V_WIDTH = N_RET_HEADS * RET_VAL_DIM
D_FF = 4 * D_MODEL
CHUNK = 64
ROPE_BASE = 10000.0
EPS = 1e-6

LANES = 128
HEADS_PER_LANE_BLOCK = LANES // FOX_HEAD_DIM
N_HEAD_PAIRS = N_FOX_HEADS // HEADS_PER_LANE_BLOCK
N_C_PIECES = 3
ONES_ROWS = 16
MASK_VALUE = -1e30
LOG2E = 1.4426950408889634
VMEM_LIMIT = 56 * 1024 * 1024

TOKEN_TILE = 512
MIX_TOKEN_TILE = 512
FOX_BLOCK = 256
FOX_Q_BLOCK = 512
SAMPLE_PAIRS_PER_STEP = 4
RET_BLOCK = 256
RET_HEADS_PER_STEP = 4
CUMSUM_TILE = 512
SKIP_EXPONENT = 160.0
NORM_SLACK = 1.01

BF16 = jnp.bfloat16
F32 = jnp.float32


def _cparams(n_axes):
    return pltpu.CompilerParams(dimension_semantics=("arbitrary",) * n_axes,
                                vmem_limit_bytes=VMEM_LIMIT)


def _resident(shape, index_map):
    return pl.BlockSpec(shape, index_map, pipeline_mode=pl.Buffered(1))


def _dot(a, b):
    return jnp.dot(a, b, preferred_element_type=F32)


def _dot_nt(a, b):
    return lax.dot_general(a, b, (((1,), (1,)), ((), ())), preferred_element_type=F32)


def _rmsnorm(x, g):
    return x * lax.rsqrt(jnp.mean(x * x, axis=-1, keepdims=True) + EPS) * g


def _log_sigmoid(z):
    return -(jnp.maximum(-z, 0.0) + jnp.log(1.0 + jnp.exp(-jnp.abs(z))))


def _split3(x):
    a = x.astype(BF16)
    r = x - a.astype(F32)
    b = r.astype(BF16)
    c = (r - b.astype(F32)).astype(BF16)
    return a, b, c


def _indicator(idx, lo, hi):
    return jnp.where(idx >= lo, jnp.where(idx < hi, 1.0, 0.0), 0.0)


def _proj_fox_t_kernel(x_ref, g_ref, wt_ref, wf_ref, bcol_ref, brow_ref, qt_ref, kt_ref, vt_ref, lft_ref, lfp_ref,
                       qn_ref, kn_ref, kb_ref):
    h = _rmsnorm(x_ref[0], g_ref[...])
    ht = h.T.astype(BF16)
    w = FOX_WIDTH
    qs = _dot(wt_ref[0:w, :], ht) * (FOX_HEAD_DIM ** -0.5 * LOG2E)
    kf = _dot(wt_ref[w:2 * w, :], ht)
    qt_ref[0] = qs.astype(BF16)
    kt_ref[0] = kf
    kb_ref[0] = kf.T.astype(BF16)
    for src, dst in ((qs, qn_ref), (kf, kn_ref)):
        sq = jnp.sum((src * src).reshape(N_FOX_HEADS, FOX_HEAD_DIM, src.shape[1]), axis=1)
        dst[0, 0] = jnp.broadcast_to(jnp.sqrt(jnp.max(sq, axis=1, keepdims=True)), (N_FOX_HEADS, LANES))
    vt_ref[0] = _dot(wt_ref[2 * w:3 * w, :], ht)
    lft_ref[0] = _log_sigmoid(_dot(wt_ref[3 * w:3 * w + N_FOX_HEADS, :], ht) + bcol_ref[:, 0:1])
    lfp_ref[0] = _log_sigmoid(_dot(h.astype(BF16), wf_ref[...]) + brow_ref[...])


def _proj_fox_t(x, g, wt, wf, bcol, brow):
    b, t, _ = x.shape
    tm = TOKEN_TILE
    fixed = lambda i, j: (0, 0)
    feat = lambda i, j: (i, 0, j)
    return pl.pallas_call(
        _proj_fox_t_kernel,
        grid=(b, t // tm),
        in_specs=[pl.BlockSpec((1, tm, D_MODEL), lambda i, j: (i, j, 0)), pl.BlockSpec((1, D_MODEL), fixed),
                  pl.BlockSpec(wt.shape, fixed), pl.BlockSpec(wf.shape, fixed),
                  pl.BlockSpec(bcol.shape, fixed), pl.BlockSpec(brow.shape, fixed)],
        out_specs=[pl.BlockSpec((1, FOX_WIDTH, tm), feat), pl.BlockSpec((1, FOX_WIDTH, tm), feat),
                   pl.BlockSpec((1, FOX_WIDTH, tm), feat), pl.BlockSpec((1, N_FOX_HEADS, tm), feat),
                   pl.BlockSpec((1, tm, LANES), lambda i, j: (i, j, 0)),
                   pl.BlockSpec((1, 1, N_FOX_HEADS, LANES), lambda i, j: (i, j, 0, 0)),
                   pl.BlockSpec((1, 1, N_FOX_HEADS, LANES), lambda i, j: (i, j, 0, 0)),
                   pl.BlockSpec((1, tm, FOX_WIDTH), lambda i, j: (i, j, 0))],
        out_shape=[jax.ShapeDtypeStruct((b, FOX_WIDTH, t), BF16), jax.ShapeDtypeStruct((b, FOX_WIDTH, t), F32),
                   jax.ShapeDtypeStruct((b, FOX_WIDTH, t), F32), jax.ShapeDtypeStruct((b, N_FOX_HEADS, t), F32),
                   jax.ShapeDtypeStruct((b, t, LANES), F32),
                   jax.ShapeDtypeStruct((b, t // tm, N_FOX_HEADS, LANES), F32),
                   jax.ShapeDtypeStruct((b, t // tm, N_FOX_HEADS, LANES), F32),
                   jax.ShapeDtypeStruct((b, t, FOX_WIDTH), BF16)],
        compiler_params=_cparams(2), name="proj_fox_t",
    )(x, g, wt, wf, bcol, brow)


def _proj_fox_kernel(x_ref, g_ref, w_ref, b_ref, q_ref, k_ref, v_ref, lf_ref):
    h = _rmsnorm(x_ref[...], g_ref[...]).astype(BF16)
    w = FOX_WIDTH
    q_ref[...] = (_dot(h, w_ref[:, 0:w]) * (FOX_HEAD_DIM ** -0.5)).astype(BF16)
    k_ref[...] = _dot(h, w_ref[:, w:2 * w])
    v_ref[...] = _dot(h, w_ref[:, 2 * w:3 * w])
    lf = _log_sigmoid(_dot(h, w_ref[:, 3 * w:3 * w + LANES]) + b_ref[...])
    lf_ref[...] = lf[:, 0:N_FOX_HEADS]


def _proj_fox(x, g, w, b):
    n = x.shape[0]
    tm = TOKEN_TILE
    tok = lambda i: (i, 0)
    fixed = lambda i: (0, 0)
    return pl.pallas_call(
        _proj_fox_kernel,
        grid=(n // tm,),
        in_specs=[pl.BlockSpec((tm, D_MODEL), tok), pl.BlockSpec((1, D_MODEL), fixed),
                  pl.BlockSpec(w.shape, fixed), pl.BlockSpec((1, LANES), fixed)],
        out_specs=[pl.BlockSpec((tm, FOX_WIDTH), tok), pl.BlockSpec((tm, FOX_WIDTH), tok),
                   pl.BlockSpec((tm, FOX_WIDTH), tok), pl.BlockSpec((tm, N_FOX_HEADS), tok)],
        out_shape=[jax.ShapeDtypeStruct((n, FOX_WIDTH), BF16), jax.ShapeDtypeStruct((n, FOX_WIDTH), F32),
                   jax.ShapeDtypeStruct((n, FOX_WIDTH), F32), jax.ShapeDtypeStruct((n, N_FOX_HEADS), F32)],
        compiler_params=_cparams(1), name="proj_fox",
    )(x, g, w, b)


def _proj_ret_qk_kernel(x_ref, g_ref, w_ref, cos_ref, sin_ref, q_ref, k_ref):
    h = _rmsnorm(x_ref[...], g_ref[...]).astype(BF16)
    cos = cos_ref[...]
    sin = sin_ref[...]
    half = RET_KEY_DIM // 2
    for out_ref, base, scale in ((q_ref, 0, 1.0), (k_ref, RET_QK_WIDTH, RET_KEY_DIM ** -0.5)):
        for hd in range(N_RET_HEADS):
            lo = hd * RET_KEY_DIM
            z = _dot(h, w_ref[:, base + lo:base + lo + RET_KEY_DIM])
            x1 = z[:, :half]
            x2 = z[:, half:]
            out_ref[:, lo:lo + half] = ((x1 * cos - x2 * sin) * scale).astype(BF16)
            out_ref[:, lo + half:lo + RET_KEY_DIM] = ((x1 * sin + x2 * cos) * scale).astype(BF16)


def _proj_ret_qk(x, g, w, cos, sin):
    n = x.shape[0]
    tm = TOKEN_TILE
    period = cos.shape[0] // tm
    tok = lambda i: (i, 0)
    fixed = lambda i: (0, 0)
    pos = lambda i: (i % period, 0)
    half = RET_KEY_DIM // 2
    return pl.pallas_call(
        _proj_ret_qk_kernel,
        grid=(n // tm,),
        in_specs=[pl.BlockSpec((tm, D_MODEL), tok), pl.BlockSpec((1, D_MODEL), fixed),
                  pl.BlockSpec((D_MODEL, 2 * RET_QK_WIDTH), fixed),
                  pl.BlockSpec((tm, half), pos), pl.BlockSpec((tm, half), pos)],
        out_specs=[pl.BlockSpec((tm, RET_QK_WIDTH), tok), pl.BlockSpec((tm, RET_QK_WIDTH), tok)],
        out_shape=[jax.ShapeDtypeStruct((n, RET_QK_WIDTH), BF16)] * 2,
        compiler_params=_cparams(1), name="proj_ret_qk",
    )(x, g, w, cos, sin)


def _proj_ret_vg_kernel(x_ref, g_ref, w_ref, v_ref, gr_ref):
    h = _rmsnorm(x_ref[...], g_ref[...]).astype(BF16)
    v_ref[...] = _dot(h, w_ref[:, 0:RET_V_WIDTH]).astype(BF16)
    gr_ref[...] = _dot(h, w_ref[:, RET_V_WIDTH:2 * RET_V_WIDTH])


def _proj_ret_vg(x, g, w):
    n = x.shape[0]
    tm = TOKEN_TILE
    tok = lambda i: (i, 0)
    fixed = lambda i: (0, 0)
    return pl.pallas_call(
        _proj_ret_vg_kernel,
        grid=(n // tm,),
        in_specs=[pl.BlockSpec((tm, D_MODEL), tok), pl.BlockSpec((1, D_MODEL), fixed),
                  pl.BlockSpec((D_MODEL, 2 * RET_V_WIDTH), fixed)],
        out_specs=[pl.BlockSpec((tm, RET_V_WIDTH), tok), pl.BlockSpec((tm, RET_V_WIDTH), tok)],
        out_shape=[jax.ShapeDtypeStruct((n, RET_V_WIDTH), BF16), jax.ShapeDtypeStruct((n, RET_V_WIDTH), F32)],
        compiler_params=_cparams(1), name="proj_ret_vg",
    )(x, g, w)


def _cumsum_kernel(lf_ref, c_ref, cend_ref, carry_ref, *, tb, scale):
    @pl.when(pl.program_id(1) == 0)
    def _():
        carry_ref[...] = jnp.zeros_like(carry_ref)

    row = lax.broadcasted_iota(jnp.int32, (tb, tb), 0)
    col = lax.broadcasted_iota(jnp.int32, (tb, tb), 1)
    tri = jnp.where(col <= row, 1.0, 0.0).astype(BF16)
    c = carry_ref[0:1, :]
    for piece in _split3(lf_ref[0]):
        c = c + _dot(tri, piece)
    carry_ref[...] = jnp.broadcast_to(c[tb - 1:tb, :], carry_ref.shape)
    erow = lax.broadcasted_iota(jnp.int32, (8, LANES), 0)
    ends = jnp.zeros((8, LANES), F32)
    for e in range(tb // FOX_BLOCK):
        ends = jnp.where(erow == e, c[(e + 1) * FOX_BLOCK - 1:(e + 1) * FOX_BLOCK, :] * scale, ends)
    cend_ref[0, 0] = ends

    prow = lax.broadcasted_iota(jnp.int32, (LANES, LANES), 0)
    pcol = lax.broadcasted_iota(jnp.int32, (LANES, LANES), 1)
    out = jnp.zeros((tb, LANES), F32)
    for p, piece in enumerate(_split3(-c * scale)):
        place = jnp.where(pcol == prow + p * N_FOX_HEADS, jnp.where(prow < N_FOX_HEADS, 1.0, 0.0), 0.0)
        out = out + _dot(piece, place.astype(BF16))
    c_ref[0] = out.astype(BF16)


def _cumsum_pieces(lf, tb, scale):
    b, t, _ = lf.shape
    blk = lambda i, j: (i, j, 0)
    return pl.pallas_call(
        functools.partial(_cumsum_kernel, tb=tb, scale=scale),
        grid=(b, t // tb),
        in_specs=[pl.BlockSpec((1, tb, LANES), blk)],
        out_specs=[pl.BlockSpec((1, tb, LANES), blk), pl.BlockSpec((1, 1, 8, LANES), lambda i, j: (i, j, 0, 0))],
        out_shape=[jax.ShapeDtypeStruct((b, t, LANES), BF16), jax.ShapeDtypeStruct((b, t // tb, 8, LANES), F32)],
        scratch_shapes=[pltpu.VMEM((8, LANES), F32)],
        compiler_params=_cparams(2), name="cumsum",
    )(lf)


def _cumsum_lanes_kernel(lf_ref, c_ref):
    heads, t = lf_ref.shape[1], lf_ref.shape[2]
    row = lax.broadcasted_iota(jnp.int32, (LANES, LANES), 0)
    col = lax.broadcasted_iota(jnp.int32, (LANES, LANES), 1)
    tri = jnp.where(row <= col, 1.0, 0.0).astype(BF16)
    ones = jnp.ones((LANES, LANES), BF16)
    carry = jnp.zeros((heads, LANES), F32)
    for seg in range(t // LANES):
        lanes = slice(seg * LANES, (seg + 1) * LANES)
        stack = jnp.concatenate(_split3(lf_ref[0, :, lanes]), axis=0)
        within = _dot(stack, tri)
        total = _dot(stack, ones)
        fold = lambda a: a[0:heads] + a[heads:2 * heads] + a[2 * heads:3 * heads]
        c_ref[0, :, lanes] = fold(within) + carry
        carry = carry + fold(total)


def _cumsum_lanes(lf):
    b, heads, t = lf.shape
    blk = lambda i: (i, 0, 0)
    return pl.pallas_call(
        _cumsum_lanes_kernel,
        grid=(b,),
        in_specs=[pl.BlockSpec((1, heads, t), blk)],
        out_specs=pl.BlockSpec((1, heads, t), blk),
        out_shape=jax.ShapeDtypeStruct((b, heads, t), F32),
        compiler_params=_cparams(1), name="cumsum_lanes",
    )(lf)


def _piece_selector(shape, axis, head):
    idx = lax.broadcasted_iota(jnp.int32, shape, axis)
    hit = jnp.where(idx < N_C_PIECES * N_FOX_HEADS, jnp.where((idx & (N_FOX_HEADS - 1)) == head, 1.0, 0.0), 0.0)
    return hit


_ROW_M, _ROW_MX, _ROW_ALPHA = 0, 1, 2


def _fox_prompt_kernel(first_ref, qt_ref, kb_ref, vt_ref, c_ref, o_ref, qa_ref, s_ref, p_ref, acc_ref,
                       stat_ref, *, t, tq, blk):
    pair = pl.program_id(1)
    qi = pl.program_id(2)
    diag_blocks = tq // blk
    d = FOX_HEAD_DIM
    heads = range(HEADS_PER_LANE_BLOCK)

    qt = qt_ref[0]
    row = lax.broadcasted_iota(jnp.int32, (LANES, tq), 0)
    for h in heads:
        in_head = _indicator(row, h * d, (h + 1) * d).astype(BF16)
        sel = _piece_selector((LANES, tq), 0, pair * HEADS_PER_LANE_BLOCK + h).astype(BF16)
        qa_ref[h] = jnp.concatenate([qt * in_head, sel], axis=0)

    acc_ref[...] = jnp.zeros_like(acc_ref)
    stat_ref[...] = jnp.full(stat_ref.shape, MASK_VALUE, F32)
    krow = lax.broadcasted_iota(jnp.int32, (blk, tq), 0)
    qcol = lax.broadcasted_iota(jnp.int32, (blk, tq), 1)

    def stat(h, slot, kind):
        r = (h * 2 + slot) * 4 + kind
        return slice(r, r + 1)

    def stage_a(j, slot, diag):
        off = pl.multiple_of(j * blk, blk)
        kb = jnp.concatenate([kb_ref[0, pl.ds(off, blk), :], c_ref[0, pl.ds(off, blk), :]], axis=1)
        for h in heads:
            s = _dot(kb, qa_ref[h])
            if diag is not None:
                s = jnp.where(krow + diag * blk <= qcol, s, MASK_VALUE)
            s_ref[slot, h] = s
            stat_ref[stat(h, slot, _ROW_MX), :] = jnp.max(s, axis=0, keepdims=True)

    def stage_b(slot):
        for h in heads:
            m_old = stat_ref[stat(h, 0, _ROW_M), :]
            m_new = jnp.maximum(m_old, stat_ref[stat(h, slot, _ROW_MX), :])
            stat_ref[stat(h, slot, _ROW_ALPHA), :] = jnp.exp2(m_old - m_new)
            stat_ref[stat(h, 0, _ROW_M), :] = m_new
            p_ref[slot, h] = jnp.exp2(s_ref[slot, h] - m_new).astype(BF16)

    ones = jnp.ones((ONES_ROWS, blk), BF16)

    def stage_c(j, slot):
        off = pl.multiple_of(j * blk, blk)
        for h in heads:
            alpha = stat_ref[stat(h, slot, _ROW_ALPHA), :]
            va = jnp.concatenate([vt_ref[0, h * d:(h + 1) * d, pl.ds(off, blk)].astype(BF16), ones], axis=0)
            acc_ref[h] = acc_ref[h] * alpha + _dot(va, p_ref[slot, h])

    nfull = qi * diag_blocks
    j0 = first_ref[(pl.program_id(0) * N_HEAD_PAIRS + pair) * (t // tq) + qi]

    @pl.when(qi > 0)
    def _pipelined():
        stage_a(j0, 0, None)
        stage_b(0)
        stage_a(j0 + 1, 1, None)

        def two_steps(j):
            stage_c(j - 2, 0)
            stage_b(1)
            stage_a(j, 0, None)
            stage_c(j - 1, 1)
            stage_b(0)
            stage_a(j + 1, 1, None)

        def body(u, carry):
            two_steps(j0 + 2 + 4 * u)
            two_steps(j0 + 4 + 4 * u)
            return carry

        pairs_left = (nfull - j0) // 2 - 1
        lax.fori_loop(0, pairs_left // 2, body, 0)

        @pl.when(pairs_left % 2 == 1)
        def _odd_group():
            two_steps(j0 + 2 * pairs_left)
        for dg in range(diag_blocks):
            stage_c(nfull + dg - 2, dg % 2)
            stage_b((dg + 1) % 2)
            stage_a(nfull + dg, dg % 2, dg)
        stage_c(nfull + diag_blocks - 2, 0)
        stage_b(1)
        stage_c(nfull + diag_blocks - 1, 1)

    @pl.when(qi == 0)
    def _first_block():
        for dg in range(diag_blocks):
            stage_a(dg, dg % 2, dg)
            stage_b(dg % 2)
            stage_c(dg, dg % 2)

    for h in heads:
        a = acc_ref[h]
        o_ref[0, h * d:(h + 1) * d, :] = (a[0:d] / a[d:d + 1]).astype(BF16)


def _first_key_block(c_end, q_norm, k_norm, t, tq, blk):
    b = c_end.shape[0]
    nblk, nq, per = t // blk, t // tq, tq // blk
    ce = c_end[:, :, :CUMSUM_TILE // blk, :N_FOX_HEADS].reshape(b, nblk, N_FOX_HEADS)
    qn = q_norm[:, :, :, 0]
    kmax = jnp.max(k_norm[:, :, :, 0], axis=1)
    before = jnp.concatenate([jnp.zeros((b, 1, N_FOX_HEADS), F32), ce[:, per - 1:-1:per]], axis=1)
    decay = before[:, :, None, :] - ce[:, None, :, :]
    bound = (2.0 * NORM_SLACK * NORM_SLACK) * (qn * kmax[:, None, :])[:, :, None, :] + decay
    below = (jnp.arange(nblk)[None, :] < (jnp.arange(nq) * per)[:, None])[None, :, :, None]
    skippable = jnp.logical_and(bound <= -SKIP_EXPONENT, below)
    block = jnp.arange(nblk, dtype=jnp.int32)[None, None, :, None]
    n_skip = jnp.min(jnp.where(skippable, nblk, block), axis=2)
    n_skip = jnp.min(n_skip.reshape(b, nq, N_HEAD_PAIRS, HEADS_PER_LANE_BLOCK), axis=3)
    first = jnp.clip((n_skip // 2) * 2, 0, jnp.maximum(jnp.arange(nq) * per - 2, 0)[None, :, None])
    return jnp.transpose(first, (0, 2, 1)).reshape(-1).astype(jnp.int32)


def _fox_prompt(first, qt, kb, vt, c):
    b, _, t = qt.shape
    blk = FOX_BLOCK
    tq = FOX_Q_BLOCK
    assert tq == 2 * blk, "the pipeline keeps two key blocks in flight and slot = block parity"
    hp = HEADS_PER_LANE_BLOCK
    acc_rows = FOX_HEAD_DIM + ONES_ROWS
    per_pair_all_time = lambda i, p, j, first: (i, p, 0)
    grid_spec = pltpu.PrefetchScalarGridSpec(
        num_scalar_prefetch=1,
        grid=(b, N_HEAD_PAIRS, t // tq),
        in_specs=[pl.BlockSpec((1, LANES, tq), lambda i, p, j, first: (i, p, j)),
                  pl.BlockSpec((1, t, LANES), lambda i, p, j, first: (i, 0, p)),
                  pl.BlockSpec((1, LANES, t), per_pair_all_time),
                  pl.BlockSpec((1, t, LANES), lambda i, p, j, first: (i, 0, 0))],
        out_specs=pl.BlockSpec((1, LANES, tq), lambda i, p, j, first: (i, p, j)),
        scratch_shapes=[pltpu.VMEM((hp, 2 * LANES, tq), BF16),
                        pltpu.VMEM((2, hp, blk, tq), F32),
                        pltpu.VMEM((2, hp, blk, tq), BF16),
                        pltpu.VMEM((hp, acc_rows, tq), F32),
                        pltpu.VMEM((hp * 2 * 4, tq), F32)])
    return pl.pallas_call(
        functools.partial(_fox_prompt_kernel, t=t, tq=tq, blk=blk),
        grid_spec=grid_spec,
        out_shape=jax.ShapeDtypeStruct((b, FOX_WIDTH, t), BF16),
        compiler_params=_cparams(3), name="fox_prompt",
    )(first, qt, kb, vt, c)


def _fox_sample_kernel(q_ref, kn_ref, vn_ref, kc_ref, vc_ref, c_ref, o_ref, *, past, new, pairs):
    group = pl.program_id(1)
    d = FOX_HEAD_DIM
    rows = HEADS_PER_LANE_BLOCK * new
    lane = lax.broadcasted_iota(jnp.int32, (rows, LANES), 1)
    qrow = lax.broadcasted_iota(jnp.int32, (rows, LANES), 0)
    head_lanes = jnp.where(qrow < new, _indicator(lane, 0, d), _indicator(lane, d, 2 * d)).astype(BF16)
    out_lane = lax.broadcasted_iota(jnp.int32, (new, LANES), 1)
    nrow = lax.broadcasted_iota(jnp.int32, (rows, new), 0)
    ncol = lax.broadcasted_iota(jnp.int32, (rows, new), 1)
    causal = ncol <= (nrow & (new - 1))
    for pp in range(pairs):
        lanes = slice(pp * LANES, (pp + 1) * LANES)
        head0 = (group * pairs + pp) * HEADS_PER_LANE_BLOCK
        tpad = c_ref.shape[2]
        bias = jnp.concatenate(
            [jnp.broadcast_to(c_ref[0, pl.ds(head0 + h, 1), :], (new, tpad)) for h in range(HEADS_PER_LANE_BLOCK)],
            axis=0)
        q = q_ref[0, :, lanes]
        q2 = jnp.concatenate([q, q], axis=0) * head_lanes
        s_c = _dot(q2, kc_ref[0, lanes, :].astype(BF16)) - bias[:, 0:past]
        s_n = _dot_nt(q2, kn_ref[0, :, lanes].astype(BF16)) - bias[:, past:past + new]
        s_n = jnp.where(causal, s_n, MASK_VALUE)
        m = jnp.maximum(jnp.max(s_c, axis=-1, keepdims=True), jnp.max(s_n, axis=-1, keepdims=True))
        e_c = jnp.exp(s_c - m)
        e_n = jnp.exp(s_n - m)
        l = jnp.sum(e_c, axis=-1, keepdims=True) + jnp.sum(e_n, axis=-1, keepdims=True)
        o2 = _dot_nt(e_c.astype(BF16), vc_ref[0, lanes, :].astype(BF16))
        o2 = (o2 + _dot(e_n.astype(BF16), vn_ref[0, :, lanes].astype(BF16))) / l
        o = jnp.where(out_lane < d, o2[0:new], o2[new:rows])
        o_ref[0, :, lanes] = o.astype(BF16)


def _fox_sample(q, kn, vn, kct, vct, c):
    b, new, _ = q.shape
    past = kct.shape[2]
    pairs = SAMPLE_PAIRS_PER_STEP
    width = pairs * LANES
    tok = lambda i, g: (i, 0, g)
    feat = lambda i, g: (i, g, 0)
    return pl.pallas_call(
        functools.partial(_fox_sample_kernel, past=past, new=new, pairs=pairs),
        grid=(b, N_HEAD_PAIRS // pairs),
        in_specs=[pl.BlockSpec((1, new, width), tok), pl.BlockSpec((1, new, width), tok),
                  pl.BlockSpec((1, new, width), tok), pl.BlockSpec((1, width, past), feat),
                  pl.BlockSpec((1, width, past), feat),
                  pl.BlockSpec((1,) + c.shape[1:], lambda i, g: (i, 0, 0))],
        out_specs=pl.BlockSpec((1, new, width), tok),
        out_shape=jax.ShapeDtypeStruct((b, new, FOX_WIDTH), BF16),
        compiler_params=_cparams(2), name="fox_sample",
    )(q, kn, vn, kct, vct, c)


def _retention_kernel(lg_ref, q_ref, k_ref, *rest, blk, heads, fused_vg):
    if fused_vg:
        x_ref, ga_ref, w_ref, gn_ref, s0_ref, o_ref, st_ref, decay_ref, qdec_ref, kdec_ref = rest
        hn = _rmsnorm(x_ref[0], ga_ref[...]).astype(BF16)
    else:
        v_ref, g_ref, gn_ref, s0_ref, o_ref, st_ref, decay_ref, qdec_ref, kdec_ref = rest
    @pl.when(pl.program_id(2) == 0)
    def _():
        n = lax.broadcasted_iota(jnp.int32, (blk, blk), 0)
        m = lax.broadcasted_iota(jnp.int32, (blk, blk), 1)
        shift = CHUNK.bit_length() - 1
        dist = jnp.abs(n - m).astype(F32)
        same_or_earlier_chunk = (m >> shift) <= (n >> shift)
        pos = lax.broadcasted_iota(jnp.int32, (blk, RET_KEY_DIM), 0).astype(F32)
        for h in range(heads):
            lg = lg_ref[h, 0:1, 0:1]
            st_ref[0, h] = s0_ref[0, h]
            decay_ref[h] = jnp.where(same_or_earlier_chunk, jnp.exp(lg * dist), 0.0)
            qdec_ref[h] = jnp.exp(lg * (pos + 1.0))
            kdec_ref[h] = jnp.exp(lg * (blk - 1.0 - pos))

    for h in range(heads):
        lg = lg_ref[h, 0:1, 0:1]
        keys = slice(h * RET_KEY_DIM, (h + 1) * RET_KEY_DIM)
        vals = slice(h * RET_VAL_DIM, (h + 1) * RET_VAL_DIM)
        q = q_ref[0, :, keys]
        k = k_ref[0, :, keys]
        if fused_vg:
            v = _dot(hn, w_ref[:, vals]).astype(BF16)
        else:
            v = v_ref[0, :, vals]
        state = st_ref[0, h]
        s = _dot_nt(q, k) * decay_ref[h]
        qd = (q.astype(F32) * qdec_ref[h]).astype(BF16)
        y = _dot(s.astype(BF16), v) + _dot(qd, state.astype(BF16))
        kd = k.astype(F32) * kdec_ref[h]
        st_ref[0, h] = jnp.exp(lg * blk) * state + _dot(kd.T.astype(BF16), v)

        yn = y * lax.rsqrt(jnp.mean(y * y, axis=-1, keepdims=True) + EPS) * gn_ref[h]
        if fused_vg:
            g = _dot(hn, w_ref[:, RET_V_WIDTH + h * RET_VAL_DIM:RET_V_WIDTH + (h + 1) * RET_VAL_DIM])
        else:
            g = g_ref[0, :, vals]
        o_ref[0, :, vals] = (g * jax.nn.sigmoid(g) * yn).astype(BF16)


def _retention(lg, q, k, vg, gn, s0, blk):
    b, t, _ = q.shape
    heads = RET_HEADS_PER_STEP
    fused_vg = len(vg) == 3
    assert not fused_vg or heads == N_RET_HEADS, "the fused projection expects every head in one step"
    qk = lambda i, h, j: (i, j, h)
    per_group = lambda i, h, j: (h, 0, 0)
    group_state = lambda i, h, j: (i, h, 0, 0)
    if fused_vg:
        vg_specs = [pl.BlockSpec((1, blk, D_MODEL), lambda i, h, j: (i, j, 0)),
                    pl.BlockSpec((1, D_MODEL), lambda i, h, j: (0, 0)),
                    _resident((D_MODEL, 2 * RET_V_WIDTH), lambda i, h, j: (0, 0))]
    else:
        vg_specs = [pl.BlockSpec((1, blk, heads * RET_VAL_DIM), qk), pl.BlockSpec((1, blk, heads * RET_VAL_DIM), qk)]
    return pl.pallas_call(
        functools.partial(_retention_kernel, blk=blk, heads=heads, fused_vg=fused_vg),
        grid=(b, N_RET_HEADS // heads, t // blk),
        in_specs=[pl.BlockSpec((heads, 8, LANES), per_group),
                  pl.BlockSpec((1, blk, heads * RET_KEY_DIM), qk), pl.BlockSpec((1, blk, heads * RET_KEY_DIM), qk),
                  *vg_specs,
                  pl.BlockSpec((heads, 1, RET_VAL_DIM), per_group),
                  pl.BlockSpec((1, heads, RET_KEY_DIM, RET_VAL_DIM), group_state)],
        out_specs=[pl.BlockSpec((1, blk, heads * RET_VAL_DIM), qk),
                   pl.BlockSpec((1, heads, RET_KEY_DIM, RET_VAL_DIM), group_state)],
        out_shape=[jax.ShapeDtypeStruct((b, t, RET_V_WIDTH), BF16),
                   jax.ShapeDtypeStruct((b, N_RET_HEADS, RET_KEY_DIM, RET_VAL_DIM), F32)],
        scratch_shapes=[pltpu.VMEM((heads, blk, blk), F32), pltpu.VMEM((heads, blk, RET_KEY_DIM), F32),
                        pltpu.VMEM((heads, blk, RET_KEY_DIM), F32)],
        compiler_params=_cparams(3), name="retention",
    )(lg, q, k, *vg, gn, s0)


def _mixer_kernel(x_ref, oa_ref, ob_ref, g_ref, wg_ref, wa_ref, wb_ref, wo_ref, x1_ref, *, oa_feature_major):
    x = x_ref[...]
    h = _rmsnorm(x, g_ref[...]).astype(BF16)
    gate_a = jax.nn.sigmoid(_dot(h, wg_ref[:, 0:D_MODEL]))
    gate_b = jax.nn.sigmoid(_dot(h, wg_ref[:, D_MODEL:2 * D_MODEL]))
    if oa_feature_major:
        ya = lax.dot_general(oa_ref[0], wa_ref[...], (((0,), (0,)), ((), ())), preferred_element_type=F32)
    else:
        ya = _dot(oa_ref[...], wa_ref[...])
    merged = gate_a * ya + gate_b * _dot(ob_ref[...], wb_ref[...])
    x1_ref[...] = x + _dot(merged.astype(BF16), wo_ref[...])


def _mixer(x, oa, ob, g, wg, wa, wb, wo):
    n = x.shape[0]
    tm = MIX_TOKEN_TILE
    tok = lambda i: (i, 0)
    fixed = lambda i: (0, 0)
    feature_major = oa.ndim == 3
    if feature_major:
        per_batch = oa.shape[2] // tm
        oa_spec = pl.BlockSpec((1, FOX_WIDTH, tm), lambda i: (i // per_batch, 0, i % per_batch))
    else:
        oa_spec = pl.BlockSpec((tm, FOX_WIDTH), tok)
    return pl.pallas_call(
        functools.partial(_mixer_kernel, oa_feature_major=feature_major),
        grid=(n // tm,),
        in_specs=[pl.BlockSpec((tm, D_MODEL), tok), oa_spec,
                  pl.BlockSpec((tm, RET_V_WIDTH), tok), pl.BlockSpec((1, D_MODEL), fixed),
                  _resident((D_MODEL, 2 * D_MODEL), fixed), _resident((FOX_WIDTH, D_MODEL), fixed),
                  _resident((RET_V_WIDTH, D_MODEL), fixed), _resident((D_MODEL, D_MODEL), fixed)],
        out_specs=pl.BlockSpec((tm, D_MODEL), tok),
        out_shape=jax.ShapeDtypeStruct((n, D_MODEL), F32),
        compiler_params=_cparams(1), name="mixer",
    )(x, oa, ob, g, wg, wa, wb, wo)


def _mlp_kernel(x_ref, g_ref, wu_ref, wd_ref, gf_ref, y_ref):
    x = x_ref[...]
    h = _rmsnorm(x, g_ref[...]).astype(BF16)
    acc = x
    for c in range(D_FF // D_MODEL):
        cols = slice(c * D_MODEL, (c + 1) * D_MODEL)
        u = jnp.square(jnp.maximum(_dot(h, wu_ref[:, cols]), 0.0)).astype(BF16)
        acc = acc + _dot(u, wd_ref[cols, :])
    y_ref[...] = _rmsnorm(acc, gf_ref[...])


def _mlp(x, g, wu, wd, gf):
    n = x.shape[0]
    tm = MIX_TOKEN_TILE
    tok = lambda i: (i, 0)
    fixed = lambda i: (0, 0)
    return pl.pallas_call(
        _mlp_kernel,
        grid=(n // tm,),
        in_specs=[pl.BlockSpec((tm, D_MODEL), tok), pl.BlockSpec((1, D_MODEL), fixed),
                  _resident((D_MODEL, D_FF), fixed), _resident((D_FF, D_MODEL), fixed),
                  pl.BlockSpec((1, D_MODEL), fixed)],
        out_specs=pl.BlockSpec((tm, D_MODEL), tok),
        out_shape=jax.ShapeDtypeStruct((n, D_MODEL), F32),
        compiler_params=_cparams(1), name="mlp",
    )(x, g, wu, wd, gf)


def _rope_tables(pos):
    inv_freq = ROPE_BASE ** (-jnp.arange(0, RET_KEY_DIM, 2, dtype=F32) / RET_KEY_DIM)
    ang = pos.astype(F32)[:, None] * inv_freq[None, :]
    return jnp.cos(ang), jnp.sin(ang)


def _after_attention(x, pos, o_a, params, ret_blk, state0, fuse_vg):
    b, t, _ = x.shape
    n = b * t
    xf = x.reshape(n, D_MODEL)
    cos, sin = _rope_tables(pos)
    reps = max(1, TOKEN_TILE // t)
    q_r, k_r = _proj_ret_qk(xf, params["g_attn"], params["w_ret_qk"], jnp.tile(cos, (reps, 1)),
                            jnp.tile(sin, (reps, 1)))
    if fuse_vg:
        vg = (x, params["g_attn"], params["w_ret_vg"])
    else:
        v_r, g_r = _proj_ret_vg(xf, params["g_attn"], params["w_ret_vg"])
        vg = (v_r.reshape(b, t, RET_V_WIDTH), g_r.reshape(b, t, RET_V_WIDTH))
    o_b, state = _retention(params["lg"], q_r.reshape(b, t, RET_QK_WIDTH), k_r.reshape(b, t, RET_QK_WIDTH),
                            vg, params["g_ret_norm"], state0, ret_blk)
    x1 = _mixer(xf, o_a, o_b.reshape(n, RET_V_WIDTH), params["g_attn"],
                params["w_gates"], params["w_branch_a"], params["w_branch_b"], params["w_out"])
    y = _mlp(x1, params["g_mlp"], params["w_up"], params["w_down"], params["g_final"])
    return y.reshape(b, t, D_MODEL), state[None]


def kernel(x_prompt, x_sample, cache_fox_k, cache_fox_v, cache_fox_logf, state_ret, g_attn, w_in, b_forget,
           g_ret_norm, w_branch, w_out, g_mlp, w_up, w_down, g_final):
    assert w_in.shape[0] == 1, "single-layer trunk"
    wi = w_in[0]
    fox_cols = 3 * FOX_WIDTH + N_FOX_HEADS
    qk_end = fox_cols + 2 * RET_QK_WIDTH
    vg_end = qk_end + 2 * RET_V_WIDTH
    lane_pad = LANES - N_FOX_HEADS
    lg = jnp.log(1.0 - 2.0 ** (-5.0 - jnp.arange(N_RET_HEADS, dtype=F32)))
    w_fox = jnp.pad(wi[:, :fox_cols], ((0, 0), (0, lane_pad))).astype(BF16)
    b_row = jnp.pad(b_forget[0], (0, lane_pad)).reshape(1, LANES)
    params = {
        "g_attn": g_attn[0].reshape(1, D_MODEL),
        "w_ret_qk": wi[:, fox_cols:qk_end].astype(BF16),
        "w_ret_vg": wi[:, qk_end:vg_end].astype(BF16),
        "w_gates": wi[:, vg_end:].astype(BF16),
        "lg": jnp.broadcast_to(lg[:, None, None], (N_RET_HEADS, 8, LANES)),
        "g_ret_norm": g_ret_norm[0].reshape(N_RET_HEADS, 1, RET_VAL_DIM),
        "w_branch_a": w_branch[0, :FOX_WIDTH].astype(BF16),
        "w_branch_b": w_branch[0, FOX_WIDTH:].astype(BF16),
        "w_out": w_out[0].astype(BF16),
        "g_mlp": g_mlp[0].reshape(1, D_MODEL),
        "w_up": w_up[0].astype(BF16),
        "w_down": w_down[0].astype(BF16),
        "g_final": g_final.reshape(1, D_MODEL),
    }
    bp, tp, _ = x_prompt.shape
    bs, ts, _ = x_sample.shape
    past = cache_fox_k.shape[2]

    qt, kt, vt, lft, lf_pad, q_norm, k_norm, kb = _proj_fox_t(
        x_prompt, params["g_attn"], wi[:, :fox_cols].T.astype(BF16), w_fox[:, 3 * FOX_WIDTH:],
        jnp.broadcast_to(b_forget[0][:, None], (N_FOX_HEADS, LANES)), b_row)
    c_pieces, c_end = _cumsum_pieces(lf_pad, CUMSUM_TILE, LOG2E)
    first = _first_key_block(c_end, q_norm, k_norm, tp, FOX_Q_BLOCK, FOX_BLOCK)
    o_a = _fox_prompt(first, qt, kb, vt, c_pieces)
    zero_state = jnp.zeros((bp, N_RET_HEADS, RET_KEY_DIM, RET_VAL_DIM), F32)
    yp, sp = _after_attention(x_prompt, jnp.arange(tp), o_a, params, RET_BLOCK, zero_state, True)
    to_heads = lambda a: jnp.transpose(a.reshape(bp, N_FOX_HEADS, FOX_HEAD_DIM, tp), (0, 3, 1, 2))[None]
    kp, vp = to_heads(kt), to_heads(vt)
    fp = jnp.transpose(lft, (0, 2, 1))[None]

    q, k, v, lf = _proj_fox(x_sample.reshape(bs * ts, D_MODEL), params["g_attn"], w_fox, b_row)
    feature_major = lambda a: jnp.transpose(a[0], (0, 2, 3, 1)).reshape(bs, FOX_WIDTH, past)
    lf_time = jnp.concatenate([jnp.transpose(cache_fox_logf[0], (0, 2, 1)),
                               jnp.transpose(lf.reshape(bs, ts, N_FOX_HEADS), (0, 2, 1))], axis=2)
    lf_time = jnp.pad(lf_time, ((0, 0), (0, 0), (0, -(past + ts) % LANES)))
    o_a = _fox_sample(q.reshape(bs, ts, FOX_WIDTH), k.reshape(bs, ts, FOX_WIDTH), v.reshape(bs, ts, FOX_WIDTH),
                      feature_major(cache_fox_k), feature_major(cache_fox_v), _cumsum_lanes(lf_time))
    ys, ss = _after_attention(x_sample, past + jnp.arange(ts), o_a.reshape(bs * ts, FOX_WIDTH), params, CHUNK,
                              state_ret[0], False)
    shape5 = (1, bs, ts, N_FOX_HEADS, FOX_HEAD_DIM)
    return (yp, ys, kp, vp, fp, sp, k.reshape(shape5), v.reshape(shape5),
            lf.reshape(1, bs, ts, N_FOX_HEADS), ss)
```

```python
import functools

import jax
import jax.numpy as jnp
from jax import lax
from jax.experimental import pallas as pl
from jax.experimental.pallas import tpu as pltpu

D_MODEL = 1024
N_FOX_HEADS = 16
FOX_HEAD_DIM = 64
FOX_WIDTH = N_FOX_HEADS * FOX_HEAD_DIM
N_RET_HEADS = 4
RET_KEY_DIM = 256
RET_VAL_DIM = 512
RET_QK_WIDTH = N_RET_HEADS * RET_KEY_DIM
RET_V_WIDTH = N_RET_HEADS * RET_VAL_DIM
D_FF = 4 * D_MODEL
CHUNK = 64
ROPE_BASE = 10000.0
EPS = 1e-6

LANES = 128
SUBLANES = 8
HEADS_PER_LANE_BLOCK = LANES // FOX_HEAD_DIM
N_HEAD_PAIRS = N_FOX_HEADS // HEADS_PER_LANE_BLOCK
N_C_PIECES = 3
ONES_ROWS = 16
MASK_VALUE = -1e30
LOG2E = 1.4426950408889634
VMEM_LIMIT = 56 * 1024 * 1024

TOKEN_TILE = 512
MIX_TOKEN_TILE = 512
FOX_BLOCK = 256
FOX_Q_BLOCK = 512
SAMPLE_PAIRS_PER_STEP = 4
RET_BLOCK = 256
RET_HEADS_PER_STEP = 4
CUMSUM_TILE = 512
SKIP_EXPONENT = 160.0
NORM_SLACK = 1.01

BF16 = jnp.bfloat16
F32 = jnp.float32


def _cparams(n_axes):
    return pltpu.CompilerParams(dimension_semantics=("arbitrary",) * n_axes,
                                vmem_limit_bytes=VMEM_LIMIT)


def _resident(shape, index_map):
    return pl.BlockSpec(shape, index_map, pipeline_mode=pl.Buffered(1))


def _dot(a, b):
    return jnp.dot(a, b, preferred_element_type=F32)


def _dot_nt(a, b):
    return lax.dot_general(a, b, (((1,), (1,)), ((), ())), preferred_element_type=F32)


def _rmsnorm(x, g):
    return x * lax.rsqrt(jnp.mean(x * x, axis=-1, keepdims=True) + EPS) * g


def _log_sigmoid(z):
    return -(jnp.maximum(-z, 0.0) + jnp.log(1.0 + jnp.exp(-jnp.abs(z))))


def _split3(x):
    a = x.astype(BF16)
    r = x - a.astype(F32)
    b = r.astype(BF16)
    c = (r - b.astype(F32)).astype(BF16)
    return a, b, c


def _indicator(idx, lo, hi):
    return jnp.where(idx >= lo, jnp.where(idx < hi, 1.0, 0.0), 0.0)


def _proj_fox_t_kernel(x_ref, g_ref, wt_ref, wf_ref, bcol_ref, brow_ref, qt_ref, kt_ref, vt_ref, lft_ref, lfp_ref,
                       qn_ref, kn_ref, kb_ref):
    h = _rmsnorm(x_ref[0], g_ref[...])
    ht = h.T.astype(BF16)
    w = FOX_WIDTH
    qs = _dot(wt_ref[0:w, :], ht) * (FOX_HEAD_DIM ** -0.5 * LOG2E)
    kf = _dot(wt_ref[w:2 * w, :], ht)
    qt_ref[0] = qs.astype(BF16)
    kt_ref[0] = kf
    kb_ref[0] = kf.T.astype(BF16)
    for src, dst in ((qs, qn_ref), (kf, kn_ref)):
        sq = jnp.sum((src * src).reshape(N_FOX_HEADS, FOX_HEAD_DIM, src.shape[1]), axis=1)
        dst[0, 0] = jnp.broadcast_to(jnp.sqrt(jnp.max(sq, axis=1, keepdims=True)), (N_FOX_HEADS, LANES))
    vt_ref[0] = _dot(wt_ref[2 * w:3 * w, :], ht)
    lft_ref[0] = _log_sigmoid(_dot(wt_ref[3 * w:3 * w + N_FOX_HEADS, :], ht) + bcol_ref[:, 0:1])
    lfp_ref[0] = _log_sigmoid(_dot(h.astype(BF16), wf_ref[...]) + brow_ref[...])


def _proj_fox_t(x, g, wt, wf, bcol, brow):
    b, t, _ = x.shape
    tm = TOKEN_TILE
    fixed = lambda i, j: (0, 0)
    feat = lambda i, j: (i, 0, j)
    return pl.pallas_call(
        _proj_fox_t_kernel,
        grid=(b, t // tm),
        in_specs=[pl.BlockSpec((1, tm, D_MODEL), lambda i, j: (i, j, 0)), pl.BlockSpec((1, D_MODEL), fixed),
                  pl.BlockSpec(wt.shape, fixed), pl.BlockSpec(wf.shape, fixed),
                  pl.BlockSpec(bcol.shape, fixed), pl.BlockSpec(brow.shape, fixed)],
        out_specs=[pl.BlockSpec((1, FOX_WIDTH, tm), feat), pl.BlockSpec((1, FOX_WIDTH, tm), feat),
                   pl.BlockSpec((1, FOX_WIDTH, tm), feat), pl.BlockSpec((1, N_FOX_HEADS, tm), feat),
                   pl.BlockSpec((1, tm, LANES), lambda i, j: (i, j, 0)),
                   pl.BlockSpec((1, 1, N_FOX_HEADS, LANES), lambda i, j: (i, j, 0, 0)),
                   pl.BlockSpec((1, 1, N_FOX_HEADS, LANES), lambda i, j: (i, j, 0, 0)),
                   pl.BlockSpec((1, tm, FOX_WIDTH), lambda i, j: (i, j, 0))],
        out_shape=[jax.ShapeDtypeStruct((b, FOX_WIDTH, t), BF16), jax.ShapeDtypeStruct((b, FOX_WIDTH, t), F32),
                   jax.ShapeDtypeStruct((b, FOX_WIDTH, t), F32), jax.ShapeDtypeStruct((b, N_FOX_HEADS, t), F32),
                   jax.ShapeDtypeStruct((b, t, LANES), F32),
                   jax.ShapeDtypeStruct((b, t // tm, N_FOX_HEADS, LANES), F32),
                   jax.ShapeDtypeStruct((b, t // tm, N_FOX_HEADS, LANES), F32),
                   jax.ShapeDtypeStruct((b, t, FOX_WIDTH), BF16)],
        compiler_params=_cparams(2), name="proj_fox_t",
    )(x, g, wt, wf, bcol, brow)


def _proj_fox_kernel(x_ref, g_ref, w_ref, b_ref, q_ref, k_ref, v_ref, lf_ref):
    h = _rmsnorm(x_ref[...], g_ref[...]).astype(BF16)
    w = FOX_WIDTH
    q_ref[...] = (_dot(h, w_ref[:, 0:w]) * (FOX_HEAD_DIM ** -0.5 * LOG2E)).astype(BF16)
    k_ref[...] = _dot(h, w_ref[:, w:2 * w])
    v_ref[...] = _dot(h, w_ref[:, 2 * w:3 * w])
    lf = _log_sigmoid(_dot(h, w_ref[:, 3 * w:3 * w + LANES]) + b_ref[...])
    lf_ref[...] = lf[:, 0:N_FOX_HEADS]


def _proj_fox(x, g, w, b):
    n = x.shape[0]
    tm = TOKEN_TILE
    tok = lambda i: (i, 0)
    fixed = lambda i: (0, 0)
    return pl.pallas_call(
        _proj_fox_kernel,
        grid=(n // tm,),
        in_specs=[pl.BlockSpec((tm, D_MODEL), tok), pl.BlockSpec((1, D_MODEL), fixed),
                  pl.BlockSpec(w.shape, fixed), pl.BlockSpec((1, LANES), fixed)],
        out_specs=[pl.BlockSpec((tm, FOX_WIDTH), tok), pl.BlockSpec((tm, FOX_WIDTH), tok),
                   pl.BlockSpec((tm, FOX_WIDTH), tok), pl.BlockSpec((tm, N_FOX_HEADS), tok)],
        out_shape=[jax.ShapeDtypeStruct((n, FOX_WIDTH), BF16), jax.ShapeDtypeStruct((n, FOX_WIDTH), F32),
                   jax.ShapeDtypeStruct((n, FOX_WIDTH), F32), jax.ShapeDtypeStruct((n, N_FOX_HEADS), F32)],
        compiler_params=_cparams(1), name="proj_fox",
    )(x, g, w, b)


def _proj_ret_qk_kernel(x_ref, g_ref, w_ref, cos_ref, sin_ref, q_ref, k_ref):
    h = _rmsnorm(x_ref[...], g_ref[...]).astype(BF16)
    cos = cos_ref[...]
    sin = sin_ref[...]
    half = RET_KEY_DIM // 2
    for out_ref, base, scale in ((q_ref, 0, 1.0), (k_ref, RET_QK_WIDTH, RET_KEY_DIM ** -0.5)):
        for hd in range(N_RET_HEADS):
            lo = hd * RET_KEY_DIM
            z = _dot(h, w_ref[:, base + lo:base + lo + RET_KEY_DIM])
            x1 = z[:, :half]
            x2 = z[:, half:]
            out_ref[:, lo:lo + half] = ((x1 * cos - x2 * sin) * scale).astype(BF16)
            out_ref[:, lo + half:lo + RET_KEY_DIM] = ((x1 * sin + x2 * cos) * scale).astype(BF16)


def _proj_ret_qk(x, g, w, cos, sin):
    n = x.shape[0]
    tm = TOKEN_TILE
    period = cos.shape[0] // tm
    tok = lambda i: (i, 0)
    fixed = lambda i: (0, 0)
    pos = lambda i: (i % period, 0)
    half = RET_KEY_DIM // 2
    return pl.pallas_call(
        _proj_ret_qk_kernel,
        grid=(n // tm,),
        in_specs=[pl.BlockSpec((tm, D_MODEL), tok), pl.BlockSpec((1, D_MODEL), fixed),
                  pl.BlockSpec((D_MODEL, 2 * RET_QK_WIDTH), fixed),
                  pl.BlockSpec((tm, half), pos), pl.BlockSpec((tm, half), pos)],
        out_specs=[pl.BlockSpec((tm, RET_QK_WIDTH), tok), pl.BlockSpec((tm, RET_QK_WIDTH), tok)],
        out_shape=[jax.ShapeDtypeStruct((n, RET_QK_WIDTH), BF16)] * 2,
        compiler_params=_cparams(1), name="proj_ret_qk",
    )(x, g, w, cos, sin)


def _proj_ret_vg_kernel(x_ref, g_ref, w_ref, v_ref, gr_ref):
    h = _rmsnorm(x_ref[...], g_ref[...]).astype(BF16)
    v_ref[...] = _dot(h, w_ref[:, 0:RET_V_WIDTH]).astype(BF16)
    gr_ref[...] = _dot(h, w_ref[:, RET_V_WIDTH:2 * RET_V_WIDTH])


def _proj_ret_vg(x, g, w):
    n = x.shape[0]
    tm = TOKEN_TILE
    tok = lambda i: (i, 0)
    fixed = lambda i: (0, 0)
    return pl.pallas_call(
        _proj_ret_vg_kernel,
        grid=(n // tm,),
        in_specs=[pl.BlockSpec((tm, D_MODEL), tok), pl.BlockSpec((1, D_MODEL), fixed),
                  pl.BlockSpec((D_MODEL, 2 * RET_V_WIDTH), fixed)],
        out_specs=[pl.BlockSpec((tm, RET_V_WIDTH), tok), pl.BlockSpec((tm, RET_V_WIDTH), tok)],
        out_shape=[jax.ShapeDtypeStruct((n, RET_V_WIDTH), BF16), jax.ShapeDtypeStruct((n, RET_V_WIDTH), F32)],
        compiler_params=_cparams(1), name="proj_ret_vg",
    )(x, g, w)


def _cumsum_kernel(lf_ref, c_ref, cend_ref, carry_ref, *, tb, scale):
    @pl.when(pl.program_id(1) == 0)
    def _():
        carry_ref[...] = jnp.zeros_like(carry_ref)

    row = lax.broadcasted_iota(jnp.int32, (tb, tb), 0)
    col = lax.broadcasted_iota(jnp.int32, (tb, tb), 1)
    tri = jnp.where(col <= row, 1.0, 0.0).astype(BF16)
    c = carry_ref[0:1, :]
    for piece in _split3(lf_ref[0]):
        c = c + _dot(tri, piece)
    carry_ref[...] = jnp.broadcast_to(c[tb - 1:tb, :], carry_ref.shape)
    erow = lax.broadcasted_iota(jnp.int32, (SUBLANES, LANES), 0)
    ends = jnp.zeros((SUBLANES, LANES), F32)
    for e in range(tb // FOX_BLOCK):
        ends = jnp.where(erow == e, c[(e + 1) * FOX_BLOCK - 1:(e + 1) * FOX_BLOCK, :] * scale, ends)
    cend_ref[0, 0] = ends

    prow = lax.broadcasted_iota(jnp.int32, (LANES, LANES), 0)
    pcol = lax.broadcasted_iota(jnp.int32, (LANES, LANES), 1)
    out = jnp.zeros((tb, LANES), F32)
    for p, piece in enumerate(_split3(-c * scale)):
        place = jnp.where(pcol == prow + p * N_FOX_HEADS, jnp.where(prow < N_FOX_HEADS, 1.0, 0.0), 0.0)
        out = out + _dot(piece, place.astype(BF16))
    c_ref[0] = out.astype(BF16)


def _cumsum_pieces(lf, tb, scale):
    b, t, _ = lf.shape
    blk = lambda i, j: (i, j, 0)
    return pl.pallas_call(
        functools.partial(_cumsum_kernel, tb=tb, scale=scale),
        grid=(b, t // tb),
        in_specs=[pl.BlockSpec((1, tb, LANES), blk)],
        out_specs=[pl.BlockSpec((1, tb, LANES), blk), pl.BlockSpec((1, 1, SUBLANES, LANES), lambda i, j: (i, j, 0, 0))],
        out_shape=[jax.ShapeDtypeStruct((b, t, LANES), BF16), jax.ShapeDtypeStruct((b, t // tb, SUBLANES, LANES), F32)],
        scratch_shapes=[pltpu.VMEM((SUBLANES, LANES), F32)],
        compiler_params=_cparams(2), name="cumsum",
    )(lf)


def _cumsum_lanes_kernel(lf_ref, c_ref, *, scale):
    heads, t = lf_ref.shape[1], lf_ref.shape[2]
    row = lax.broadcasted_iota(jnp.int32, (LANES, LANES), 0)
    col = lax.broadcasted_iota(jnp.int32, (LANES, LANES), 1)
    tri = jnp.where(row <= col, 1.0, 0.0).astype(BF16)
    ones = jnp.ones((LANES, LANES), BF16)
    carry = jnp.zeros((heads, LANES), F32)
    for seg in range(t // LANES):
        lanes = slice(seg * LANES, (seg + 1) * LANES)
        stack = jnp.concatenate(_split3(lf_ref[0, :, lanes]), axis=0)
        within = _dot(stack, tri)
        total = _dot(stack, ones)
        fold = lambda a: a[0:heads] + a[heads:2 * heads] + a[2 * heads:3 * heads]
        c_ref[0, :, lanes] = (fold(within) + carry) * scale
        carry = carry + fold(total)


def _cumsum_lanes(lf, scale):
    b, heads, t = lf.shape
    blk = lambda i: (i, 0, 0)
    return pl.pallas_call(
        functools.partial(_cumsum_lanes_kernel, scale=scale),
        grid=(b,),
        in_specs=[pl.BlockSpec((1, heads, t), blk)],
        out_specs=pl.BlockSpec((1, heads, t), blk),
        out_shape=jax.ShapeDtypeStruct((b, heads, t), F32),
        compiler_params=_cparams(1), name="cumsum_lanes",
    )(lf)


def _piece_selector(shape, axis, head):
    idx = lax.broadcasted_iota(jnp.int32, shape, axis)
    hit = jnp.where(idx < N_C_PIECES * N_FOX_HEADS, jnp.where((idx & (N_FOX_HEADS - 1)) == head, 1.0, 0.0), 0.0)
    return hit


_ROW_M, _ROW_MX, _ROW_ALPHA = 0, 1, 2


def _fox_prompt_kernel(first_ref, qt_ref, kb_ref, vt_ref, c_ref, o_ref, qa_ref, s_ref, p_ref, acc_ref,
                       stat_ref, *, t, tq, blk):
    pair = pl.program_id(1)
    qi = pl.program_id(2)
    diag_blocks = tq // blk
    d = FOX_HEAD_DIM
    heads = range(HEADS_PER_LANE_BLOCK)

    qt = qt_ref[0]
    row = lax.broadcasted_iota(jnp.int32, (LANES, tq), 0)
    for h in heads:
        in_head = _indicator(row, h * d, (h + 1) * d).astype(BF16)
        sel = _piece_selector((LANES, tq), 0, pair * HEADS_PER_LANE_BLOCK + h).astype(BF16)
        qa_ref[h] = jnp.concatenate([qt * in_head, sel], axis=0)

    acc_ref[...] = jnp.zeros_like(acc_ref)
    stat_ref[...] = jnp.full(stat_ref.shape, MASK_VALUE, F32)
    krow = lax.broadcasted_iota(jnp.int32, (blk, tq), 0)
    qcol = lax.broadcasted_iota(jnp.int32, (blk, tq), 1)

    def stat(h, slot, kind):
        r = (h * 2 + slot) * 4 + kind
        return slice(r, r + 1)

    def stage_a(j, slot, diag):
        off = pl.multiple_of(j * blk, blk)
        kb = jnp.concatenate([kb_ref[0, pl.ds(off, blk), :], c_ref[0, pl.ds(off, blk), :]], axis=1)
        for h in heads:
            s = _dot(kb, qa_ref[h])
            if diag is not None:
                s = jnp.where(krow + diag * blk <= qcol, s, MASK_VALUE)
            s_ref[slot, h] = s
            stat_ref[stat(h, slot, _ROW_MX), :] = jnp.max(s, axis=0, keepdims=True)

    def stage_b(slot):
        for h in heads:
            m_old = stat_ref[stat(h, 0, _ROW_M), :]
            m_new = jnp.maximum(m_old, stat_ref[stat(h, slot, _ROW_MX), :])
            stat_ref[stat(h, slot, _ROW_ALPHA), :] = jnp.exp2(m_old - m_new)
            stat_ref[stat(h, 0, _ROW_M), :] = m_new
            p_ref[slot, h] = jnp.exp2(s_ref[slot, h] - m_new).astype(BF16)

    ones = jnp.ones((ONES_ROWS, blk), BF16)

    def stage_c(j, slot):
        off = pl.multiple_of(j * blk, blk)
        for h in heads:
            alpha = stat_ref[stat(h, slot, _ROW_ALPHA), :]
            va = jnp.concatenate([vt_ref[0, h * d:(h + 1) * d, pl.ds(off, blk)].astype(BF16), ones], axis=0)
            acc_ref[h] = acc_ref[h] * alpha + _dot(va, p_ref[slot, h])

    nfull = qi * diag_blocks
    j0 = first_ref[(pl.program_id(0) * N_HEAD_PAIRS + pair) * (t // tq) + qi]

    @pl.when(qi > 0)
    def _pipelined():
        stage_a(j0, 0, None)
        stage_b(0)
        stage_a(j0 + 1, 1, None)

        def two_steps(j):
            stage_c(j - 2, 0)
            stage_b(1)
            stage_a(j, 0, None)
            stage_c(j - 1, 1)
            stage_b(0)
            stage_a(j + 1, 1, None)

        def body(u, carry):
            two_steps(j0 + 2 + 4 * u)
            two_steps(j0 + 4 + 4 * u)
            return carry

        pairs_left = (nfull - j0) // 2 - 1
        lax.fori_loop(0, pairs_left // 2, body, 0)

        @pl.when(pairs_left % 2 == 1)
        def _odd_group():
            two_steps(j0 + 2 * pairs_left)
        for dg in range(diag_blocks):
            stage_c(nfull + dg - 2, dg % 2)
            stage_b((dg + 1) % 2)
            stage_a(nfull + dg, dg % 2, dg)
        stage_c(nfull + diag_blocks - 2, 0)
        stage_b(1)
        stage_c(nfull + diag_blocks - 1, 1)

    @pl.when(qi == 0)
    def _first_block():
        for dg in range(diag_blocks):
            stage_a(dg, dg % 2, dg)
            stage_b(dg % 2)
            stage_c(dg, dg % 2)

    for h in heads:
        a = acc_ref[h]
        o_ref[0, h * d:(h + 1) * d, :] = (a[0:d] / a[d:d + 1]).astype(BF16)


def _first_key_block(c_end, q_norm, k_norm, t, tq, blk):
    b = c_end.shape[0]
    nblk, nq, per = t // blk, t // tq, tq // blk
    ce = c_end[:, :, :CUMSUM_TILE // blk, :N_FOX_HEADS].reshape(b, nblk, N_FOX_HEADS)
    qn = q_norm[:, :, :, 0]
    kmax = jnp.max(k_norm[:, :, :, 0], axis=1)
    before = jnp.concatenate([jnp.zeros((b, 1, N_FOX_HEADS), F32), ce[:, per - 1:-1:per]], axis=1)
    decay = before[:, :, None, :] - ce[:, None, :, :]
    bound = (2.0 * NORM_SLACK * NORM_SLACK) * (qn * kmax[:, None, :])[:, :, None, :] + decay
    below = (jnp.arange(nblk)[None, :] < (jnp.arange(nq) * per)[:, None])[None, :, :, None]
    skippable = jnp.logical_and(bound <= -SKIP_EXPONENT, below)
    block = jnp.arange(nblk, dtype=jnp.int32)[None, None, :, None]
    n_skip = jnp.min(jnp.where(skippable, nblk, block), axis=2)
    n_skip = jnp.min(n_skip.reshape(b, nq, N_HEAD_PAIRS, HEADS_PER_LANE_BLOCK), axis=3)
    first = jnp.clip((n_skip // 2) * 2, 0, jnp.maximum(jnp.arange(nq) * per - 2, 0)[None, :, None])
    return jnp.transpose(first, (0, 2, 1)).reshape(-1).astype(jnp.int32)


def _fox_prompt(first, qt, kb, vt, c):
    b, _, t = qt.shape
    blk = FOX_BLOCK
    tq = FOX_Q_BLOCK
    assert tq == 2 * blk, "the pipeline keeps two key blocks in flight and slot = block parity"
    hp = HEADS_PER_LANE_BLOCK
    acc_rows = FOX_HEAD_DIM + ONES_ROWS
    per_pair_all_time = lambda i, p, j, first: (i, p, 0)
    grid_spec = pltpu.PrefetchScalarGridSpec(
        num_scalar_prefetch=1,
        grid=(b, N_HEAD_PAIRS, t // tq),
        in_specs=[pl.BlockSpec((1, LANES, tq), lambda i, p, j, first: (i, p, j)),
                  pl.BlockSpec((1, t, LANES), lambda i, p, j, first: (i, 0, p)),
                  pl.BlockSpec((1, LANES, t), per_pair_all_time),
                  pl.BlockSpec((1, t, LANES), lambda i, p, j, first: (i, 0, 0))],
        out_specs=pl.BlockSpec((1, LANES, tq), lambda i, p, j, first: (i, p, j)),
        scratch_shapes=[pltpu.VMEM((hp, 2 * LANES, tq), BF16),
                        pltpu.VMEM((2, hp, blk, tq), F32),
                        pltpu.VMEM((2, hp, blk, tq), BF16),
                        pltpu.VMEM((hp, acc_rows, tq), F32),
                        pltpu.VMEM((hp * 2 * 4, tq), F32)])
    return pl.pallas_call(
        functools.partial(_fox_prompt_kernel, t=t, tq=tq, blk=blk),
        grid_spec=grid_spec,
        out_shape=jax.ShapeDtypeStruct((b, FOX_WIDTH, t), BF16),
        compiler_params=_cparams(3), name="fox_prompt",
    )(first, qt, kb, vt, c)


def _fox_sample_kernel(q_ref, kn_ref, vn_ref, kc_ref, vc_ref, c_ref, o_ref, *, past, new, pairs):
    group = pl.program_id(1)
    d = FOX_HEAD_DIM
    rows = HEADS_PER_LANE_BLOCK * new
    lane = lax.broadcasted_iota(jnp.int32, (rows, LANES), 1)
    qrow = lax.broadcasted_iota(jnp.int32, (rows, LANES), 0)
    head_lanes = jnp.where(qrow < new, _indicator(lane, 0, d), _indicator(lane, d, 2 * d)).astype(BF16)
    out_lane = lax.broadcasted_iota(jnp.int32, (new, LANES), 1)
    nrow = lax.broadcasted_iota(jnp.int32, (rows, new), 0)
    ncol = lax.broadcasted_iota(jnp.int32, (rows, new), 1)
    causal = ncol <= (nrow & (new - 1))
    for pp in range(pairs):
        lanes = slice(pp * LANES, (pp + 1) * LANES)
        head0 = (group * pairs + pp) * HEADS_PER_LANE_BLOCK
        tpad = c_ref.shape[2]
        bias = jnp.concatenate(
            [jnp.broadcast_to(c_ref[0, pl.ds(head0 + h, 1), :], (new, tpad)) for h in range(HEADS_PER_LANE_BLOCK)],
            axis=0)
        q = q_ref[0, :, lanes]
        q2 = jnp.concatenate([q, q], axis=0) * head_lanes
        s_c = _dot(q2, kc_ref[0, lanes, :].astype(BF16)) - bias[:, 0:past]
        s_n = _dot_nt(q2, kn_ref[0, :, lanes].astype(BF16)) - bias[:, past:past + new]
        s_n = jnp.where(causal, s_n, MASK_VALUE)
        m = jnp.maximum(jnp.max(s_c, axis=-1, keepdims=True), jnp.max(s_n, axis=-1, keepdims=True))
        e_c = jnp.exp2(s_c - m)
        e_n = jnp.exp2(s_n - m)
        l = jnp.sum(e_c, axis=-1, keepdims=True) + jnp.sum(e_n, axis=-1, keepdims=True)
        o2 = _dot_nt(e_c.astype(BF16), vc_ref[0, lanes, :].astype(BF16))
        o2 = (o2 + _dot(e_n.astype(BF16), vn_ref[0, :, lanes].astype(BF16))) / l
        o = jnp.where(out_lane < d, o2[0:new], o2[new:rows])
        o_ref[0, :, lanes] = o.astype(BF16)


def _fox_sample(q, kn, vn, kct, vct, c):
    b, new, _ = q.shape
    past = kct.shape[2]
    pairs = SAMPLE_PAIRS_PER_STEP
    width = pairs * LANES
    tok = lambda i, g: (i, 0, g)
    feat = lambda i, g: (i, g, 0)
    return pl.pallas_call(
        functools.partial(_fox_sample_kernel, past=past, new=new, pairs=pairs),
        grid=(b, N_HEAD_PAIRS // pairs),
        in_specs=[pl.BlockSpec((1, new, width), tok), pl.BlockSpec((1, new, width), tok),
                  pl.BlockSpec((1, new, width), tok), pl.BlockSpec((1, width, past), feat),
                  pl.BlockSpec((1, width, past), feat),
                  pl.BlockSpec((1,) + c.shape[1:], lambda i, g: (i, 0, 0))],
        out_specs=pl.BlockSpec((1, new, width), tok),
        out_shape=jax.ShapeDtypeStruct((b, new, FOX_WIDTH), BF16),
        compiler_params=_cparams(2), name="fox_sample",
    )(q, kn, vn, kct, vct, c)


def _retention_kernel(lg_ref, q_ref, k_ref, *rest, blk, heads, fused_vg):
    if fused_vg:
        x_ref, ga_ref, w_ref, gn_ref, s0_ref, o_ref, st_ref, decay_ref, qdec_ref, kdec_ref = rest
        hn = _rmsnorm(x_ref[0], ga_ref[...]).astype(BF16)
    else:
        v_ref, g_ref, gn_ref, s0_ref, o_ref, st_ref, decay_ref, qdec_ref, kdec_ref = rest
    @pl.when(pl.program_id(2) == 0)
    def _():
        n = lax.broadcasted_iota(jnp.int32, (blk, blk), 0)
        m = lax.broadcasted_iota(jnp.int32, (blk, blk), 1)
        shift = CHUNK.bit_length() - 1
        dist = jnp.abs(n - m).astype(F32)
        same_or_earlier_chunk = (m >> shift) <= (n >> shift)
        pos = lax.broadcasted_iota(jnp.int32, (blk, RET_KEY_DIM), 0).astype(F32)
        for h in range(heads):
            lg = lg_ref[h, 0:1, 0:1]
            st_ref[0, h] = s0_ref[0, h]
            decay_ref[h] = jnp.where(same_or_earlier_chunk, jnp.exp(lg * dist), 0.0)
            qdec_ref[h] = jnp.exp(lg * (pos + 1.0))
            kdec_ref[h] = jnp.exp(lg * (blk - 1.0 - pos))

    for h in range(heads):
        lg = lg_ref[h, 0:1, 0:1]
        keys = slice(h * RET_KEY_DIM, (h + 1) * RET_KEY_DIM)
        vals = slice(h * RET_VAL_DIM, (h + 1) * RET_VAL_DIM)
        q = q_ref[0, :, keys]
        k = k_ref[0, :, keys]
        if fused_vg:
            v = _dot(hn, w_ref[:, vals]).astype(BF16)
        else:
            v = v_ref[0, :, vals]
        state = st_ref[0, h]
        s = _dot_nt(q, k) * decay_ref[h]
        qd = (q.astype(F32) * qdec_ref[h]).astype(BF16)
        y = _dot(s.astype(BF16), v) + _dot(qd, state.astype(BF16))
        kd = k.astype(F32) * kdec_ref[h]
        st_ref[0, h] = jnp.exp(lg * blk) * state + _dot(kd.T.astype(BF16), v)

        yn = y * lax.rsqrt(jnp.mean(y * y, axis=-1, keepdims=True) + EPS) * gn_ref[h]
        if fused_vg:
            g = _dot(hn, w_ref[:, RET_V_WIDTH + h * RET_VAL_DIM:RET_V_WIDTH + (h + 1) * RET_VAL_DIM])
        else:
            g = g_ref[0, :, vals]
        o_ref[0, :, vals] = (g * jax.nn.sigmoid(g) * yn).astype(BF16)


def _retention(lg, q, k, vg, gn, s0, blk):
    b, t, _ = q.shape
    heads = RET_HEADS_PER_STEP
    fused_vg = len(vg) == 3
    assert not fused_vg or heads == N_RET_HEADS, "the fused projection expects every head in one step"
    qk = lambda i, h, j: (i, j, h)
    per_group = lambda i, h, j: (h, 0, 0)
    group_state = lambda i, h, j: (i, h, 0, 0)
    if fused_vg:
        vg_specs = [pl.BlockSpec((1, blk, D_MODEL), lambda i, h, j: (i, j, 0)),
                    pl.BlockSpec((1, D_MODEL), lambda i, h, j: (0, 0)),
                    _resident((D_MODEL, 2 * RET_V_WIDTH), lambda i, h, j: (0, 0))]
    else:
        vg_specs = [pl.BlockSpec((1, blk, heads * RET_VAL_DIM), qk), pl.BlockSpec((1, blk, heads * RET_VAL_DIM), qk)]
    return pl.pallas_call(
        functools.partial(_retention_kernel, blk=blk, heads=heads, fused_vg=fused_vg),
        grid=(b, N_RET_HEADS // heads, t // blk),
        in_specs=[pl.BlockSpec((heads, SUBLANES, LANES), per_group),
                  pl.BlockSpec((1, blk, heads * RET_KEY_DIM), qk), pl.BlockSpec((1, blk, heads * RET_KEY_DIM), qk),
                  *vg_specs,
                  pl.BlockSpec((heads, 1, RET_VAL_DIM), per_group),
                  pl.BlockSpec((1, heads, RET_KEY_DIM, RET_VAL_DIM), group_state)],
        out_specs=[pl.BlockSpec((1, blk, heads * RET_VAL_DIM), qk),
                   pl.BlockSpec((1, heads, RET_KEY_DIM, RET_VAL_DIM), group_state)],
        out_shape=[jax.ShapeDtypeStruct((b, t, RET_V_WIDTH), BF16),
                   jax.ShapeDtypeStruct((b, N_RET_HEADS, RET_KEY_DIM, RET_VAL_DIM), F32)],
        scratch_shapes=[pltpu.VMEM((heads, blk, blk), F32), pltpu.VMEM((heads, blk, RET_KEY_DIM), F32),
                        pltpu.VMEM((heads, blk, RET_KEY_DIM), F32)],
        compiler_params=_cparams(3), name="retention",
    )(lg, q, k, *vg, gn, s0)


def _mixer_kernel(x_ref, oa_ref, ob_ref, g_ref, wg_ref, wa_ref, wb_ref, wo_ref, x1_ref, *, oa_feature_major):
    x = x_ref[...]
    h = _rmsnorm(x, g_ref[...]).astype(BF16)
    gate_a = jax.nn.sigmoid(_dot(h, wg_ref[:, 0:D_MODEL]))
    gate_b = jax.nn.sigmoid(_dot(h, wg_ref[:, D_MODEL:2 * D_MODEL]))
    if oa_feature_major:
        ya = lax.dot_general(oa_ref[0], wa_ref[...], (((0,), (0,)), ((), ())), preferred_element_type=F32)
    else:
        ya = _dot(oa_ref[...], wa_ref[...])
    merged = gate_a * ya + gate_b * _dot(ob_ref[...], wb_ref[...])
    x1_ref[...] = x + _dot(merged.astype(BF16), wo_ref[...])


def _mixer(x, oa, ob, g, wg, wa, wb, wo):
    n = x.shape[0]
    tm = MIX_TOKEN_TILE
    tok = lambda i: (i, 0)
    fixed = lambda i: (0, 0)
    feature_major = oa.ndim == 3
    if feature_major:
        per_batch = oa.shape[2] // tm
        oa_spec = pl.BlockSpec((1, FOX_WIDTH, tm), lambda i: (i // per_batch, 0, i % per_batch))
    else:
        oa_spec = pl.BlockSpec((tm, FOX_WIDTH), tok)
    return pl.pallas_call(
        functools.partial(_mixer_kernel, oa_feature_major=feature_major),
        grid=(n // tm,),
        in_specs=[pl.BlockSpec((tm, D_MODEL), tok), oa_spec,
                  pl.BlockSpec((tm, RET_V_WIDTH), tok), pl.BlockSpec((1, D_MODEL), fixed),
                  _resident((D_MODEL, 2 * D_MODEL), fixed), _resident((FOX_WIDTH, D_MODEL), fixed),
                  _resident((RET_V_WIDTH, D_MODEL), fixed), _resident((D_MODEL, D_MODEL), fixed)],
        out_specs=pl.BlockSpec((tm, D_MODEL), tok),
        out_shape=jax.ShapeDtypeStruct((n, D_MODEL), F32),
        compiler_params=_cparams(1), name="mixer",
    )(x, oa, ob, g, wg, wa, wb, wo)


def _mlp_kernel(x_ref, g_ref, wu_ref, wd_ref, gf_ref, y_ref):
    x = x_ref[...]
    h = _rmsnorm(x, g_ref[...]).astype(BF16)
    acc = x
    for c in range(D_FF // D_MODEL):
        cols = slice(c * D_MODEL, (c + 1) * D_MODEL)
        u = jnp.square(jnp.maximum(_dot(h, wu_ref[:, cols]), 0.0)).astype(BF16)
        acc = acc + _dot(u, wd_ref[cols, :])
    y_ref[...] = _rmsnorm(acc, gf_ref[...])


def _mlp(x, g, wu, wd, gf):
    n = x.shape[0]
    tm = MIX_TOKEN_TILE
    tok = lambda i: (i, 0)
    fixed = lambda i: (0, 0)
    return pl.pallas_call(
        _mlp_kernel,
        grid=(n // tm,),
        in_specs=[pl.BlockSpec((tm, D_MODEL), tok), pl.BlockSpec((1, D_MODEL), fixed),
                  _resident((D_MODEL, D_FF), fixed), _resident((D_FF, D_MODEL), fixed),
                  pl.BlockSpec((1, D_MODEL), fixed)],
        out_specs=pl.BlockSpec((tm, D_MODEL), tok),
        out_shape=jax.ShapeDtypeStruct((n, D_MODEL), F32),
        compiler_params=_cparams(1), name="mlp",
    )(x, g, wu, wd, gf)


def _rope_tables(pos):
    inv_freq = ROPE_BASE ** (-jnp.arange(0, RET_KEY_DIM, 2, dtype=F32) / RET_KEY_DIM)
    ang = pos.astype(F32)[:, None] * inv_freq[None, :]
    return jnp.cos(ang), jnp.sin(ang)


def _after_attention(x, pos, o_a, params, ret_blk, state0, fuse_vg):
    b, t, _ = x.shape
    n = b * t
    xf = x.reshape(n, D_MODEL)
    cos, sin = _rope_tables(pos)
    reps = max(1, TOKEN_TILE // t)
    q_r, k_r = _proj_ret_qk(xf, params["g_attn"], params["w_ret_qk"], jnp.tile(cos, (reps, 1)),
                            jnp.tile(sin, (reps, 1)))
    if fuse_vg:
        vg = (x, params["g_attn"], params["w_ret_vg"])
    else:
        v_r, g_r = _proj_ret_vg(xf, params["g_attn"], params["w_ret_vg"])
        vg = (v_r.reshape(b, t, RET_V_WIDTH), g_r.reshape(b, t, RET_V_WIDTH))
    o_b, state = _retention(params["lg"], q_r.reshape(b, t, RET_QK_WIDTH), k_r.reshape(b, t, RET_QK_WIDTH),
                            vg, params["g_ret_norm"], state0, ret_blk)
    x1 = _mixer(xf, o_a, o_b.reshape(n, RET_V_WIDTH), params["g_attn"],
                params["w_gates"], params["w_branch_a"], params["w_branch_b"], params["w_out"])
    y = _mlp(x1, params["g_mlp"], params["w_up"], params["w_down"], params["g_final"])
    return y.reshape(b, t, D_MODEL), state[None]


def kernel(x_prompt, x_sample, cache_fox_k, cache_fox_v, cache_fox_logf, state_ret, g_attn, w_in, b_forget,
           g_ret_norm, w_branch, w_out, g_mlp, w_up, w_down, g_final):
    assert w_in.shape[0] == 1, "single-layer trunk"
    wi = w_in[0]
    fox_cols = 3 * FOX_WIDTH + N_FOX_HEADS
    qk_end = fox_cols + 2 * RET_QK_WIDTH
    vg_end = qk_end + 2 * RET_V_WIDTH
    lane_pad = LANES - N_FOX_HEADS
    lg = jnp.log(1.0 - 2.0 ** (-5.0 - jnp.arange(N_RET_HEADS, dtype=F32)))
    w_fox = jnp.pad(wi[:, :fox_cols], ((0, 0), (0, lane_pad))).astype(BF16)
    b_row = jnp.pad(b_forget[0], (0, lane_pad)).reshape(1, LANES)
    params = {
        "g_attn": g_attn[0].reshape(1, D_MODEL),
        "w_ret_qk": wi[:, fox_cols:qk_end].astype(BF16),
        "w_ret_vg": wi[:, qk_end:vg_end].astype(BF16),
        "w_gates": wi[:, vg_end:].astype(BF16),
        "lg": jnp.broadcast_to(lg[:, None, None], (N_RET_HEADS, SUBLANES, LANES)),
        "g_ret_norm": g_ret_norm[0].reshape(N_RET_HEADS, 1, RET_VAL_DIM),
        "w_branch_a": w_branch[0, :FOX_WIDTH].astype(BF16),
        "w_branch_b": w_branch[0, FOX_WIDTH:].astype(BF16),
        "w_out": w_out[0].astype(BF16),
        "g_mlp": g_mlp[0].reshape(1, D_MODEL),
        "w_up": w_up[0].astype(BF16),
        "w_down": w_down[0].astype(BF16),
        "g_final": g_final.reshape(1, D_MODEL),
    }
    bp, tp, _ = x_prompt.shape
    bs, ts, _ = x_sample.shape
    past = cache_fox_k.shape[2]

    qt, kt, vt, lft, lf_pad, q_norm, k_norm, kb = _proj_fox_t(
        x_prompt, params["g_attn"], wi[:, :fox_cols].T.astype(BF16), w_fox[:, 3 * FOX_WIDTH:],
        jnp.broadcast_to(b_forget[0][:, None], (N_FOX_HEADS, LANES)), b_row)
    c_pieces, c_end = _cumsum_pieces(lf_pad, CUMSUM_TILE, LOG2E)
    first = _first_key_block(c_end, q_norm, k_norm, tp, FOX_Q_BLOCK, FOX_BLOCK)
    o_a = _fox_prompt(first, qt, kb, vt, c_pieces)
    zero_state = jnp.zeros((bp, N_RET_HEADS, RET_KEY_DIM, RET_VAL_DIM), F32)
    yp, sp = _after_attention(x_prompt, jnp.arange(tp), o_a, params, RET_BLOCK, zero_state, True)
    to_heads = lambda a: jnp.transpose(a.reshape(bp, N_FOX_HEADS, FOX_HEAD_DIM, tp), (0, 3, 1, 2))[None]
    kp, vp = to_heads(kt), to_heads(vt)
    fp = jnp.transpose(lft, (0, 2, 1))[None]

    q, k, v, lf = _proj_fox(x_sample.reshape(bs * ts, D_MODEL), params["g_attn"], w_fox, b_row)
    feature_major = lambda a: jnp.transpose(a[0], (0, 2, 3, 1)).reshape(bs, FOX_WIDTH, past)
    lf_time = jnp.concatenate([jnp.transpose(cache_fox_logf[0], (0, 2, 1)),
                               jnp.transpose(lf.reshape(bs, ts, N_FOX_HEADS), (0, 2, 1))], axis=2)
    lf_time = jnp.pad(lf_time, ((0, 0), (0, 0), (0, -(past + ts) % LANES)))
    o_a = _fox_sample(q.reshape(bs, ts, FOX_WIDTH), k.reshape(bs, ts, FOX_WIDTH), v.reshape(bs, ts, FOX_WIDTH),
                      feature_major(cache_fox_k), feature_major(cache_fox_v), _cumsum_lanes(lf_time, LOG2E))
    ys, ss = _after_attention(x_sample, past + jnp.arange(ts), o_a.reshape(bs * ts, FOX_WIDTH), params, CHUNK,
                              state_ret[0], False)
    shape5 = (1, bs, ts, N_FOX_HEADS, FOX_HEAD_DIM)
    return (yp, ys, kp, vp, fp, sp, k.reshape(shape5), v.reshape(shape5),
            lf.reshape(1, bs, ts, N_FOX_HEADS), ss)
```

```python
import functools

import jax
import jax.numpy as jnp
from jax import lax
from jax.experimental import pallas as pl
from jax.experimental.pallas import tpu as pltpu

D_MODEL = 1024
N_FOX_HEADS = 16
FOX_HEAD_DIM = 64
FOX_WIDTH = N_FOX_HEADS * FOX_HEAD_DIM
N_RET_HEADS = 4
RET_KEY_DIM = 256
RET_VAL_DIM = 512
RET_QK_WIDTH = N_RET_HEADS * RET_KEY_DIM
RET_V_WIDTH = N_RET_HEADS * RET_VAL_DIM
D_FF = 4 * D_MODEL
CHUNK = 64
ROPE_BASE = 10000.0
EPS = 1e-6

LANES = 128
SUBLANES = 8
HEADS_PER_LANE_BLOCK = LANES // FOX_HEAD_DIM
N_HEAD_PAIRS = N_FOX_HEADS // HEADS_PER_LANE_BLOCK
N_C_PIECES = 3
ONES_ROWS = 16
MASK_VALUE = -1e30
LOG2E = 1.4426950408889634
VMEM_LIMIT = 56 * 1024 * 1024

TOKEN_TILE = 512
MIX_TOKEN_TILE = 512
FOX_BLOCK = 256
FOX_Q_BLOCK = 512
SAMPLE_PAIRS_PER_STEP = 8
RET_BLOCK = 256
RET_HEADS_PER_STEP = 4
CUMSUM_TILE = 512
SKIP_EXPONENT = 160.0
NORM_SLACK = 1.01

BF16 = jnp.bfloat16
F32 = jnp.float32


def _cparams(n_axes):
    return pltpu.CompilerParams(dimension_semantics=("arbitrary",) * n_axes,
                                vmem_limit_bytes=VMEM_LIMIT)


def _resident(shape, index_map):
    return pl.BlockSpec(shape, index_map, pipeline_mode=pl.Buffered(1))


def _dot(a, b):
    return jnp.dot(a, b, preferred_element_type=F32)


def _dot_nt(a, b):
    return lax.dot_general(a, b, (((1,), (1,)), ((), ())), preferred_element_type=F32)


def _rmsnorm(x, g):
    return x * lax.rsqrt(jnp.mean(x * x, axis=-1, keepdims=True) + EPS) * g


def _log_sigmoid(z):
    return -(jnp.maximum(-z, 0.0) + jnp.log(1.0 + jnp.exp(-jnp.abs(z))))


def _split3(x):
    a = x.astype(BF16)
    r = x - a.astype(F32)
    b = r.astype(BF16)
    c = (r - b.astype(F32)).astype(BF16)
    return a, b, c


def _indicator(idx, lo, hi):
    return jnp.where(idx >= lo, jnp.where(idx < hi, 1.0, 0.0), 0.0)


def _proj_fox_t_kernel(x_ref, g_ref, wt_ref, wf_ref, bcol_ref, brow_ref, qt_ref, kt_ref, vt_ref, lft_ref, lfp_ref,
                       qn_ref, kn_ref, kb_ref):
    h = _rmsnorm(x_ref[0], g_ref[...])
    ht = h.T.astype(BF16)
    w = FOX_WIDTH
    qs = _dot(wt_ref[0:w, :], ht) * (FOX_HEAD_DIM ** -0.5 * LOG2E)
    kf = _dot(wt_ref[w:2 * w, :], ht)
    qt_ref[0] = qs.astype(BF16)
    kt_ref[0] = kf
    kb_ref[0] = kf.T.astype(BF16)
    for src, dst in ((qs, qn_ref), (kf, kn_ref)):
        sq = jnp.sum((src * src).reshape(N_FOX_HEADS, FOX_HEAD_DIM, src.shape[1]), axis=1)
        dst[0, 0] = jnp.broadcast_to(jnp.sqrt(jnp.max(sq, axis=1, keepdims=True)), (N_FOX_HEADS, LANES))
    vt_ref[0] = _dot(wt_ref[2 * w:3 * w, :], ht)
    lft_ref[0] = _log_sigmoid(_dot(wt_ref[3 * w:3 * w + N_FOX_HEADS, :], ht) + bcol_ref[:, 0:1])
    lfp_ref[0] = _log_sigmoid(_dot(h.astype(BF16), wf_ref[...]) + brow_ref[...])


def _proj_fox_t(x, g, wt, wf, bcol, brow):
    b, t, _ = x.shape
    tm = TOKEN_TILE
    fixed = lambda i, j: (0, 0)
    feat = lambda i, j: (i, 0, j)
    return pl.pallas_call(
        _proj_fox_t_kernel,
        grid=(b, t // tm),
        in_specs=[pl.BlockSpec((1, tm, D_MODEL), lambda i, j: (i, j, 0)), pl.BlockSpec((1, D_MODEL), fixed),
                  pl.BlockSpec(wt.shape, fixed), pl.BlockSpec(wf.shape, fixed),
                  pl.BlockSpec(bcol.shape, fixed), pl.BlockSpec(brow.shape, fixed)],
        out_specs=[pl.BlockSpec((1, FOX_WIDTH, tm), feat), pl.BlockSpec((1, FOX_WIDTH, tm), feat),
                   pl.BlockSpec((1, FOX_WIDTH, tm), feat), pl.BlockSpec((1, N_FOX_HEADS, tm), feat),
                   pl.BlockSpec((1, tm, LANES), lambda i, j: (i, j, 0)),
                   pl.BlockSpec((1, 1, N_FOX_HEADS, LANES), lambda i, j: (i, j, 0, 0)),
                   pl.BlockSpec((1, 1, N_FOX_HEADS, LANES), lambda i, j: (i, j, 0, 0)),
                   pl.BlockSpec((1, tm, FOX_WIDTH), lambda i, j: (i, j, 0))],
        out_shape=[jax.ShapeDtypeStruct((b, FOX_WIDTH, t), BF16), jax.ShapeDtypeStruct((b, FOX_WIDTH, t), F32),
                   jax.ShapeDtypeStruct((b, FOX_WIDTH, t), F32), jax.ShapeDtypeStruct((b, N_FOX_HEADS, t), F32),
                   jax.ShapeDtypeStruct((b, t, LANES), F32),
                   jax.ShapeDtypeStruct((b, t // tm, N_FOX_HEADS, LANES), F32),
                   jax.ShapeDtypeStruct((b, t // tm, N_FOX_HEADS, LANES), F32),
                   jax.ShapeDtypeStruct((b, t, FOX_WIDTH), BF16)],
        compiler_params=_cparams(2), name="proj_fox_t",
    )(x, g, wt, wf, bcol, brow)


def _proj_fox_kernel(x_ref, g_ref, w_ref, b_ref, q_ref, k_ref, v_ref, lf_ref):
    h = _rmsnorm(x_ref[...], g_ref[...]).astype(BF16)
    w = FOX_WIDTH
    q_ref[...] = (_dot(h, w_ref[:, 0:w]) * (FOX_HEAD_DIM ** -0.5 * LOG2E)).astype(BF16)
    k_ref[...] = _dot(h, w_ref[:, w:2 * w])
    v_ref[...] = _dot(h, w_ref[:, 2 * w:3 * w])
    lf = _log_sigmoid(_dot(h, w_ref[:, 3 * w:3 * w + LANES]) + b_ref[...])
    lf_ref[...] = lf[:, 0:N_FOX_HEADS]


def _proj_fox(x, g, w, b):
    n = x.shape[0]
    tm = TOKEN_TILE
    tok = lambda i: (i, 0)
    fixed = lambda i: (0, 0)
    return pl.pallas_call(
        _proj_fox_kernel,
        grid=(n // tm,),
        in_specs=[pl.BlockSpec((tm, D_MODEL), tok), pl.BlockSpec((1, D_MODEL), fixed),
                  pl.BlockSpec(w.shape, fixed), pl.BlockSpec((1, LANES), fixed)],
        out_specs=[pl.BlockSpec((tm, FOX_WIDTH), tok), pl.BlockSpec((tm, FOX_WIDTH), tok),
                   pl.BlockSpec((tm, FOX_WIDTH), tok), pl.BlockSpec((tm, N_FOX_HEADS), tok)],
        out_shape=[jax.ShapeDtypeStruct((n, FOX_WIDTH), BF16), jax.ShapeDtypeStruct((n, FOX_WIDTH), F32),
                   jax.ShapeDtypeStruct((n, FOX_WIDTH), F32), jax.ShapeDtypeStruct((n, N_FOX_HEADS), F32)],
        compiler_params=_cparams(1), name="proj_fox",
    )(x, g, w, b)


def _proj_ret_qk_kernel(x_ref, g_ref, w_ref, cos_ref, sin_ref, q_ref, k_ref):
    h = _rmsnorm(x_ref[...], g_ref[...]).astype(BF16)
    cos = cos_ref[...]
    sin = sin_ref[...]
    half = RET_KEY_DIM // 2
    for out_ref, base, scale in ((q_ref, 0, 1.0), (k_ref, RET_QK_WIDTH, RET_KEY_DIM ** -0.5)):
        for hd in range(N_RET_HEADS):
            lo = hd * RET_KEY_DIM
            z = _dot(h, w_ref[:, base + lo:base + lo + RET_KEY_DIM])
            x1 = z[:, :half]
            x2 = z[:, half:]
            out_ref[:, lo:lo + half] = ((x1 * cos - x2 * sin) * scale).astype(BF16)
            out_ref[:, lo + half:lo + RET_KEY_DIM] = ((x1 * sin + x2 * cos) * scale).astype(BF16)


def _proj_ret_qk(x, g, w, cos, sin):
    n = x.shape[0]
    tm = TOKEN_TILE
    period = cos.shape[0] // tm
    tok = lambda i: (i, 0)
    fixed = lambda i: (0, 0)
    pos = lambda i: (i % period, 0)
    half = RET_KEY_DIM // 2
    return pl.pallas_call(
        _proj_ret_qk_kernel,
        grid=(n // tm,),
        in_specs=[pl.BlockSpec((tm, D_MODEL), tok), pl.BlockSpec((1, D_MODEL), fixed),
                  pl.BlockSpec((D_MODEL, 2 * RET_QK_WIDTH), fixed),
                  pl.BlockSpec((tm, half), pos), pl.BlockSpec((tm, half), pos)],
        out_specs=[pl.BlockSpec((tm, RET_QK_WIDTH), tok), pl.BlockSpec((tm, RET_QK_WIDTH), tok)],
        out_shape=[jax.ShapeDtypeStruct((n, RET_QK_WIDTH), BF16)] * 2,
        compiler_params=_cparams(1), name="proj_ret_qk",
    )(x, g, w, cos, sin)


def _proj_ret_vg_kernel(x_ref, g_ref, w_ref, v_ref, gr_ref):
    h = _rmsnorm(x_ref[...], g_ref[...]).astype(BF16)
    v_ref[...] = _dot(h, w_ref[:, 0:RET_V_WIDTH]).astype(BF16)
    gr_ref[...] = _dot(h, w_ref[:, RET_V_WIDTH:2 * RET_V_WIDTH])


def _proj_ret_vg(x, g, w):
    n = x.shape[0]
    tm = TOKEN_TILE
    tok = lambda i: (i, 0)
    fixed = lambda i: (0, 0)
    return pl.pallas_call(
        _proj_ret_vg_kernel,
        grid=(n // tm,),
        in_specs=[pl.BlockSpec((tm, D_MODEL), tok), pl.BlockSpec((1, D_MODEL), fixed),
                  pl.BlockSpec((D_MODEL, 2 * RET_V_WIDTH), fixed)],
        out_specs=[pl.BlockSpec((tm, RET_V_WIDTH), tok), pl.BlockSpec((tm, RET_V_WIDTH), tok)],
        out_shape=[jax.ShapeDtypeStruct((n, RET_V_WIDTH), BF16), jax.ShapeDtypeStruct((n, RET_V_WIDTH), F32)],
        compiler_params=_cparams(1), name="proj_ret_vg",
    )(x, g, w)


def _cumsum_kernel(lf_ref, c_ref, cend_ref, carry_ref, *, tb, scale):
    @pl.when(pl.program_id(1) == 0)
    def _():
        carry_ref[...] = jnp.zeros_like(carry_ref)

    row = lax.broadcasted_iota(jnp.int32, (tb, tb), 0)
    col = lax.broadcasted_iota(jnp.int32, (tb, tb), 1)
    tri = jnp.where(col <= row, 1.0, 0.0).astype(BF16)
    c = carry_ref[0:1, :]
    for piece in _split3(lf_ref[0]):
        c = c + _dot(tri, piece)
    carry_ref[...] = jnp.broadcast_to(c[tb - 1:tb, :], carry_ref.shape)
    erow = lax.broadcasted_iota(jnp.int32, (SUBLANES, LANES), 0)
    ends = jnp.zeros((SUBLANES, LANES), F32)
    for e in range(tb // FOX_BLOCK):
        ends = jnp.where(erow == e, c[(e + 1) * FOX_BLOCK - 1:(e + 1) * FOX_BLOCK, :] * scale, ends)
    cend_ref[0, 0] = ends

    prow = lax.broadcasted_iota(jnp.int32, (LANES, LANES), 0)
    pcol = lax.broadcasted_iota(jnp.int32, (LANES, LANES), 1)
    out = jnp.zeros((tb, LANES), F32)
    for p, piece in enumerate(_split3(-c * scale)):
        place = jnp.where(pcol == prow + p * N_FOX_HEADS, jnp.where(prow < N_FOX_HEADS, 1.0, 0.0), 0.0)
        out = out + _dot(piece, place.astype(BF16))
    c_ref[0] = out.astype(BF16)


def _cumsum_pieces(lf, tb, scale):
    b, t, _ = lf.shape
    blk = lambda i, j: (i, j, 0)
    return pl.pallas_call(
        functools.partial(_cumsum_kernel, tb=tb, scale=scale),
        grid=(b, t // tb),
        in_specs=[pl.BlockSpec((1, tb, LANES), blk)],
        out_specs=[pl.BlockSpec((1, tb, LANES), blk), pl.BlockSpec((1, 1, SUBLANES, LANES), lambda i, j: (i, j, 0, 0))],
        out_shape=[jax.ShapeDtypeStruct((b, t, LANES), BF16), jax.ShapeDtypeStruct((b, t // tb, SUBLANES, LANES), F32)],
        scratch_shapes=[pltpu.VMEM((SUBLANES, LANES), F32)],
        compiler_params=_cparams(2), name="cumsum",
    )(lf)


def _cumsum_lanes_kernel(lf_ref, c_ref, *, scale):
    heads, t = lf_ref.shape[1], lf_ref.shape[2]
    row = lax.broadcasted_iota(jnp.int32, (LANES, LANES), 0)
    col = lax.broadcasted_iota(jnp.int32, (LANES, LANES), 1)
    tri = jnp.where(row <= col, 1.0, 0.0).astype(BF16)
    ones = jnp.ones((LANES, LANES), BF16)
    carry = jnp.zeros((heads, LANES), F32)
    for seg in range(t // LANES):
        lanes = slice(seg * LANES, (seg + 1) * LANES)
        stack = jnp.concatenate(_split3(lf_ref[0, :, lanes]), axis=0)
        within = _dot(stack, tri)
        total = _dot(stack, ones)
        fold = lambda a: a[0:heads] + a[heads:2 * heads] + a[2 * heads:3 * heads]
        c_ref[0, :, lanes] = (fold(within) + carry) * scale
        carry = carry + fold(total)


def _cumsum_lanes(lf, scale):
    b, heads, t = lf.shape
    blk = lambda i: (i, 0, 0)
    return pl.pallas_call(
        functools.partial(_cumsum_lanes_kernel, scale=scale),
        grid=(b,),
        in_specs=[pl.BlockSpec((1, heads, t), blk)],
        out_specs=pl.BlockSpec((1, heads, t), blk),
        out_shape=jax.ShapeDtypeStruct((b, heads, t), F32),
        compiler_params=_cparams(1), name="cumsum_lanes",
    )(lf)


def _piece_selector(shape, axis, head):
    idx = lax.broadcasted_iota(jnp.int32, shape, axis)
    hit = jnp.where(idx < N_C_PIECES * N_FOX_HEADS, jnp.where((idx & (N_FOX_HEADS - 1)) == head, 1.0, 0.0), 0.0)
    return hit


_ROW_M, _ROW_MX, _ROW_ALPHA = 0, 1, 2


def _fox_prompt_kernel(first_ref, qt_ref, kb_ref, vt_ref, c_ref, o_ref, qa_ref, s_ref, p_ref, acc_ref,
                       stat_ref, *, t, tq, blk):
    pair = pl.program_id(1)
    qi = pl.program_id(2)
    diag_blocks = tq // blk
    d = FOX_HEAD_DIM
    heads = range(HEADS_PER_LANE_BLOCK)

    qt = qt_ref[0]
    row = lax.broadcasted_iota(jnp.int32, (LANES, tq), 0)
    for h in heads:
        in_head = _indicator(row, h * d, (h + 1) * d).astype(BF16)
        sel = _piece_selector((LANES, tq), 0, pair * HEADS_PER_LANE_BLOCK + h).astype(BF16)
        qa_ref[h] = jnp.concatenate([qt * in_head, sel], axis=0)

    acc_ref[...] = jnp.zeros_like(acc_ref)
    stat_ref[...] = jnp.full(stat_ref.shape, MASK_VALUE, F32)
    krow = lax.broadcasted_iota(jnp.int32, (blk, tq), 0)
    qcol = lax.broadcasted_iota(jnp.int32, (blk, tq), 1)

    def stat(h, slot, kind):
        r = (h * 2 + slot) * 4 + kind
        return slice(r, r + 1)

    def stage_a(j, slot, diag):
        off = pl.multiple_of(j * blk, blk)
        kb = jnp.concatenate([kb_ref[0, pl.ds(off, blk), :], c_ref[0, pl.ds(off, blk), :]], axis=1)
        for h in heads:
            s = _dot(kb, qa_ref[h])
            if diag is not None:
                s = jnp.where(krow + diag * blk <= qcol, s, MASK_VALUE)
            s_ref[slot, h] = s
            stat_ref[stat(h, slot, _ROW_MX), :] = jnp.max(s, axis=0, keepdims=True)

    def stage_b(slot):
        for h in heads:
            m_old = stat_ref[stat(h, 0, _ROW_M), :]
            m_new = jnp.maximum(m_old, stat_ref[stat(h, slot, _ROW_MX), :])
            stat_ref[stat(h, slot, _ROW_ALPHA), :] = jnp.exp2(m_old - m_new)
            stat_ref[stat(h, 0, _ROW_M), :] = m_new
            p_ref[slot, h] = jnp.exp2(s_ref[slot, h] - m_new).astype(BF16)

    ones = jnp.ones((ONES_ROWS, blk), BF16)

    def stage_c(j, slot):
        off = pl.multiple_of(j * blk, blk)
        for h in heads:
            alpha = stat_ref[stat(h, slot, _ROW_ALPHA), :]
            va = jnp.concatenate([vt_ref[0, h * d:(h + 1) * d, pl.ds(off, blk)].astype(BF16), ones], axis=0)
            acc_ref[h] = acc_ref[h] * alpha + _dot(va, p_ref[slot, h])

    nfull = qi * diag_blocks
    j0 = first_ref[(pl.program_id(0) * N_HEAD_PAIRS + pair) * (t // tq) + qi]

    @pl.when(qi > 0)
    def _pipelined():
        stage_a(j0, 0, None)
        stage_b(0)
        stage_a(j0 + 1, 1, None)

        def two_steps(j):
            stage_c(j - 2, 0)
            stage_b(1)
            stage_a(j, 0, None)
            stage_c(j - 1, 1)
            stage_b(0)
            stage_a(j + 1, 1, None)

        def body(u, carry):
            two_steps(j0 + 2 + 4 * u)
            two_steps(j0 + 4 + 4 * u)
            return carry

        pairs_left = (nfull - j0) // 2 - 1
        lax.fori_loop(0, pairs_left // 2, body, 0)

        @pl.when(pairs_left % 2 == 1)
        def _odd_group():
            two_steps(j0 + 2 * pairs_left)
        for dg in range(diag_blocks):
            stage_c(nfull + dg - 2, dg % 2)
            stage_b((dg + 1) % 2)
            stage_a(nfull + dg, dg % 2, dg)
        stage_c(nfull + diag_blocks - 2, 0)
        stage_b(1)
        stage_c(nfull + diag_blocks - 1, 1)

    @pl.when(qi == 0)
    def _first_block():
        for dg in range(diag_blocks):
            stage_a(dg, dg % 2, dg)
            stage_b(dg % 2)
            stage_c(dg, dg % 2)

    for h in heads:
        a = acc_ref[h]
        o_ref[0, h * d:(h + 1) * d, :] = (a[0:d] / a[d:d + 1]).astype(BF16)


def _first_key_block(c_end, q_norm, k_norm, t, tq, blk):
    b = c_end.shape[0]
    nblk, nq, per = t // blk, t // tq, tq // blk
    ce = c_end[:, :, :CUMSUM_TILE // blk, :N_FOX_HEADS].reshape(b, nblk, N_FOX_HEADS)
    qn = q_norm[:, :, :, 0]
    kmax = jnp.max(k_norm[:, :, :, 0], axis=1)
    before = jnp.concatenate([jnp.zeros((b, 1, N_FOX_HEADS), F32), ce[:, per - 1:-1:per]], axis=1)
    decay = before[:, :, None, :] - ce[:, None, :, :]
    bound = (2.0 * NORM_SLACK * NORM_SLACK) * (qn * kmax[:, None, :])[:, :, None, :] + decay
    below = (jnp.arange(nblk)[None, :] < (jnp.arange(nq) * per)[:, None])[None, :, :, None]
    skippable = jnp.logical_and(bound <= -SKIP_EXPONENT, below)
    block = jnp.arange(nblk, dtype=jnp.int32)[None, None, :, None]
    n_skip = jnp.min(jnp.where(skippable, nblk, block), axis=2)
    n_skip = jnp.min(n_skip.reshape(b, nq, N_HEAD_PAIRS, HEADS_PER_LANE_BLOCK), axis=3)
    first = jnp.clip((n_skip // 2) * 2, 0, jnp.maximum(jnp.arange(nq) * per - 2, 0)[None, :, None])
    return jnp.transpose(first, (0, 2, 1)).reshape(-1).astype(jnp.int32)


def _fox_prompt(first, qt, kb, vt, c):
    b, _, t = qt.shape
    blk = FOX_BLOCK
    tq = FOX_Q_BLOCK
    assert tq == 2 * blk, "the pipeline keeps two key blocks in flight and slot = block parity"
    hp = HEADS_PER_LANE_BLOCK
    acc_rows = FOX_HEAD_DIM + ONES_ROWS
    per_pair_all_time = lambda i, p, j, first: (i, p, 0)
    grid_spec = pltpu.PrefetchScalarGridSpec(
        num_scalar_prefetch=1,
        grid=(b, N_HEAD_PAIRS, t // tq),
        in_specs=[pl.BlockSpec((1, LANES, tq), lambda i, p, j, first: (i, p, j)),
                  pl.BlockSpec((1, t, LANES), lambda i, p, j, first: (i, 0, p)),
                  pl.BlockSpec((1, LANES, t), per_pair_all_time),
                  pl.BlockSpec((1, t, LANES), lambda i, p, j, first: (i, 0, 0))],
        out_specs=pl.BlockSpec((1, LANES, tq), lambda i, p, j, first: (i, p, j)),
        scratch_shapes=[pltpu.VMEM((hp, 2 * LANES, tq), BF16),
                        pltpu.VMEM((2, hp, blk, tq), F32),
                        pltpu.VMEM((2, hp, blk, tq), BF16),
                        pltpu.VMEM((hp, acc_rows, tq), F32),
                        pltpu.VMEM((hp * 2 * 4, tq), F32)])
    return pl.pallas_call(
        functools.partial(_fox_prompt_kernel, t=t, tq=tq, blk=blk),
        grid_spec=grid_spec,
        out_shape=jax.ShapeDtypeStruct((b, FOX_WIDTH, t), BF16),
        compiler_params=_cparams(3), name="fox_prompt",
    )(first, qt, kb, vt, c)


def _fox_sample_kernel(q_ref, kn_ref, vn_ref, kc_ref, vc_ref, c_ref, o_ref, *, past, new, pairs):
    group = pl.program_id(1)
    d = FOX_HEAD_DIM
    rows = HEADS_PER_LANE_BLOCK * new
    lane = lax.broadcasted_iota(jnp.int32, (rows, LANES), 1)
    qrow = lax.broadcasted_iota(jnp.int32, (rows, LANES), 0)
    head_lanes = jnp.where(qrow < new, _indicator(lane, 0, d), _indicator(lane, d, 2 * d)).astype(BF16)
    out_lane = lax.broadcasted_iota(jnp.int32, (new, LANES), 1)
    nrow = lax.broadcasted_iota(jnp.int32, (rows, new), 0)
    ncol = lax.broadcasted_iota(jnp.int32, (rows, new), 1)
    causal = ncol <= (nrow & (new - 1))
    for pp in range(pairs):
        lanes = slice(pp * LANES, (pp + 1) * LANES)
        head0 = (group * pairs + pp) * HEADS_PER_LANE_BLOCK
        tpad = c_ref.shape[2]
        bias = jnp.concatenate(
            [jnp.broadcast_to(c_ref[0, pl.ds(head0 + h, 1), :], (new, tpad)) for h in range(HEADS_PER_LANE_BLOCK)],
            axis=0)
        q = q_ref[0, :, lanes]
        q2 = jnp.concatenate([q, q], axis=0) * head_lanes
        s_c = _dot(q2, kc_ref[0, lanes, :].astype(BF16)) - bias[:, 0:past]
        s_n = _dot_nt(q2, kn_ref[0, :, lanes].astype(BF16)) - bias[:, past:past + new]
        s_n = jnp.where(causal, s_n, MASK_VALUE)
        m = jnp.maximum(jnp.max(s_c, axis=-1, keepdims=True), jnp.max(s_n, axis=-1, keepdims=True))
        e_c = jnp.exp2(s_c - m)
        e_n = jnp.exp2(s_n - m)
        l = jnp.sum(e_c, axis=-1, keepdims=True) + jnp.sum(e_n, axis=-1, keepdims=True)
        o2 = _dot_nt(e_c.astype(BF16), vc_ref[0, lanes, :].astype(BF16))
        o2 = (o2 + _dot(e_n.astype(BF16), vn_ref[0, :, lanes].astype(BF16))) / l
        o = jnp.where(out_lane < d, o2[0:new], o2[new:rows])
        o_ref[0, :, lanes] = o.astype(BF16)


def _fox_sample(q, kn, vn, kct, vct, c):
    b, new, _ = q.shape
    past = kct.shape[2]
    pairs = SAMPLE_PAIRS_PER_STEP
    width = pairs * LANES
    tok = lambda i, g: (i, 0, g)
    feat = lambda i, g: (i, g, 0)
    return pl.pallas_call(
        functools.partial(_fox_sample_kernel, past=past, new=new, pairs=pairs),
        grid=(b, N_HEAD_PAIRS // pairs),
        in_specs=[pl.BlockSpec((1, new, width), tok), pl.BlockSpec((1, new, width), tok),
                  pl.BlockSpec((1, new, width), tok), pl.BlockSpec((1, width, past), feat),
                  pl.BlockSpec((1, width, past), feat),
                  pl.BlockSpec((1,) + c.shape[1:], lambda i, g: (i, 0, 0))],
        out_specs=pl.BlockSpec((1, new, width), tok),
        out_shape=jax.ShapeDtypeStruct((b, new, FOX_WIDTH), BF16),
        compiler_params=_cparams(2), name="fox_sample",
    )(q, kn, vn, kct, vct, c)


def _retention_kernel(lg_ref, q_ref, k_ref, *rest, blk, heads, fused_vg):
    if fused_vg:
        x_ref, ga_ref, w_ref, gn_ref, s0_ref, o_ref, st_ref, decay_ref, qdec_ref, kdec_ref = rest
        hn = _rmsnorm(x_ref[0], ga_ref[...]).astype(BF16)
    else:
        v_ref, g_ref, gn_ref, s0_ref, o_ref, st_ref, decay_ref, qdec_ref, kdec_ref = rest
    @pl.when(pl.program_id(2) == 0)
    def _():
        n = lax.broadcasted_iota(jnp.int32, (blk, blk), 0)
        m = lax.broadcasted_iota(jnp.int32, (blk, blk), 1)
        shift = CHUNK.bit_length() - 1
        dist = jnp.abs(n - m).astype(F32)
        same_or_earlier_chunk = (m >> shift) <= (n >> shift)
        pos = lax.broadcasted_iota(jnp.int32, (blk, RET_KEY_DIM), 0).astype(F32)
        for h in range(heads):
            lg = lg_ref[h, 0:1, 0:1]
            st_ref[0, h] = s0_ref[0, h]
            decay_ref[h] = jnp.where(same_or_earlier_chunk, jnp.exp(lg * dist), 0.0)
            qdec_ref[h] = jnp.exp(lg * (pos + 1.0))
            kdec_ref[h] = jnp.exp(lg * (blk - 1.0 - pos))

    for h in range(heads):
        lg = lg_ref[h, 0:1, 0:1]
        keys = slice(h * RET_KEY_DIM, (h + 1) * RET_KEY_DIM)
        vals = slice(h * RET_VAL_DIM, (h + 1) * RET_VAL_DIM)
        q = q_ref[0, :, keys]
        k = k_ref[0, :, keys]
        if fused_vg:
            v = _dot(hn, w_ref[:, vals]).astype(BF16)
        else:
            v = v_ref[0, :, vals]
        state = st_ref[0, h]
        s = _dot_nt(q, k) * decay_ref[h]
        qd = (q.astype(F32) * qdec_ref[h]).astype(BF16)
        y = _dot(s.astype(BF16), v) + _dot(qd, state.astype(BF16))
        kd = k.astype(F32) * kdec_ref[h]
        st_ref[0, h] = jnp.exp(lg * blk) * state + _dot(kd.T.astype(BF16), v)

        yn = y * lax.rsqrt(jnp.mean(y * y, axis=-1, keepdims=True) + EPS) * gn_ref[h]
        if fused_vg:
            g = _dot(hn, w_ref[:, RET_V_WIDTH + h * RET_VAL_DIM:RET_V_WIDTH + (h + 1) * RET_VAL_DIM])
        else:
            g = g_ref[0, :, vals]
        o_ref[0, :, vals] = (g * jax.nn.sigmoid(g) * yn).astype(BF16)


def _retention(lg, q, k, vg, gn, s0, blk):
    b, t, _ = q.shape
    heads = RET_HEADS_PER_STEP
    fused_vg = len(vg) == 3
    assert not fused_vg or heads == N_RET_HEADS, "the fused projection expects every head in one step"
    qk = lambda i, h, j: (i, j, h)
    per_group = lambda i, h, j: (h, 0, 0)
    group_state = lambda i, h, j: (i, h, 0, 0)
    if fused_vg:
        vg_specs = [pl.BlockSpec((1, blk, D_MODEL), lambda i, h, j: (i, j, 0)),
                    pl.BlockSpec((1, D_MODEL), lambda i, h, j: (0, 0)),
                    _resident((D_MODEL, 2 * RET_V_WIDTH), lambda i, h, j: (0, 0))]
    else:
        vg_specs = [pl.BlockSpec((1, blk, heads * RET_VAL_DIM), qk), pl.BlockSpec((1, blk, heads * RET_VAL_DIM), qk)]
    return pl.pallas_call(
        functools.partial(_retention_kernel, blk=blk, heads=heads, fused_vg=fused_vg),
        grid=(b, N_RET_HEADS // heads, t // blk),
        in_specs=[pl.BlockSpec((heads, SUBLANES, LANES), per_group),
                  pl.BlockSpec((1, blk, heads * RET_KEY_DIM), qk), pl.BlockSpec((1, blk, heads * RET_KEY_DIM), qk),
                  *vg_specs,
                  pl.BlockSpec((heads, 1, RET_VAL_DIM), per_group),
                  pl.BlockSpec((1, heads, RET_KEY_DIM, RET_VAL_DIM), group_state)],
        out_specs=[pl.BlockSpec((1, blk, heads * RET_VAL_DIM), qk),
                   pl.BlockSpec((1, heads, RET_KEY_DIM, RET_VAL_DIM), group_state)],
        out_shape=[jax.ShapeDtypeStruct((b, t, RET_V_WIDTH), BF16),
                   jax.ShapeDtypeStruct((b, N_RET_HEADS, RET_KEY_DIM, RET_VAL_DIM), F32)],
        scratch_shapes=[pltpu.VMEM((heads, blk, blk), F32), pltpu.VMEM((heads, blk, RET_KEY_DIM), F32),
                        pltpu.VMEM((heads, blk, RET_KEY_DIM), F32)],
        compiler_params=_cparams(3), name="retention",
    )(lg, q, k, *vg, gn, s0)


def _mixer_kernel(x_ref, oa_ref, ob_ref, g_ref, wg_ref, wa_ref, wb_ref, wo_ref, x1_ref, *, oa_feature_major):
    x = x_ref[...]
    h = _rmsnorm(x, g_ref[...]).astype(BF16)
    gate_a = jax.nn.sigmoid(_dot(h, wg_ref[:, 0:D_MODEL]))
    gate_b = jax.nn.sigmoid(_dot(h, wg_ref[:, D_MODEL:2 * D_MODEL]))
    if oa_feature_major:
        ya = lax.dot_general(oa_ref[0], wa_ref[...], (((0,), (0,)), ((), ())), preferred_element_type=F32)
    else:
        ya = _dot(oa_ref[...], wa_ref[...])
    merged = gate_a * ya + gate_b * _dot(ob_ref[...], wb_ref[...])
    x1_ref[...] = x + _dot(merged.astype(BF16), wo_ref[...])


def _mixer(x, oa, ob, g, wg, wa, wb, wo):
    n = x.shape[0]
    tm = MIX_TOKEN_TILE
    tok = lambda i: (i, 0)
    fixed = lambda i: (0, 0)
    feature_major = oa.ndim == 3
    if feature_major:
        per_batch = oa.shape[2] // tm
        oa_spec = pl.BlockSpec((1, FOX_WIDTH, tm), lambda i: (i // per_batch, 0, i % per_batch))
    else:
        oa_spec = pl.BlockSpec((tm, FOX_WIDTH), tok)
    return pl.pallas_call(
        functools.partial(_mixer_kernel, oa_feature_major=feature_major),
        grid=(n // tm,),
        in_specs=[pl.BlockSpec((tm, D_MODEL), tok), oa_spec,
                  pl.BlockSpec((tm, RET_V_WIDTH), tok), pl.BlockSpec((1, D_MODEL), fixed),
                  _resident((D_MODEL, 2 * D_MODEL), fixed), _resident((FOX_WIDTH, D_MODEL), fixed),
                  _resident((RET_V_WIDTH, D_MODEL), fixed), _resident((D_MODEL, D_MODEL), fixed)],
        out_specs=pl.BlockSpec((tm, D_MODEL), tok),
        out_shape=jax.ShapeDtypeStruct((n, D_MODEL), F32),
        compiler_params=_cparams(1), name="mixer",
    )(x, oa, ob, g, wg, wa, wb, wo)


def _mlp_kernel(x_ref, g_ref, wu_ref, wd_ref, gf_ref, y_ref):
    x = x_ref[...]
    h = _rmsnorm(x, g_ref[...]).astype(BF16)
    acc = x
    for c in range(D_FF // D_MODEL):
        cols = slice(c * D_MODEL, (c + 1) * D_MODEL)
        u = jnp.square(jnp.maximum(_dot(h, wu_ref[:, cols]), 0.0)).astype(BF16)
        acc = acc + _dot(u, wd_ref[cols, :])
    y_ref[...] = _rmsnorm(acc, gf_ref[...])


def _mlp(x, g, wu, wd, gf):
    n = x.shape[0]
    tm = MIX_TOKEN_TILE
    tok = lambda i: (i, 0)
    fixed = lambda i: (0, 0)
    return pl.pallas_call(
        _mlp_kernel,
        grid=(n // tm,),
        in_specs=[pl.BlockSpec((tm, D_MODEL), tok), pl.BlockSpec((1, D_MODEL), fixed),
                  _resident((D_MODEL, D_FF), fixed), _resident((D_FF, D_MODEL), fixed),
                  pl.BlockSpec((1, D_MODEL), fixed)],
        out_specs=pl.BlockSpec((tm, D_MODEL), tok),
        out_shape=jax.ShapeDtypeStruct((n, D_MODEL), F32),
        compiler_params=_cparams(1), name="mlp",
    )(x, g, wu, wd, gf)


def _rope_tables(pos):
    inv_freq = ROPE_BASE ** (-jnp.arange(0, RET_KEY_DIM, 2, dtype=F32) / RET_KEY_DIM)
    ang = pos.astype(F32)[:, None] * inv_freq[None, :]
    return jnp.cos(ang), jnp.sin(ang)


def _after_attention(x, pos, o_a, params, ret_blk, state0, fuse_vg):
    b, t, _ = x.shape
    n = b * t
    xf = x.reshape(n, D_MODEL)
    cos, sin = _rope_tables(pos)
    reps = max(1, TOKEN_TILE // t)
    q_r, k_r = _proj_ret_qk(xf, params["g_attn"], params["w_ret_qk"], jnp.tile(cos, (reps, 1)),
                            jnp.tile(sin, (reps, 1)))
    if fuse_vg:
        vg = (x, params["g_attn"], params["w_ret_vg"])
    else:
        v_r, g_r = _proj_ret_vg(xf, params["g_attn"], params["w_ret_vg"])
        vg = (v_r.reshape(b, t, RET_V_WIDTH), g_r.reshape(b, t, RET_V_WIDTH))
    o_b, state = _retention(params["lg"], q_r.reshape(b, t, RET_QK_WIDTH), k_r.reshape(b, t, RET_QK_WIDTH),
                            vg, params["g_ret_norm"], state0, ret_blk)
    x1 = _mixer(xf, o_a, o_b.reshape(n, RET_V_WIDTH), params["g_attn"],
                params["w_gates"], params["w_branch_a"], params["w_branch_b"], params["w_out"])
    y = _mlp(x1, params["g_mlp"], params["w_up"], params["w_down"], params["g_final"])
    return y.reshape(b, t, D_MODEL), state[None]


def kernel(x_prompt, x_sample, cache_fox_k, cache_fox_v, cache_fox_logf, state_ret, g_attn, w_in, b_forget,
           g_ret_norm, w_branch, w_out, g_mlp, w_up, w_down, g_final):
    assert w_in.shape[0] == 1, "single-layer trunk"
    wi = w_in[0]
    fox_cols = 3 * FOX_WIDTH + N_FOX_HEADS
    qk_end = fox_cols + 2 * RET_QK_WIDTH
    vg_end = qk_end + 2 * RET_V_WIDTH
    lane_pad = LANES - N_FOX_HEADS
    lg = jnp.log(1.0 - 2.0 ** (-5.0 - jnp.arange(N_RET_HEADS, dtype=F32)))
    w_fox = jnp.pad(wi[:, :fox_cols], ((0, 0), (0, lane_pad))).astype(BF16)
    b_row = jnp.pad(b_forget[0], (0, lane_pad)).reshape(1, LANES)
    params = {
        "g_attn": g_attn[0].reshape(1, D_MODEL),
        "w_ret_qk": wi[:, fox_cols:qk_end].astype(BF16),
        "w_ret_vg": wi[:, qk_end:vg_end].astype(BF16),
        "w_gates": wi[:, vg_end:].astype(BF16),
        "lg": jnp.broadcast_to(lg[:, None, None], (N_RET_HEADS, SUBLANES, LANES)),
        "g_ret_norm": g_ret_norm[0].reshape(N_RET_HEADS, 1, RET_VAL_DIM),
        "w_branch_a": w_branch[0, :FOX_WIDTH].astype(BF16),
        "w_branch_b": w_branch[0, FOX_WIDTH:].astype(BF16),
        "w_out": w_out[0].astype(BF16),
        "g_mlp": g_mlp[0].reshape(1, D_MODEL),
        "w_up": w_up[0].astype(BF16),
        "w_down": w_down[0].astype(BF16),
        "g_final": g_final.reshape(1, D_MODEL),
    }
    bp, tp, _ = x_prompt.shape
    bs, ts, _ = x_sample.shape
    past = cache_fox_k.shape[2]

    qt, kt, vt, lft, lf_pad, q_norm, k_norm, kb = _proj_fox_t(
        x_prompt, params["g_attn"], wi[:, :fox_cols].T.astype(BF16), w_fox[:, 3 * FOX_WIDTH:],
        jnp.broadcast_to(b_forget[0][:, None], (N_FOX_HEADS, LANES)), b_row)
    c_pieces, c_end = _cumsum_pieces(lf_pad, CUMSUM_TILE, LOG2E)
    first = _first_key_block(c_end, q_norm, k_norm, tp, FOX_Q_BLOCK, FOX_BLOCK)
    o_a = _fox_prompt(first, qt, kb, vt, c_pieces)
    zero_state = jnp.zeros((bp, N_RET_HEADS, RET_KEY_DIM, RET_VAL_DIM), F32)
    yp, sp = _after_attention(x_prompt, jnp.arange(tp), o_a, params, RET_BLOCK, zero_state, True)
    to_heads = lambda a: jnp.transpose(a.reshape(bp, N_FOX_HEADS, FOX_HEAD_DIM, tp), (0, 3, 1, 2))[None]
    kp, vp = to_heads(kt), to_heads(vt)
    fp = jnp.transpose(lft, (0, 2, 1))[None]

    q, k, v, lf = _proj_fox(x_sample.reshape(bs * ts, D_MODEL), params["g_attn"], w_fox, b_row)
    feature_major = lambda a: jnp.transpose(a[0], (0, 2, 3, 1)).reshape(bs, FOX_WIDTH, past)
    lf_time = jnp.concatenate([jnp.transpose(cache_fox_logf[0], (0, 2, 1)),
                               jnp.transpose(lf.reshape(bs, ts, N_FOX_HEADS), (0, 2, 1))], axis=2)
    lf_time = jnp.pad(lf_time, ((0, 0), (0, 0), (0, -(past + ts) % LANES)))
    o_a = _fox_sample(q.reshape(bs, ts, FOX_WIDTH), k.reshape(bs, ts, FOX_WIDTH), v.reshape(bs, ts, FOX_WIDTH),
                      feature_major(cache_fox_k), feature_major(cache_fox_v), _cumsum_lanes(lf_time, LOG2E))
    ys, ss = _after_attention(x_sample, past + jnp.arange(ts), o_a.reshape(bs * ts, FOX_WIDTH), params, CHUNK,
                              state_ret[0], False)
    shape5 = (1, bs, ts, N_FOX_HEADS, FOX_HEAD_DIM)
    return (yp, ys, kp, vp, fp, sp, k.reshape(shape5), v.reshape(shape5),
            lf.reshape(1, bs, ts, N_FOX_HEADS), ss)
```

```python
import functools

import jax
import jax.numpy as jnp
from jax import lax
from jax.experimental import pallas as pl
from jax.experimental.pallas import tpu as pltpu

D_MODEL = 1024
N_FOX_HEADS = 16
FOX_HEAD_DIM = 64
FOX_WIDTH = N_FOX_HEADS * FOX_HEAD_DIM
N_RET_HEADS = 4
RET_KEY_DIM = 256
RET_VAL_DIM = 512
RET_QK_WIDTH = N_RET_HEADS * RET_KEY_DIM
RET_V_WIDTH = N_RET_HEADS * RET_VAL_DIM
D_FF = 4 * D_MODEL
CHUNK = 64
ROPE_BASE = 10000.0
EPS = 1e-6

LANES = 128
SUBLANES = 8
HEADS_PER_LANE_BLOCK = LANES // FOX_HEAD_DIM
N_HEAD_PAIRS = N_FOX_HEADS // HEADS_PER_LANE_BLOCK
N_C_PIECES = 3
ONES_ROWS = 16
MASK_VALUE = -1e30
LOG2E = 1.4426950408889634
VMEM_LIMIT = 56 * 1024 * 1024

TOKEN_TILE = 512
MIX_TOKEN_TILE = 512
FOX_BLOCK = 256
FOX_Q_BLOCK = 512
SAMPLE_PAIRS_PER_STEP = 8
RET_BLOCK = 256
RET_HEADS_PER_STEP = 4
CUMSUM_TILE = 512
SKIP_EXPONENT = 160.0
NORM_SLACK = 1.01

BF16 = jnp.bfloat16
F32 = jnp.float32


def _cparams(n_axes):
    return pltpu.CompilerParams(dimension_semantics=("arbitrary",) * n_axes,
                                vmem_limit_bytes=VMEM_LIMIT)


def _resident(shape, index_map):
    return pl.BlockSpec(shape, index_map, pipeline_mode=pl.Buffered(1))


def _dot(a, b):
    return jnp.dot(a, b, preferred_element_type=F32)


def _dot_nt(a, b):
    return lax.dot_general(a, b, (((1,), (1,)), ((), ())), preferred_element_type=F32)


def _rmsnorm(x, g):
    return x * lax.rsqrt(jnp.mean(x * x, axis=-1, keepdims=True) + EPS) * g


def _log_sigmoid(z):
    return -(jnp.maximum(-z, 0.0) + jnp.log(1.0 + jnp.exp(-jnp.abs(z))))


def _split3(x):
    a = x.astype(BF16)
    r = x - a.astype(F32)
    b = r.astype(BF16)
    c = (r - b.astype(F32)).astype(BF16)
    return a, b, c


def _indicator(idx, lo, hi):
    return jnp.where(idx >= lo, jnp.where(idx < hi, 1.0, 0.0), 0.0)


def _proj_fox_t_kernel(x_ref, g_ref, wt_ref, wf_ref, bcol_ref, brow_ref, qt_ref, kt_ref, vt_ref, lft_ref, lfp_ref,
                       qn_ref, kn_ref, kb_ref):
    h = _rmsnorm(x_ref[0], g_ref[...])
    ht = h.T.astype(BF16)
    w = FOX_WIDTH
    qs = _dot(wt_ref[0:w, :], ht) * (FOX_HEAD_DIM ** -0.5 * LOG2E)
    kf = _dot(wt_ref[w:2 * w, :], ht)
    qt_ref[0] = qs.astype(BF16)
    kt_ref[0] = kf
    kb_ref[0] = kf.T.astype(BF16)
    for src, dst in ((qs, qn_ref), (kf, kn_ref)):
        sq = jnp.sum((src * src).reshape(N_FOX_HEADS, FOX_HEAD_DIM, src.shape[1]), axis=1)
        dst[0, 0] = jnp.broadcast_to(jnp.sqrt(jnp.max(sq, axis=1, keepdims=True)), (N_FOX_HEADS, LANES))
    vt_ref[0] = _dot(wt_ref[2 * w:3 * w, :], ht)
    lft_ref[0] = _log_sigmoid(_dot(wt_ref[3 * w:3 * w + N_FOX_HEADS, :], ht) + bcol_ref[:, 0:1])
    lfp_ref[0] = _log_sigmoid(_dot(h.astype(BF16), wf_ref[...]) + brow_ref[...])


def _proj_fox_t(x, g, wt, wf, bcol, brow):
    b, t, _ = x.shape
    tm = TOKEN_TILE
    fixed = lambda i, j: (0, 0)
    feat = lambda i, j: (i, 0, j)
    return pl.pallas_call(
        _proj_fox_t_kernel,
        grid=(b, t // tm),
        in_specs=[pl.BlockSpec((1, tm, D_MODEL), lambda i, j: (i, j, 0)), pl.BlockSpec((1, D_MODEL), fixed),
                  pl.BlockSpec(wt.shape, fixed), pl.BlockSpec(wf.shape, fixed),
                  pl.BlockSpec(bcol.shape, fixed), pl.BlockSpec(brow.shape, fixed)],
        out_specs=[pl.BlockSpec((1, FOX_WIDTH, tm), feat), pl.BlockSpec((1, FOX_WIDTH, tm), feat),
                   pl.BlockSpec((1, FOX_WIDTH, tm), feat), pl.BlockSpec((1, N_FOX_HEADS, tm), feat),
                   pl.BlockSpec((1, tm, LANES), lambda i, j: (i, j, 0)),
                   pl.BlockSpec((1, 1, N_FOX_HEADS, LANES), lambda i, j: (i, j, 0, 0)),
                   pl.BlockSpec((1, 1, N_FOX_HEADS, LANES), lambda i, j: (i, j, 0, 0)),
                   pl.BlockSpec((1, tm, FOX_WIDTH), lambda i, j: (i, j, 0))],
        out_shape=[jax.ShapeDtypeStruct((b, FOX_WIDTH, t), BF16), jax.ShapeDtypeStruct((b, FOX_WIDTH, t), F32),
                   jax.ShapeDtypeStruct((b, FOX_WIDTH, t), F32), jax.ShapeDtypeStruct((b, N_FOX_HEADS, t), F32),
                   jax.ShapeDtypeStruct((b, t, LANES), F32),
                   jax.ShapeDtypeStruct((b, t // tm, N_FOX_HEADS, LANES), F32),
                   jax.ShapeDtypeStruct((b, t // tm, N_FOX_HEADS, LANES), F32),
                   jax.ShapeDtypeStruct((b, t, FOX_WIDTH), BF16)],
        compiler_params=_cparams(2), name="proj_fox_t",
    )(x, g, wt, wf, bcol, brow)


def _proj_fox_kernel(x_ref, g_ref, w_ref, b_ref, q_ref, k_ref, v_ref, lf_ref):
    h = _rmsnorm(x_ref[...], g_ref[...]).astype(BF16)
    w = FOX_WIDTH
    q_ref[...] = (_dot(h, w_ref[:, 0:w]) * (FOX_HEAD_DIM ** -0.5 * LOG2E)).astype(BF16)
    k_ref[...] = _dot(h, w_ref[:, w:2 * w])
    v_ref[...] = _dot(h, w_ref[:, 2 * w:3 * w])
    lf = _log_sigmoid(_dot(h, w_ref[:, 3 * w:3 * w + LANES]) + b_ref[...])
    lf_ref[...] = lf[:, 0:N_FOX_HEADS]


def _proj_fox(x, g, w, b):
    n = x.shape[0]
    tm = TOKEN_TILE
    tok = lambda i: (i, 0)
    fixed = lambda i: (0, 0)
    return pl.pallas_call(
        _proj_fox_kernel,
        grid=(n // tm,),
        in_specs=[pl.BlockSpec((tm, D_MODEL), tok), pl.BlockSpec((1, D_MODEL), fixed),
                  pl.BlockSpec(w.shape, fixed), pl.BlockSpec((1, LANES), fixed)],
        out_specs=[pl.BlockSpec((tm, FOX_WIDTH), tok), pl.BlockSpec((tm, FOX_WIDTH), tok),
                   pl.BlockSpec((tm, FOX_WIDTH), tok), pl.BlockSpec((tm, N_FOX_HEADS), tok)],
        out_shape=[jax.ShapeDtypeStruct((n, FOX_WIDTH), BF16), jax.ShapeDtypeStruct((n, FOX_WIDTH), F32),
                   jax.ShapeDtypeStruct((n, FOX_WIDTH), F32), jax.ShapeDtypeStruct((n, N_FOX_HEADS), F32)],
        compiler_params=_cparams(1), name="proj_fox",
    )(x, g, w, b)


def _proj_ret_qk_kernel(x_ref, g_ref, w_ref, cos_ref, sin_ref, q_ref, k_ref):
    h = _rmsnorm(x_ref[...], g_ref[...]).astype(BF16)
    cos = cos_ref[...]
    sin = sin_ref[...]
    half = RET_KEY_DIM // 2
    for out_ref, base, scale in ((q_ref, 0, 1.0), (k_ref, RET_QK_WIDTH, RET_KEY_DIM ** -0.5)):
        for hd in range(N_RET_HEADS):
            lo = hd * RET_KEY_DIM
            z = _dot(h, w_ref[:, base + lo:base + lo + RET_KEY_DIM])
            x1 = z[:, :half]
            x2 = z[:, half:]
            out_ref[:, lo:lo + half] = ((x1 * cos - x2 * sin) * scale).astype(BF16)
            out_ref[:, lo + half:lo + RET_KEY_DIM] = ((x1 * sin + x2 * cos) * scale).astype(BF16)


def _proj_ret_qk(x, g, w, cos, sin):
    n = x.shape[0]
    tm = TOKEN_TILE
    period = cos.shape[0] // tm
    tok = lambda i: (i, 0)
    fixed = lambda i: (0, 0)
    pos = lambda i: (i % period, 0)
    half = RET_KEY_DIM // 2
    return pl.pallas_call(
        _proj_ret_qk_kernel,
        grid=(n // tm,),
        in_specs=[pl.BlockSpec((tm, D_MODEL), tok), pl.BlockSpec((1, D_MODEL), fixed),
                  pl.BlockSpec((D_MODEL, 2 * RET_QK_WIDTH), fixed),
                  pl.BlockSpec((tm, half), pos), pl.BlockSpec((tm, half), pos)],
        out_specs=[pl.BlockSpec((tm, RET_QK_WIDTH), tok), pl.BlockSpec((tm, RET_QK_WIDTH), tok)],
        out_shape=[jax.ShapeDtypeStruct((n, RET_QK_WIDTH), BF16)] * 2,
        compiler_params=_cparams(1), name="proj_ret_qk",
    )(x, g, w, cos, sin)


def _proj_ret_vg_kernel(x_ref, g_ref, w_ref, v_ref, gr_ref):
    h = _rmsnorm(x_ref[...], g_ref[...]).astype(BF16)
    v_ref[...] = _dot(h, w_ref[:, 0:RET_V_WIDTH]).astype(BF16)
    gr_ref[...] = _dot(h, w_ref[:, RET_V_WIDTH:2 * RET_V_WIDTH])


def _proj_ret_vg(x, g, w):
    n = x.shape[0]
    tm = TOKEN_TILE
    tok = lambda i: (i, 0)
    fixed = lambda i: (0, 0)
    return pl.pallas_call(
        _proj_ret_vg_kernel,
        grid=(n // tm,),
        in_specs=[pl.BlockSpec((tm, D_MODEL), tok), pl.BlockSpec((1, D_MODEL), fixed),
                  pl.BlockSpec((D_MODEL, 2 * RET_V_WIDTH), fixed)],
        out_specs=[pl.BlockSpec((tm, RET_V_WIDTH), tok), pl.BlockSpec((tm, RET_V_WIDTH), tok)],
        out_shape=[jax.ShapeDtypeStruct((n, RET_V_WIDTH), BF16), jax.ShapeDtypeStruct((n, RET_V_WIDTH), F32)],
        compiler_params=_cparams(1), name="proj_ret_vg",
    )(x, g, w)


def _cumsum_kernel(lf_ref, c_ref, cend_ref, carry_ref, *, tb, scale):
    @pl.when(pl.program_id(1) == 0)
    def _():
        carry_ref[...] = jnp.zeros_like(carry_ref)

    row = lax.broadcasted_iota(jnp.int32, (tb, tb), 0)
    col = lax.broadcasted_iota(jnp.int32, (tb, tb), 1)
    tri = jnp.where(col <= row, 1.0, 0.0).astype(BF16)
    c = carry_ref[0:1, :]
    for piece in _split3(lf_ref[0]):
        c = c + _dot(tri, piece)
    carry_ref[...] = jnp.broadcast_to(c[tb - 1:tb, :], carry_ref.shape)
    erow = lax.broadcasted_iota(jnp.int32, (SUBLANES, LANES), 0)
    ends = jnp.zeros((SUBLANES, LANES), F32)
    for e in range(tb // FOX_BLOCK):
        ends = jnp.where(erow == e, c[(e + 1) * FOX_BLOCK - 1:(e + 1) * FOX_BLOCK, :] * scale, ends)
    cend_ref[0, 0] = ends

    prow = lax.broadcasted_iota(jnp.int32, (LANES, LANES), 0)
    pcol = lax.broadcasted_iota(jnp.int32, (LANES, LANES), 1)
    out = jnp.zeros((tb, LANES), F32)
    for p, piece in enumerate(_split3(-c * scale)):
        place = jnp.where(pcol == prow + p * N_FOX_HEADS, jnp.where(prow < N_FOX_HEADS, 1.0, 0.0), 0.0)
        out = out + _dot(piece, place.astype(BF16))
    c_ref[0] = out.astype(BF16)


def _cumsum_pieces(lf, tb, scale):
    b, t, _ = lf.shape
    blk = lambda i, j: (i, j, 0)
    return pl.pallas_call(
        functools.partial(_cumsum_kernel, tb=tb, scale=scale),
        grid=(b, t // tb),
        in_specs=[pl.BlockSpec((1, tb, LANES), blk)],
        out_specs=[pl.BlockSpec((1, tb, LANES), blk), pl.BlockSpec((1, 1, SUBLANES, LANES), lambda i, j: (i, j, 0, 0))],
        out_shape=[jax.ShapeDtypeStruct((b, t, LANES), BF16), jax.ShapeDtypeStruct((b, t // tb, SUBLANES, LANES), F32)],
        scratch_shapes=[pltpu.VMEM((SUBLANES, LANES), F32)],
        compiler_params=_cparams(2), name="cumsum",
    )(lf)


def _cumsum_lanes_kernel(lf_ref, c_ref, *, scale):
    heads, t = lf_ref.shape[1], lf_ref.shape[2]
    row = lax.broadcasted_iota(jnp.int32, (LANES, LANES), 0)
    col = lax.broadcasted_iota(jnp.int32, (LANES, LANES), 1)
    tri = jnp.where(row <= col, 1.0, 0.0).astype(BF16)
    ones = jnp.ones((LANES, LANES), BF16)
    carry = jnp.zeros((heads, LANES), F32)
    for seg in range(t // LANES):
        lanes = slice(seg * LANES, (seg + 1) * LANES)
        stack = jnp.concatenate(_split3(lf_ref[0, :, lanes]), axis=0)
        within = _dot(stack, tri)
        total = _dot(stack, ones)
        fold = lambda a: a[0:heads] + a[heads:2 * heads] + a[2 * heads:3 * heads]
        c_ref[0, :, lanes] = (fold(within) + carry) * scale
        carry = carry + fold(total)


def _cumsum_lanes(lf, scale):
    b, heads, t = lf.shape
    blk = lambda i: (i, 0, 0)
    return pl.pallas_call(
        functools.partial(_cumsum_lanes_kernel, scale=scale),
        grid=(b,),
        in_specs=[pl.BlockSpec((1, heads, t), blk)],
        out_specs=pl.BlockSpec((1, heads, t), blk),
        out_shape=jax.ShapeDtypeStruct((b, heads, t), F32),
        compiler_params=_cparams(1), name="cumsum_lanes",
    )(lf)


def _piece_selector(shape, axis, head):
    idx = lax.broadcasted_iota(jnp.int32, shape, axis)
    hit = jnp.where(idx < N_C_PIECES * N_FOX_HEADS, jnp.where((idx & (N_FOX_HEADS - 1)) == head, 1.0, 0.0), 0.0)
    return hit


_ROW_M, _ROW_MX, _ROW_ALPHA = 0, 1, 2


def _fox_prompt_kernel(first_ref, qt_ref, kb_ref, vt_ref, c_ref, o_ref, qa_ref, s_ref, p_ref, acc_ref,
                       stat_ref, *, t, tq, blk):
    pair = pl.program_id(1)
    qi = pl.program_id(2)
    diag_blocks = tq // blk
    d = FOX_HEAD_DIM
    heads = range(HEADS_PER_LANE_BLOCK)

    qt = qt_ref[0]
    row = lax.broadcasted_iota(jnp.int32, (LANES, tq), 0)
    for h in heads:
        in_head = _indicator(row, h * d, (h + 1) * d).astype(BF16)
        sel = _piece_selector((LANES, tq), 0, pair * HEADS_PER_LANE_BLOCK + h).astype(BF16)
        qa_ref[h] = jnp.concatenate([qt * in_head, sel], axis=0)

    acc_ref[...] = jnp.zeros_like(acc_ref)
    stat_ref[...] = jnp.full(stat_ref.shape, MASK_VALUE, F32)
    krow = lax.broadcasted_iota(jnp.int32, (blk, blk), 0)
    qcol = lax.broadcasted_iota(jnp.int32, (blk, blk), 1)

    def stat(h, slot, kind, part):
        r = (((h * 2 + slot) * 4 + kind) * diag_blocks) + part
        return slice(r, r + 1)

    def live_parts(diag):
        return range(diag or 0, diag_blocks)

    def stage_a(j, slot, diag):
        off = pl.multiple_of(j * blk, blk)
        kb = jnp.concatenate([kb_ref[0, pl.ds(off, blk), :], c_ref[0, pl.ds(off, blk), :]], axis=1)
        for h in heads:
            for part in live_parts(diag):
                cols = slice(part * blk, (part + 1) * blk)
                s = _dot(kb, qa_ref[h, :, cols])
                if diag == part:
                    s = jnp.where(krow <= qcol, s, MASK_VALUE)
                s_ref[slot, h, :, cols] = s
                stat_ref[stat(h, slot, _ROW_MX, part), :] = jnp.max(s, axis=0, keepdims=True)

    def stage_b(slot, diag=None):
        for h in heads:
            for part in live_parts(diag):
                cols = slice(part * blk, (part + 1) * blk)
                m_old = stat_ref[stat(h, 0, _ROW_M, part), :]
                m_new = jnp.maximum(m_old, stat_ref[stat(h, slot, _ROW_MX, part), :])
                stat_ref[stat(h, slot, _ROW_ALPHA, part), :] = jnp.exp2(m_old - m_new)
                stat_ref[stat(h, 0, _ROW_M, part), :] = m_new
                p_ref[slot, h, :, cols] = jnp.exp2(s_ref[slot, h, :, cols] - m_new).astype(BF16)

    ones = jnp.ones((ONES_ROWS, blk), BF16)

    def stage_c(j, slot, diag=None):
        off = pl.multiple_of(j * blk, blk)
        for h in heads:
            va = jnp.concatenate([vt_ref[0, h * d:(h + 1) * d, pl.ds(off, blk)].astype(BF16), ones], axis=0)
            for part in live_parts(diag):
                cols = slice(part * blk, (part + 1) * blk)
                alpha = stat_ref[stat(h, slot, _ROW_ALPHA, part), :]
                acc_ref[h, :, cols] = acc_ref[h, :, cols] * alpha + _dot(va, p_ref[slot, h, :, cols])

    nfull = qi * diag_blocks
    j0 = first_ref[(pl.program_id(0) * N_HEAD_PAIRS + pair) * (t // tq) + qi]

    @pl.when(qi > 0)
    def _pipelined():
        stage_a(j0, 0, None)
        stage_b(0)
        stage_a(j0 + 1, 1, None)

        def two_steps(j):
            stage_c(j - 2, 0)
            stage_b(1)
            stage_a(j, 0, None)
            stage_c(j - 1, 1)
            stage_b(0)
            stage_a(j + 1, 1, None)

        def body(u, carry):
            two_steps(j0 + 2 + 4 * u)
            two_steps(j0 + 4 + 4 * u)
            return carry

        pairs_left = (nfull - j0) // 2 - 1
        lax.fori_loop(0, pairs_left // 2, body, 0)

        @pl.when(pairs_left % 2 == 1)
        def _odd_group():
            two_steps(j0 + 2 * pairs_left)
        stage_c(nfull - 2, 0)
        stage_b(1)
        stage_a(nfull, 0, 0)
        stage_c(nfull - 1, 1)
        stage_b(0, 0)
        stage_a(nfull + 1, 1, 1)
        stage_c(nfull, 0, 0)
        stage_b(1, 1)
        stage_c(nfull + 1, 1, 1)

    @pl.when(qi == 0)
    def _first_block():
        for dg in range(diag_blocks):
            stage_a(dg, dg % 2, dg)
            stage_b(dg % 2, dg)
            stage_c(dg, dg % 2, dg)

    for h in heads:
        a = acc_ref[h]
        o_ref[0, h * d:(h + 1) * d, :] = (a[0:d] / a[d:d + 1]).astype(BF16)


def _first_key_block(c_end, q_norm, k_norm, t, tq, blk):
    b = c_end.shape[0]
    nblk, nq, per = t // blk, t // tq, tq // blk
    ce = c_end[:, :, :CUMSUM_TILE // blk, :N_FOX_HEADS].reshape(b, nblk, N_FOX_HEADS)
    qn = q_norm[:, :, :, 0]
    kmax = jnp.max(k_norm[:, :, :, 0], axis=1)
    before = jnp.concatenate([jnp.zeros((b, 1, N_FOX_HEADS), F32), ce[:, per - 1:-1:per]], axis=1)
    decay = before[:, :, None, :] - ce[:, None, :, :]
    bound = (2.0 * NORM_SLACK * NORM_SLACK) * (qn * kmax[:, None, :])[:, :, None, :] + decay
    below = (jnp.arange(nblk)[None, :] < (jnp.arange(nq) * per)[:, None])[None, :, :, None]
    skippable = jnp.logical_and(bound <= -SKIP_EXPONENT, below)
    block = jnp.arange(nblk, dtype=jnp.int32)[None, None, :, None]
    n_skip = jnp.min(jnp.where(skippable, nblk, block), axis=2)
    n_skip = jnp.min(n_skip.reshape(b, nq, N_HEAD_PAIRS, HEADS_PER_LANE_BLOCK), axis=3)
    first = jnp.clip((n_skip // 2) * 2, 0, jnp.maximum(jnp.arange(nq) * per - 2, 0)[None, :, None])
    return jnp.transpose(first, (0, 2, 1)).reshape(-1).astype(jnp.int32)


def _fox_prompt(first, qt, kb, vt, c):
    b, _, t = qt.shape
    blk = FOX_BLOCK
    tq = FOX_Q_BLOCK
    assert tq == 2 * blk, "the pipeline keeps two key blocks in flight and slot = block parity"
    hp = HEADS_PER_LANE_BLOCK
    acc_rows = FOX_HEAD_DIM + ONES_ROWS
    per_pair_all_time = lambda i, p, j, first: (i, p, 0)
    grid_spec = pltpu.PrefetchScalarGridSpec(
        num_scalar_prefetch=1,
        grid=(b, N_HEAD_PAIRS, t // tq),
        in_specs=[pl.BlockSpec((1, LANES, tq), lambda i, p, j, first: (i, p, j)),
                  pl.BlockSpec((1, t, LANES), lambda i, p, j, first: (i, 0, p)),
                  pl.BlockSpec((1, LANES, t), per_pair_all_time),
                  pl.BlockSpec((1, t, LANES), lambda i, p, j, first: (i, 0, 0))],
        out_specs=pl.BlockSpec((1, LANES, tq), lambda i, p, j, first: (i, p, j)),
        scratch_shapes=[pltpu.VMEM((hp, 2 * LANES, tq), BF16),
                        pltpu.VMEM((2, hp, blk, tq), F32),
                        pltpu.VMEM((2, hp, blk, tq), BF16),
                        pltpu.VMEM((hp, acc_rows, tq), F32),
                        pltpu.VMEM((hp * 2 * 4 * (tq // blk), blk), F32)])
    return pl.pallas_call(
        functools.partial(_fox_prompt_kernel, t=t, tq=tq, blk=blk),
        grid_spec=grid_spec,
        out_shape=jax.ShapeDtypeStruct((b, FOX_WIDTH, t), BF16),
        compiler_params=_cparams(3), name="fox_prompt",
    )(first, qt, kb, vt, c)


def _fox_sample_kernel(q_ref, kn_ref, vn_ref, kc_ref, vc_ref, c_ref, o_ref, *, past, new, pairs):
    group = pl.program_id(1)
    d = FOX_HEAD_DIM
    rows = HEADS_PER_LANE_BLOCK * new
    lane = lax.broadcasted_iota(jnp.int32, (rows, LANES), 1)
    qrow = lax.broadcasted_iota(jnp.int32, (rows, LANES), 0)
    head_lanes = jnp.where(qrow < new, _indicator(lane, 0, d), _indicator(lane, d, 2 * d)).astype(BF16)
    out_lane = lax.broadcasted_iota(jnp.int32, (new, LANES), 1)
    nrow = lax.broadcasted_iota(jnp.int32, (rows, new), 0)
    ncol = lax.broadcasted_iota(jnp.int32, (rows, new), 1)
    causal = ncol <= (nrow & (new - 1))
    for pp in range(pairs):
        lanes = slice(pp * LANES, (pp + 1) * LANES)
        head0 = (group * pairs + pp) * HEADS_PER_LANE_BLOCK
        tpad = c_ref.shape[2]
        bias = jnp.concatenate(
            [jnp.broadcast_to(c_ref[0, pl.ds(head0 + h, 1), :], (new, tpad)) for h in range(HEADS_PER_LANE_BLOCK)],
            axis=0)
        q = q_ref[0, :, lanes]
        q2 = jnp.concatenate([q, q], axis=0) * head_lanes
        s_c = _dot(q2, kc_ref[0, lanes, :].astype(BF16)) - bias[:, 0:past]
        s_n = _dot_nt(q2, kn_ref[0, :, lanes].astype(BF16)) - bias[:, past:past + new]
        s_n = jnp.where(causal, s_n, MASK_VALUE)
        m = jnp.maximum(jnp.max(s_c, axis=-1, keepdims=True), jnp.max(s_n, axis=-1, keepdims=True))
        e_c = jnp.exp2(s_c - m)
        e_n = jnp.exp2(s_n - m)
        l = jnp.sum(e_c, axis=-1, keepdims=True) + jnp.sum(e_n, axis=-1, keepdims=True)
        o2 = _dot_nt(e_c.astype(BF16), vc_ref[0, lanes, :].astype(BF16))
        o2 = (o2 + _dot(e_n.astype(BF16), vn_ref[0, :, lanes].astype(BF16))) / l
        o = jnp.where(out_lane < d, o2[0:new], o2[new:rows])
        o_ref[0, :, lanes] = o.astype(BF16)


def _fox_sample(q, kn, vn, kct, vct, c):
    b, new, _ = q.shape
    past = kct.shape[2]
    pairs = SAMPLE_PAIRS_PER_STEP
    width = pairs * LANES
    tok = lambda i, g: (i, 0, g)
    feat = lambda i, g: (i, g, 0)
    return pl.pallas_call(
        functools.partial(_fox_sample_kernel, past=past, new=new, pairs=pairs),
        grid=(b, N_HEAD_PAIRS // pairs),
        in_specs=[pl.BlockSpec((1, new, width), tok), pl.BlockSpec((1, new, width), tok),
                  pl.BlockSpec((1, new, width), tok), pl.BlockSpec((1, width, past), feat),
                  pl.BlockSpec((1, width, past), feat),
                  pl.BlockSpec((1,) + c.shape[1:], lambda i, g: (i, 0, 0))],
        out_specs=pl.BlockSpec((1, new, width), tok),
        out_shape=jax.ShapeDtypeStruct((b, new, FOX_WIDTH), BF16),
        compiler_params=_cparams(2), name="fox_sample",
    )(q, kn, vn, kct, vct, c)


def _retention_kernel(lg_ref, q_ref, k_ref, *rest, blk, heads, fused_vg):
    if fused_vg:
        x_ref, ga_ref, w_ref, gn_ref, s0_ref, o_ref, st_ref, decay_ref, qdec_ref, kdec_ref = rest
        hn = _rmsnorm(x_ref[0], ga_ref[...]).astype(BF16)
    else:
        v_ref, g_ref, gn_ref, s0_ref, o_ref, st_ref, decay_ref, qdec_ref, kdec_ref = rest
    @pl.when(pl.program_id(2) == 0)
    def _():
        n = lax.broadcasted_iota(jnp.int32, (blk, blk), 0)
        m = lax.broadcasted_iota(jnp.int32, (blk, blk), 1)
        shift = CHUNK.bit_length() - 1
        dist = jnp.abs(n - m).astype(F32)
        same_or_earlier_chunk = (m >> shift) <= (n >> shift)
        pos = lax.broadcasted_iota(jnp.int32, (blk, RET_KEY_DIM), 0).astype(F32)
        for h in range(heads):
            lg = lg_ref[h, 0:1, 0:1]
            st_ref[0, h] = s0_ref[0, h]
            decay_ref[h] = jnp.where(same_or_earlier_chunk, jnp.exp(lg * dist), 0.0)
            qdec_ref[h] = jnp.exp(lg * (pos + 1.0))
            kdec_ref[h] = jnp.exp(lg * (blk - 1.0 - pos))

    for h in range(heads):
        lg = lg_ref[h, 0:1, 0:1]
        keys = slice(h * RET_KEY_DIM, (h + 1) * RET_KEY_DIM)
        vals = slice(h * RET_VAL_DIM, (h + 1) * RET_VAL_DIM)
        q = q_ref[0, :, keys]
        k = k_ref[0, :, keys]
        if fused_vg:
            v = _dot(hn, w_ref[:, vals]).astype(BF16)
        else:
            v = v_ref[0, :, vals]
        state = st_ref[0, h]
        s = _dot_nt(q, k) * decay_ref[h]
        qd = (q.astype(F32) * qdec_ref[h]).astype(BF16)
        y = _dot(s.astype(BF16), v) + _dot(qd, state.astype(BF16))
        kd = k.astype(F32) * kdec_ref[h]
        st_ref[0, h] = jnp.exp(lg * blk) * state + _dot(kd.T.astype(BF16), v)

        yn = y * lax.rsqrt(jnp.mean(y * y, axis=-1, keepdims=True) + EPS) * gn_ref[h]
        if fused_vg:
            g = _dot(hn, w_ref[:, RET_V_WIDTH + h * RET_VAL_DIM:RET_V_WIDTH + (h + 1) * RET_VAL_DIM])
        else:
            g = g_ref[0, :, vals]
        o_ref[0, :, vals] = (g * jax.nn.sigmoid(g) * yn).astype(BF16)


def _retention(lg, q, k, vg, gn, s0, blk):
    b, t, _ = q.shape
    heads = RET_HEADS_PER_STEP
    fused_vg = len(vg) == 3
    assert not fused_vg or heads == N_RET_HEADS, "the fused projection expects every head in one step"
    qk = lambda i, h, j: (i, j, h)
    per_group = lambda i, h, j: (h, 0, 0)
    group_state = lambda i, h, j: (i, h, 0, 0)
    if fused_vg:
        vg_specs = [pl.BlockSpec((1, blk, D_MODEL), lambda i, h, j: (i, j, 0)),
                    pl.BlockSpec((1, D_MODEL), lambda i, h, j: (0, 0)),
                    _resident((D_MODEL, 2 * RET_V_WIDTH), lambda i, h, j: (0, 0))]
    else:
        vg_specs = [pl.BlockSpec((1, blk, heads * RET_VAL_DIM), qk), pl.BlockSpec((1, blk, heads * RET_VAL_DIM), qk)]
    return pl.pallas_call(
        functools.partial(_retention_kernel, blk=blk, heads=heads, fused_vg=fused_vg),
        grid=(b, N_RET_HEADS // heads, t // blk),
        in_specs=[pl.BlockSpec((heads, SUBLANES, LANES), per_group),
                  pl.BlockSpec((1, blk, heads * RET_KEY_DIM), qk), pl.BlockSpec((1, blk, heads * RET_KEY_DIM), qk),
                  *vg_specs,
                  pl.BlockSpec((heads, 1, RET_VAL_DIM), per_group),
                  pl.BlockSpec((1, heads, RET_KEY_DIM, RET_VAL_DIM), group_state)],
        out_specs=[pl.BlockSpec((1, blk, heads * RET_VAL_DIM), qk),
                   pl.BlockSpec((1, heads, RET_KEY_DIM, RET_VAL_DIM), group_state)],
        out_shape=[jax.ShapeDtypeStruct((b, t, RET_V_WIDTH), BF16),
                   jax.ShapeDtypeStruct((b, N_RET_HEADS, RET_KEY_DIM, RET_VAL_DIM), F32)],
        scratch_shapes=[pltpu.VMEM((heads, blk, blk), F32), pltpu.VMEM((heads, blk, RET_KEY_DIM), F32),
                        pltpu.VMEM((heads, blk, RET_KEY_DIM), F32)],
        compiler_params=_cparams(3), name="retention",
    )(lg, q, k, *vg, gn, s0)


def _mixer_kernel(x_ref, oa_ref, ob_ref, g_ref, wg_ref, wa_ref, wb_ref, wo_ref, x1_ref, *, oa_feature_major):
    x = x_ref[...]
    h = _rmsnorm(x, g_ref[...]).astype(BF16)
    gate_a = jax.nn.sigmoid(_dot(h, wg_ref[:, 0:D_MODEL]))
    gate_b = jax.nn.sigmoid(_dot(h, wg_ref[:, D_MODEL:2 * D_MODEL]))
    if oa_feature_major:
        ya = lax.dot_general(oa_ref[0], wa_ref[...], (((0,), (0,)), ((), ())), preferred_element_type=F32)
    else:
        ya = _dot(oa_ref[...], wa_ref[...])
    merged = gate_a * ya + gate_b * _dot(ob_ref[...], wb_ref[...])
    x1_ref[...] = x + _dot(merged.astype(BF16), wo_ref[...])


def _mixer(x, oa, ob, g, wg, wa, wb, wo):
    n = x.shape[0]
    tm = MIX_TOKEN_TILE
    tok = lambda i: (i, 0)
    fixed = lambda i: (0, 0)
    feature_major = oa.ndim == 3
    if feature_major:
        per_batch = oa.shape[2] // tm
        oa_spec = pl.BlockSpec((1, FOX_WIDTH, tm), lambda i: (i // per_batch, 0, i % per_batch))
    else:
        oa_spec = pl.BlockSpec((tm, FOX_WIDTH), tok)
    return pl.pallas_call(
        functools.partial(_mixer_kernel, oa_feature_major=feature_major),
        grid=(n // tm,),
        in_specs=[pl.BlockSpec((tm, D_MODEL), tok), oa_spec,
                  pl.BlockSpec((tm, RET_V_WIDTH), tok), pl.BlockSpec((1, D_MODEL), fixed),
                  _resident((D_MODEL, 2 * D_MODEL), fixed), _resident((FOX_WIDTH, D_MODEL), fixed),
                  _resident((RET_V_WIDTH, D_MODEL), fixed), _resident((D_MODEL, D_MODEL), fixed)],
        out_specs=pl.BlockSpec((tm, D_MODEL), tok),
        out_shape=jax.ShapeDtypeStruct((n, D_MODEL), F32),
        compiler_params=_cparams(1), name="mixer",
    )(x, oa, ob, g, wg, wa, wb, wo)


def _mlp_kernel(x_ref, g_ref, wu_ref, wd_ref, gf_ref, y_ref):
    x = x_ref[...]
    h = _rmsnorm(x, g_ref[...]).astype(BF16)
    acc = x
    for c in range(D_FF // D_MODEL):
        cols = slice(c * D_MODEL, (c + 1) * D_MODEL)
        u = jnp.square(jnp.maximum(_dot(h, wu_ref[:, cols]), 0.0)).astype(BF16)
        acc = acc + _dot(u, wd_ref[cols, :])
    y_ref[...] = _rmsnorm(acc, gf_ref[...])


def _mlp(x, g, wu, wd, gf):
    n = x.shape[0]
    tm = MIX_TOKEN_TILE
    tok = lambda i: (i, 0)
    fixed = lambda i: (0, 0)
    return pl.pallas_call(
        _mlp_kernel,
        grid=(n // tm,),
        in_specs=[pl.BlockSpec((tm, D_MODEL), tok), pl.BlockSpec((1, D_MODEL), fixed),
                  _resident((D_MODEL, D_FF), fixed), _resident((D_FF, D_MODEL), fixed),
                  pl.BlockSpec((1, D_MODEL), fixed)],
        out_specs=pl.BlockSpec((tm, D_MODEL), tok),
        out_shape=jax.ShapeDtypeStruct((n, D_MODEL), F32),
        compiler_params=_cparams(1), name="mlp",
    )(x, g, wu, wd, gf)


def _rope_tables(pos):
    inv_freq = ROPE_BASE ** (-jnp.arange(0, RET_KEY_DIM, 2, dtype=F32) / RET_KEY_DIM)
    ang = pos.astype(F32)[:, None] * inv_freq[None, :]
    return jnp.cos(ang), jnp.sin(ang)


def _after_attention(x, pos, o_a, params, ret_blk, state0, fuse_vg):
    b, t, _ = x.shape
    n = b * t
    xf = x.reshape(n, D_MODEL)
    cos, sin = _rope_tables(pos)
    reps = max(1, TOKEN_TILE // t)
    q_r, k_r = _proj_ret_qk(xf, params["g_attn"], params["w_ret_qk"], jnp.tile(cos, (reps, 1)),
                            jnp.tile(sin, (reps, 1)))
    if fuse_vg:
        vg = (x, params["g_attn"], params["w_ret_vg"])
    else:
        v_r, g_r = _proj_ret_vg(xf, params["g_attn"], params["w_ret_vg"])
        vg = (v_r.reshape(b, t, RET_V_WIDTH), g_r.reshape(b, t, RET_V_WIDTH))
    o_b, state = _retention(params["lg"], q_r.reshape(b, t, RET_QK_WIDTH), k_r.reshape(b, t, RET_QK_WIDTH),
                            vg, params["g_ret_norm"], state0, ret_blk)
    x1 = _mixer(xf, o_a, o_b.reshape(n, RET_V_WIDTH), params["g_attn"],
                params["w_gates"], params["w_branch_a"], params["w_branch_b"], params["w_out"])
    y = _mlp(x1, params["g_mlp"], params["w_up"], params["w_down"], params["g_final"])
    return y.reshape(b, t, D_MODEL), state[None]


def kernel(x_prompt, x_sample, cache_fox_k, cache_fox_v, cache_fox_logf, state_ret, g_attn, w_in, b_forget,
           g_ret_norm, w_branch, w_out, g_mlp, w_up, w_down, g_final):
    assert w_in.shape[0] == 1, "single-layer trunk"
    wi = w_in[0]
    fox_cols = 3 * FOX_WIDTH + N_FOX_HEADS
    qk_end = fox_cols + 2 * RET_QK_WIDTH
    vg_end = qk_end + 2 * RET_V_WIDTH
    lane_pad = LANES - N_FOX_HEADS
    lg = jnp.log(1.0 - 2.0 ** (-5.0 - jnp.arange(N_RET_HEADS, dtype=F32)))
    w_fox = jnp.pad(wi[:, :fox_cols], ((0, 0), (0, lane_pad))).astype(BF16)
    b_row = jnp.pad(b_forget[0], (0, lane_pad)).reshape(1, LANES)
    params = {
        "g_attn": g_attn[0].reshape(1, D_MODEL),
        "w_ret_qk": wi[:, fox_cols:qk_end].astype(BF16),
        "w_ret_vg": wi[:, qk_end:vg_end].astype(BF16),
        "w_gates": wi[:, vg_end:].astype(BF16),
        "lg": jnp.broadcast_to(lg[:, None, None], (N_RET_HEADS, SUBLANES, LANES)),
        "g_ret_norm": g_ret_norm[0].reshape(N_RET_HEADS, 1, RET_VAL_DIM),
        "w_branch_a": w_branch[0, :FOX_WIDTH].astype(BF16),
        "w_branch_b": w_branch[0, FOX_WIDTH:].astype(BF16),
        "w_out": w_out[0].astype(BF16),
        "g_mlp": g_mlp[0].reshape(1, D_MODEL),
        "w_up": w_up[0].astype(BF16),
        "w_down": w_down[0].astype(BF16),
        "g_final": g_final.reshape(1, D_MODEL),
    }
    bp, tp, _ = x_prompt.shape
    bs, ts, _ = x_sample.shape
    past = cache_fox_k.shape[2]

    qt, kt, vt, lft, lf_pad, q_norm, k_norm, kb = _proj_fox_t(
        x_prompt, params["g_attn"], wi[:, :fox_cols].T.astype(BF16), w_fox[:, 3 * FOX_WIDTH:],
        jnp.broadcast_to(b_forget[0][:, None], (N_FOX_HEADS, LANES)), b_row)
    c_pieces, c_end = _cumsum_pieces(lf_pad, CUMSUM_TILE, LOG2E)
    first = _first_key_block(c_end, q_norm, k_norm, tp, FOX_Q_BLOCK, FOX_BLOCK)
    o_a = _fox_prompt(first, qt, kb, vt, c_pieces)
    zero_state = jnp.zeros((bp, N_RET_HEADS, RET_KEY_DIM, RET_VAL_DIM), F32)
    yp, sp = _after_attention(x_prompt, jnp.arange(tp), o_a, params, RET_BLOCK, zero_state, True)
    to_heads = lambda a: jnp.transpose(a.reshape(bp, N_FOX_HEADS, FOX_HEAD_DIM, tp), (0, 3, 1, 2))[None]
    kp, vp = to_heads(kt), to_heads(vt)
    fp = jnp.transpose(lft, (0, 2, 1))[None]

    q, k, v, lf = _proj_fox(x_sample.reshape(bs * ts, D_MODEL), params["g_attn"], w_fox, b_row)
    feature_major = lambda a: jnp.transpose(a[0], (0, 2, 3, 1)).reshape(bs, FOX_WIDTH, past)
    lf_time = jnp.concatenate([jnp.transpose(cache_fox_logf[0], (0, 2, 1)),
                               jnp.transpose(lf.reshape(bs, ts, N_FOX_HEADS), (0, 2, 1))], axis=2)
    lf_time = jnp.pad(lf_time, ((0, 0), (0, 0), (0, -(past + ts) % LANES)))
    o_a = _fox_sample(q.reshape(bs, ts, FOX_WIDTH), k.reshape(bs, ts, FOX_WIDTH), v.reshape(bs, ts, FOX_WIDTH),
                      feature_major(cache_fox_k), feature_major(cache_fox_v), _cumsum_lanes(lf_time, LOG2E))
    ys, ss = _after_attention(x_sample, past + jnp.arange(ts), o_a.reshape(bs * ts, FOX_WIDTH), params, CHUNK,
                              state_ret[0], False)
    shape5 = (1, bs, ts, N_FOX_HEADS, FOX_HEAD_DIM)
    return (yp, ys, kp, vp, fp, sp, k.reshape(shape5), v.reshape(shape5),
            lf.reshape(1, bs, ts, N_FOX_HEADS), ss)
```
